```python
import math
import jax, jax.numpy as jnp
from jax import lax
import numpy as np

D_MODEL = 2048
BATCH = 16
SEQ = 2048
DEPTH = 2
DEC_BATCH = 8
DEC_SEQ = 16
PAST_LEN = 1024

CHUNK = 64
N_META = 16
EPS = 1e-5
SSD_EXPAND = 2
D_INNER = SSD_EXPAND * D_MODEL
SSD_HEAD_DIM = 64
SSD_HEADS = D_INNER // SSD_HEAD_DIM
SSD_GROUPS = 8
SSD_STATE = 128
SSD_CONV = 4
GN = SSD_GROUPS * SSD_STATE
CONV_DIM = D_INNER + 2 * GN
IN_PROJ_DIM = D_INNER + CONV_DIM + SSD_HEADS
WINDOW = 128
WINDOW_CHUNKS = WINDOW // CHUNK
ATTN_HEAD_DIM = 64
N_HEADS = D_MODEL // ATTN_HEAD_DIM
N_KV_HEADS = 8
Q_PER_KV = N_HEADS // N_KV_HEADS
QKV_DIM = (N_HEADS + 2 * N_KV_HEADS) * ATTN_HEAD_DIM
N_GROUPS = 4
EXPERTS_PER_GROUP = 8
N_EXPERTS = N_GROUPS * EXPERTS_PER_GROUP
TOP_K = 2
D_EXPERT = 512
MOE_BLOCK = 128
N_SSD_LAYERS = (DEPTH + 1) // 2
N_ATTN_LAYERS = DEPTH // 2

kernel_name = 'hybrid_ssd_swa_sink_hmoe_stream_step'

F32 = jnp.float32


def rmsnorm(x, g):
    xf = x.astype(F32)
    y = xf * lax.rsqrt(jnp.mean(xf * xf, axis=-1, keepdims=True) + EPS)
    return (y * g.astype(F32)).astype(x.dtype)


def causal_conv(xbc, conv_state, w, b):
    t_len = xbc.shape[1]
    xp = jnp.concatenate([conv_state.astype(xbc.dtype), xbc], axis=1)
    y = b.astype(xbc.dtype)
    for k in range(SSD_CONV):
        y = y + xp[:, k:k + t_len] * w[k].astype(xbc.dtype)
    return jax.nn.silu(y), xp[:, t_len:]


def ssd_inputs(u, w_in, conv_state, conv_w, conv_b, dt_bias):
    b, t_len, _ = u.shape
    zxbcdt = u @ w_in
    z = zxbcdt[..., :D_INNER]
    xbc = zxbcdt[..., D_INNER:D_INNER + CONV_DIM]
    dt = zxbcdt[..., D_INNER + CONV_DIM:]
    xbc, new_conv = causal_conv(xbc, conv_state, conv_w, conv_b)
    xs = xbc[..., :D_INNER].astype(F32).reshape(b, t_len, SSD_HEADS, SSD_HEAD_DIM)
    bm = xbc[..., D_INNER:D_INNER + GN].astype(F32).reshape(b, t_len, SSD_GROUPS, SSD_STATE)
    cm = xbc[..., D_INNER + GN:].astype(F32).reshape(b, t_len, SSD_GROUPS, SSD_STATE)
    dt = jax.nn.softplus(dt.astype(F32) + dt_bias.astype(F32))
    return z, xs, bm, cm, dt, new_conv


def ssd_block(x, dt, bm, cm, a, h0):
    b, blk = x.shape[0], x.shape[1]
    hg = SSD_HEADS // SSD_GROUPS
    cum = jnp.cumsum(dt * a, axis=1)
    causal = jnp.tril(jnp.ones((blk, blk), dtype=bool))[None, :, :, None]
    decay = jnp.exp(jnp.where(causal, cum[:, :, None, :] - cum[:, None, :, :], -jnp.inf))
    cb = jnp.einsum('blgn,bsgn->blsg', cm, bm)
    w = decay.reshape(b, blk, blk, SSD_GROUPS, hg) * cb[..., None] * dt.reshape(b, 1, blk, SSD_GROUPS, hg)
    xg = x.reshape(b, blk, SSD_GROUPS, hg, SSD_HEAD_DIM)
    y_diag = jnp.einsum('blsgh,bsghp->blghp', w, xg)
    h0g = h0.reshape(b, SSD_GROUPS, hg, SSD_HEAD_DIM, SSD_STATE)
    y_off = jnp.einsum('blgn,bghpn->blghp', cm, h0g) * jnp.exp(cum).reshape(b, blk, SSD_GROUPS, hg)[..., None]
    y = (y_diag + y_off).reshape(b, blk, SSD_HEADS, SSD_HEAD_DIM)
    tail = (jnp.exp(cum[:, -1:] - cum) * dt).reshape(b, blk, SSD_GROUPS, hg)
    h_in = jnp.einsum('bsgn,bsghp->bghpn', bm, xg * tail[..., None])
    h_new = h0 * jnp.exp(cum[:, -1])[:, :, None, None] + h_in.reshape(b, SSD_HEADS, SSD_HEAD_DIM, SSD_STATE)
    return y, h_new


def ssd_output(y, xs, z, d_skip, norm_w, w_out, dtype):
    b, t_len = y.shape[0], y.shape[1]
    y = y + xs * d_skip.astype(F32)[:, None]
    y = y.reshape(b, t_len, D_INNER) * jax.nn.silu(z.astype(F32))
    yg = y.reshape(b, t_len, SSD_GROUPS, D_INNER // SSD_GROUPS)
    yg = yg * lax.rsqrt(jnp.mean(yg * yg, axis=-1, keepdims=True) + EPS)
    y = yg.reshape(b, t_len, D_INNER) * norm_w.astype(F32)
    return y.astype(dtype) @ w_out


def ssd_prompt(u, w_in, conv_w, conv_b, dt_bias, a_log, d_skip, norm_w, w_out):
    b, t_len, _ = u.shape
    s_len = t_len - N_META
    n_chunks = s_len // CHUNK
    conv0 = jnp.zeros((b, SSD_CONV - 1, CONV_DIM), u.dtype)
    z, xs, bm, cm, dt, conv_new = ssd_inputs(u, w_in, conv0, conv_w, conv_b, dt_bias)
    a = -jnp.exp(a_log.astype(F32))
    h0 = jnp.zeros((b, SSD_HEADS, SSD_HEAD_DIM, SSD_STATE), F32)
    y_meta, h_meta = ssd_block(xs[:, :N_META], dt[:, :N_META], bm[:, :N_META], cm[:, :N_META], a, h0)

    def chunks(t):
        r = t[:, N_META:]
        return jnp.swapaxes(r.reshape((b, n_chunks, CHUNK) + r.shape[2:]), 0, 1)

    def step(h, inp):
        xc, dtc, bc, cc = inp
        yc, hn = ssd_block(xc, dtc, bc, cc, a, h)
        return hn, yc

    h_fin, y_chunks = lax.scan(step, h_meta, (chunks(xs), chunks(dt), chunks(bm), chunks(cm)))
    y_real = jnp.swapaxes(y_chunks, 0, 1).reshape(b, s_len, SSD_HEADS, SSD_HEAD_DIM)
    y = jnp.concatenate([y_meta, y_real], axis=1)
    out = ssd_output(y, xs, z, d_skip, norm_w, w_out, u.dtype)
    return out, h_fin.astype(u.dtype), conv_new


def ssd_sample(u, ssm_state, conv_state, w_in, conv_w, conv_b, dt_bias, a_log, d_skip, norm_w, w_out):
    z, xs, bm, cm, dt, conv_new = ssd_inputs(u, w_in, conv_state, conv_w, conv_b, dt_bias)
    a = -jnp.exp(a_log.astype(F32))
    y, h_new = ssd_block(xs, dt, bm, cm, a, ssm_state.astype(F32))
    out = ssd_output(y, xs, z, d_skip, norm_w, w_out, u.dtype)
    return out, h_new.astype(u.dtype), conv_new


def sink_softmax(s, sinks):
    sink = sinks.astype(F32).reshape((1, N_KV_HEADS, Q_PER_KV) + (1,) * (s.ndim - 3))
    sink = jnp.broadcast_to(sink, s.shape[:-1] + (1,))
    return jax.nn.softmax(jnp.concatenate([s, sink], axis=-1), axis=-1)[..., :-1]


def split_qkv(u, w_qkv):
    b, t_len, _ = u.shape
    qkv = u @ w_qkv
    nq = N_HEADS * ATTN_HEAD_DIM
    nk = N_KV_HEADS * ATTN_HEAD_DIM
    q = qkv[..., :nq].reshape(b, t_len, N_KV_HEADS, Q_PER_KV, ATTN_HEAD_DIM)
    k = qkv[..., nq:nq + nk].reshape(b, t_len, N_KV_HEADS, ATTN_HEAD_DIM)
    v = qkv[..., nq + nk:].reshape(b, t_len, N_KV_HEADS, ATTN_HEAD_DIM)
    return q, k, v


def attn_prompt(u, w_qkv, sinks, w_out):
    b, t_len, _ = u.shape
    s_len = t_len - N_META
    n_chunks = s_len // CHUNK
    scale = ATTN_HEAD_DIM ** -0.5
    q, k, v = split_qkv(u, w_qkv)
    km, vm = k[:, :N_META], v[:, :N_META]
    s_mm = jnp.einsum('bqkrd,bjkd->bkrqj', q[:, :N_META], km).astype(F32) * scale
    o_meta = jnp.einsum('bkrqj,bjkd->bqkrd', sink_softmax(s_mm, sinks), vm)
    o_meta = o_meta.reshape(b, N_META, N_HEADS * ATTN_HEAD_DIM)
    qr = q[:, N_META:].reshape(b, n_chunks, CHUNK, N_KV_HEADS, Q_PER_KV, ATTN_HEAD_DIM)
    kr = k[:, N_META:].reshape(b, n_chunks, CHUNK, N_KV_HEADS, ATTN_HEAD_DIM)
    vr = v[:, N_META:].reshape(b, n_chunks, CHUNK, N_KV_HEADS, ATTN_HEAD_DIM)

    def band(t):
        tp = jnp.pad(t, ((0, 0), (WINDOW_CHUNKS, 0), (0, 0), (0, 0), (0, 0)))
        return jnp.concatenate([tp[:, i:i + n_chunks] for i in range(WINDOW_CHUNKS + 1)], axis=2)

    kb, vb = band(kr), band(vr)
    s_band = jnp.einsum('bcqkrd,bcjkd->bkrcqj', qr, kb).astype(F32) * scale
    chunk_off = jnp.arange((WINDOW_CHUNKS + 1) * CHUNK) // CHUNK - WINDOW_CHUNKS
    valid = (jnp.arange(n_chunks)[:, None] + chunk_off[None, :]) >= 0
    s_band = jnp.where(valid[:, None, :], s_band, -jnp.inf)
    s_meta = jnp.einsum('bcqkrd,bjkd->bkrcqj', qr, km).astype(F32) * scale
    p = sink_softmax(jnp.concatenate([s_meta, s_band], axis=-1), sinks)
    o = (jnp.einsum('bkrcqj,bcjkd->bcqkrd', p[..., N_META:], vb)
         + jnp.einsum('bkrcqj,bjkd->bcqkrd', p[..., :N_META], vm))
    o_real = o.reshape(b, s_len, N_HEADS * ATTN_HEAD_DIM)
    o_all = jnp.concatenate([o_meta, o_real], axis=1).astype(u.dtype)
    return o_all @ w_out, k[:, -WINDOW:], v[:, -WINDOW:], km, vm


def attn_sample(u, win_k, win_v, meta_k, meta_v, w_qkv, sinks, w_out):
    b, t_len, _ = u.shape
    scale = ATTN_HEAD_DIM ** -0.5
    q, k, v = split_qkv(u, w_qkv)
    kk = jnp.concatenate([meta_k.astype(u.dtype), win_k.astype(u.dtype), k], axis=1)
    vv = jnp.concatenate([meta_v.astype(u.dtype), win_v.astype(u.dtype), v], axis=1)
    s = jnp.einsum('bqkrd,bjkd->bkrqj', q, kk).astype(F32) * scale
    o = jnp.einsum('bkrqj,bjkd->bqkrd', sink_softmax(s, sinks), vv)
    o = o.reshape(b, t_len, N_HEADS * ATTN_HEAD_DIM).astype(u.dtype)
    return o @ w_out, k, v


def moe_ffn(h, w_group, b_group, w_router, b_router, w_in, w_out):
    shape = h.shape
    x2 = h.reshape(-1, D_MODEL)
    n_tok = x2.shape[0]
    pg = jax.nn.softmax((x2 @ w_group).astype(F32) + b_group.astype(F32), axis=-1)
    g = jnp.argmax(pg, axis=-1)
    gate_g = jnp.max(pg, axis=-1)
    el = ((x2 @ w_router).astype(F32) + b_router.astype(F32)).reshape(n_tok, N_GROUPS, EXPERTS_PER_GROUP)
    el = jnp.take_along_axis(el, g[:, None, None], axis=1)[:, 0]
    top_v, top_i = lax.top_k(el, TOP_K)
    wts = (jax.nn.softmax(top_v, axis=-1) * gate_g[:, None]).reshape(-1)
    eid = (g[:, None] * EXPERTS_PER_GROUP + top_i).reshape(-1).astype(jnp.int32)
    tok = jnp.repeat(jnp.arange(n_tok, dtype=jnp.int32), TOP_K)
    n_assign = n_tok * TOP_K
    order = jnp.argsort(eid)
    se, st, sw = eid[order], tok[order], wts[order]
    counts = jnp.bincount(eid, length=N_EXPERTS)
    starts = jnp.cumsum(counts) - counts
    padded = (counts + MOE_BLOCK - 1) // MOE_BLOCK * MOE_BLOCK
    pend = jnp.cumsum(padded)
    pstarts = pend - padded
    dest = pstarts[se] + jnp.arange(n_assign, dtype=jnp.int32) - starts[se]
    n_blocks = -(-(n_assign + N_EXPERTS * (MOE_BLOCK - 1)) // MOE_BLOCK)
    n_rows = n_blocks * MOE_BLOCK
    row_tok = jnp.zeros((n_rows,), jnp.int32).at[dest].set(st)
    row_w = jnp.zeros((n_rows,), F32).at[dest].set(sw)
    blk_start = jnp.arange(n_blocks, dtype=jnp.int32) * MOE_BLOCK
    blk_e = jnp.minimum(jnp.searchsorted(pend, blk_start, side='right'), N_EXPERTS - 1)

    def block_fn(args):
        tk, wt, e = args
        xb = x2[tk]
        gu = xb @ w_in[e]
        hmid = jax.nn.silu(gu[:, :D_EXPERT]) * gu[:, D_EXPERT:]
        return (hmid @ w_out[e]).astype(F32) * wt[:, None]

    outs = lax.map(block_fn, (row_tok.reshape(n_blocks, MOE_BLOCK), row_w.reshape(n_blocks, MOE_BLOCK), blk_e))
    y = jnp.zeros((n_tok, D_MODEL), F32).at[row_tok].add(outs.reshape(n_rows, D_MODEL))
    return y.astype(h.dtype).reshape(shape)


def setup_inputs(seed: int = 0) -> dict:
    key = jax.random.key(seed)
    ks = iter(jax.random.split(key, 32))

    def nrm(shape, scale):
        return jax.random.normal(next(ks), shape, F32) * scale

    dt0 = jnp.exp(jax.random.uniform(next(ks), (N_SSD_LAYERS, SSD_HEADS), F32,
                                     minval=math.log(1e-3), maxval=math.log(1e-1)))
    dt_bias = dt0 + jnp.log(-jnp.expm1(-dt0))
    a_log = jnp.log(jax.random.uniform(next(ks), (N_SSD_LAYERS, SSD_HEADS), F32, minval=1.0, maxval=16.0))
    kv_win = (N_ATTN_LAYERS, DEC_BATCH, WINDOW, N_KV_HEADS, ATTN_HEAD_DIM)
    kv_meta = (N_ATTN_LAYERS, DEC_BATCH, N_META, N_KV_HEADS, ATTN_HEAD_DIM)
    return {
        'x_prompt': nrm((BATCH, SEQ, D_MODEL), 1.0),
        'x_sample': nrm((DEC_BATCH, DEC_SEQ, D_MODEL), 1.0),
        'cache_ssm_state': nrm((N_SSD_LAYERS, DEC_BATCH, SSD_HEADS, SSD_HEAD_DIM, SSD_STATE), 0.3),
        'cache_ssm_conv': nrm((N_SSD_LAYERS, DEC_BATCH, SSD_CONV - 1, CONV_DIM), 1.0),
        'cache_attn_k': nrm(kv_win, 1.0),
        'cache_attn_v': nrm(kv_win, 1.0),
        'cache_meta_k': nrm(kv_meta, 1.0),
        'cache_meta_v': nrm(kv_meta, 1.0),
        'meta_tokens': nrm((N_META, D_MODEL), 1.0),
        'norm_mix': 1.0 + nrm((DEPTH, D_MODEL), 0.02),
        'norm_ffn': 1.0 + nrm((DEPTH, D_MODEL), 0.02),
        'norm_final': 1.0 + nrm((D_MODEL,), 0.02),
        'ssd_w_in': nrm((N_SSD_LAYERS, D_MODEL, IN_PROJ_DIM), D_MODEL ** -0.5),
        'ssd_conv_w': nrm((N_SSD_LAYERS, SSD_CONV, CONV_DIM), SSD_CONV ** -0.5),
        'ssd_conv_b': nrm((N_SSD_LAYERS, CONV_DIM), 0.02),
        'ssd_dt_bias': dt_bias,
        'ssd_a_log': a_log,
        'ssd_d': 1.0 + nrm((N_SSD_LAYERS, SSD_HEADS), 0.1),
        'ssd_norm': 1.0 + nrm((N_SSD_LAYERS, D_INNER), 0.02),
        'ssd_w_out': nrm((N_SSD_LAYERS, D_INNER, D_MODEL), D_INNER ** -0.5),
        'attn_w_qkv': nrm((N_ATTN_LAYERS, D_MODEL, QKV_DIM), D_MODEL ** -0.5),
        'attn_sinks': nrm((N_ATTN_LAYERS, N_HEADS), 1.0),
        'attn_w_out': nrm((N_ATTN_LAYERS, N_HEADS * ATTN_HEAD_DIM, D_MODEL), (N_HEADS * ATTN_HEAD_DIM) ** -0.5),
        'moe_w_group': nrm((DEPTH, D_MODEL, N_GROUPS), D_MODEL ** -0.5),
        'moe_b_group': nrm((DEPTH, N_GROUPS), 0.01),
        'moe_w_router': nrm((DEPTH, D_MODEL, N_EXPERTS), D_MODEL ** -0.5),
        'moe_b_router': nrm((DEPTH, N_EXPERTS), 0.01),
        'moe_w_in': nrm((DEPTH, N_EXPERTS, D_MODEL, 2 * D_EXPERT), D_MODEL ** -0.5),
        'moe_w_out': nrm((DEPTH, N_EXPERTS, D_EXPERT, D_MODEL), D_EXPERT ** -0.5),
    }


def reference(x_prompt, x_sample, cache_ssm_state, cache_ssm_conv, cache_attn_k, cache_attn_v,
              cache_meta_k, cache_meta_v, meta_tokens, norm_mix, norm_ffn, norm_final,
              ssd_w_in, ssd_conv_w, ssd_conv_b, ssd_dt_bias, ssd_a_log, ssd_d, ssd_norm, ssd_w_out,
              attn_w_qkv, attn_sinks, attn_w_out,
              moe_w_group, moe_b_group, moe_w_router, moe_b_router, moe_w_in, moe_w_out):
    b = x_prompt.shape[0]
    meta = jnp.broadcast_to(meta_tokens.astype(x_prompt.dtype)[None], (b, N_META, D_MODEL))
    hp = jnp.concatenate([meta, x_prompt], axis=1)
    hs = x_sample
    ssm_p, conv_p, k_p, v_p, mk_p, mv_p = [], [], [], [], [], []
    ssm_s, conv_s, k_s, v_s = [], [], [], []
    for i in range(DEPTH):
        j = i // 2
        up = rmsnorm(hp, norm_mix[i])
        us = rmsnorm(hs, norm_mix[i])
        if i % 2 == 0:
            mp, st, cv = ssd_prompt(up, ssd_w_in[j], ssd_conv_w[j], ssd_conv_b[j], ssd_dt_bias[j],
                                    ssd_a_log[j], ssd_d[j], ssd_norm[j], ssd_w_out[j])
            ms, st2, cv2 = ssd_sample(us, cache_ssm_state[j], cache_ssm_conv[j], ssd_w_in[j], ssd_conv_w[j],
                                      ssd_conv_b[j], ssd_dt_bias[j], ssd_a_log[j], ssd_d[j], ssd_norm[j],
                                      ssd_w_out[j])
            ssm_p.append(st); conv_p.append(cv); ssm_s.append(st2); conv_s.append(cv2)
        else:
            mp, kw, vw, km, vm = attn_prompt(up, attn_w_qkv[j], attn_sinks[j], attn_w_out[j])
            ms, kn, vn = attn_sample(us, cache_attn_k[j], cache_attn_v[j], cache_meta_k[j], cache_meta_v[j],
                                     attn_w_qkv[j], attn_sinks[j], attn_w_out[j])
            k_p.append(kw); v_p.append(vw); mk_p.append(km); mv_p.append(vm)
            k_s.append(kn); v_s.append(vn)
        hp = hp + mp
        hs = hs + ms
        hp = hp + moe_ffn(rmsnorm(hp, norm_ffn[i]), moe_w_group[i], moe_b_group[i], moe_w_router[i],
                          moe_b_router[i], moe_w_in[i], moe_w_out[i])
        hs = hs + moe_ffn(rmsnorm(hs, norm_ffn[i]), moe_w_group[i], moe_b_group[i], moe_w_router[i],
                          moe_b_router[i], moe_w_in[i], moe_w_out[i])
    y_prompt = rmsnorm(hp, norm_final)[:, N_META:]
    y_sample = rmsnorm(hs, norm_final)
    return (y_prompt, y_sample,
            jnp.stack(ssm_p), jnp.stack(conv_p), jnp.stack(k_p), jnp.stack(v_p), jnp.stack(mk_p), jnp.stack(mv_p),
            jnp.stack(ssm_s), jnp.stack(conv_s), jnp.stack(k_s), jnp.stack(v_s))
```

```python
import functools

import jax
import jax.numpy as jnp
from jax import lax
from jax.experimental import pallas as pl
from jax.experimental.pallas import tpu as pltpu

F32, BF16, I32 = jnp.float32, jnp.bfloat16, jnp.int32

D_MODEL = 2048
N_META = 16
EPS = 1e-5
D_INNER = 4096
SSD_HEAD_DIM = 64
SSD_HEADS = 64
SSD_GROUPS = 8
SSD_STATE = 128
SSD_CONV = 4
GN = SSD_GROUPS * SSD_STATE
CONV_DIM = D_INNER + 2 * GN
ZX_DIM = D_INNER + CONV_DIM
GROUP_W = D_INNER // SSD_GROUPS
WINDOW = 128
CHUNK = 64
ATTN_HEAD_DIM = 64
N_HEADS = 32
N_KV_HEADS = 8
Q_PER_KV = N_HEADS // N_KV_HEADS
Q_DIM = N_HEADS * ATTN_HEAD_DIM
KV_DIM = N_KV_HEADS * ATTN_HEAD_DIM
N_GROUPS = 4
EXPERTS_PER_GROUP = 8
N_EXPERTS = 32
D_EXPERT = 512

LANES = 128
SSD_BLOCK = 128
NEG = -1e30
VMEM_LIMIT = 56 * 1024 * 1024
MOE_TM = 256


def _cparams(sem):
    return pltpu.CompilerParams(dimension_semantics=sem, vmem_limit_bytes=VMEM_LIMIT)


def _sigmoid(x):
    return 1.0 / (1.0 + jnp.exp(-x))


def _rms(h, g):
    var = jnp.mean(h * h, axis=-1, keepdims=True)
    return h * lax.rsqrt(var + EPS) * g


def _row_chunk(tm):
    for rc in (256, 128, 64, 32, 16, 8):
        if tm % rc == 0:
            return rc
    raise ValueError(tm)


_N_MOE_REFS = {"none": 0, "sum": 1, "pair": 3}


def _combine_rows(x_ref, moe_refs, mode, rows):
    h = x_ref[rows, :]
    if mode == "sum":
        h = h + moe_refs[0][rows, :]
    elif mode == "pair":
        y0, y1, w = moe_refs
        wv = w[rows, :]
        h = h + (y0[rows, :] * wv[:, 0:1] + y1[rows, :] * wv[:, 1:2])
    return h


def _norm_mm_kernel(*refs, mode, write_h, has_extra, tm):
    it = iter(refs)
    x_ref = next(it)
    moe_refs = [next(it) for _ in range(_N_MOE_REFS[mode])]
    g_ref, w_ref = next(it), next(it)
    we_ref = next(it) if has_extra else None
    o_ref = next(it)
    h_ref = next(it) if write_h else None
    e_ref = next(it) if has_extra else None
    xn_ref = next(it)
    rc = _row_chunk(tm)

    @pl.when(pl.program_id(1) == 0)
    def _():
        def body(r, carry):
            rows = pl.ds(pl.multiple_of(r * rc, rc), rc)
            h = _combine_rows(x_ref, moe_refs, mode, rows)
            if write_h:
                h_ref[rows, :] = h
            xn_ref[rows, :] = _rms(h, g_ref[...]).astype(BF16)
            return carry

        lax.fori_loop(0, tm // rc, body, 0)
        if has_extra:
            e_ref[...] = jnp.dot(xn_ref[...], we_ref[...], preferred_element_type=F32)

    o_ref[...] = jnp.dot(xn_ref[...], w_ref[...], preferred_element_type=F32)


def _moe_in_specs(mode, moe, tm, n_rows):
    if mode == "none":
        return [], []
    if mode == "sum":
        return [pl.BlockSpec((tm, D_MODEL), lambda i, j: (i, 0))], [moe[0]]
    y2, wts = moe
    off = n_rows // tm
    specs = [pl.BlockSpec((tm, D_MODEL), lambda i, j: (i, 0)),
             pl.BlockSpec((tm, D_MODEL), lambda i, j: (i + off, 0)),
             pl.BlockSpec((tm, LANES), lambda i, j: (i, 0))]
    return specs, [y2, y2, wts]


def norm_matmul(x, g, w, *, tm, tn, mode="none", moe=(), w_extra=None, write_h=False):
    n_rows, n_out = x.shape[0], w.shape[1]
    assert n_rows % tm == 0 and n_out % tn == 0
    moe_specs, moe_ops = _moe_in_specs(mode, moe, tm, n_rows)
    in_specs = [pl.BlockSpec((tm, D_MODEL), lambda i, j: (i, 0))] + moe_specs + [
        pl.BlockSpec((1, D_MODEL), lambda i, j: (0, 0)),
        pl.BlockSpec((D_MODEL, tn), lambda i, j: (0, j))]
    operands = [x] + moe_ops + [g.reshape(1, D_MODEL), w]
    out_shape = [jax.ShapeDtypeStruct((n_rows, n_out), F32)]
    out_specs = [pl.BlockSpec((tm, tn), lambda i, j: (i, j))]
    if w_extra is not None:
        in_specs.append(pl.BlockSpec((D_MODEL, LANES), lambda i, j: (0, 0)))
        operands.append(w_extra)
    if write_h:
        out_shape.append(jax.ShapeDtypeStruct((n_rows, D_MODEL), F32))
        out_specs.append(pl.BlockSpec((tm, D_MODEL), lambda i, j: (i, 0)))
    if w_extra is not None:
        out_shape.append(jax.ShapeDtypeStruct((n_rows, LANES), F32))
        out_specs.append(pl.BlockSpec((tm, LANES), lambda i, j: (i, 0)))
    kern = functools.partial(_norm_mm_kernel, mode=mode, write_h=write_h,
                             has_extra=w_extra is not None, tm=tm)
    return pl.pallas_call(
        kern, grid=(n_rows // tm, n_out // tn), in_specs=in_specs, out_specs=out_specs,
        out_shape=out_shape, scratch_shapes=[pltpu.VMEM((tm, D_MODEL), BF16)],
        compiler_params=_cparams(("parallel", "arbitrary")), name="norm_matmul")(*operands)


def _final_norm_kernel(*refs, mode, tm):
    x_ref = refs[0]
    moe_refs = refs[1:1 + _N_MOE_REFS[mode]]
    g_ref, o_ref = refs[-2], refs[-1]
    rc = _row_chunk(tm)

    def body(r, carry):
        rows = pl.ds(pl.multiple_of(r * rc, rc), rc)
        o_ref[rows, :] = _rms(_combine_rows(x_ref, moe_refs, mode, rows), g_ref[...])
        return carry

    lax.fori_loop(0, tm // rc, body, 0)


def final_norm(x, g, *, tm, mode, moe):
    n_rows = x.shape[0]
    moe_specs, moe_ops = _moe_in_specs(mode, moe, tm, n_rows)
    in_specs = [pl.BlockSpec((tm, D_MODEL), lambda i, j: (i, 0))] + moe_specs + [
        pl.BlockSpec((1, D_MODEL), lambda i, j: (0, 0))]
    return pl.pallas_call(
        functools.partial(_final_norm_kernel, mode=mode, tm=tm), grid=(n_rows // tm, 1),
        in_specs=in_specs, out_specs=pl.BlockSpec((tm, D_MODEL), lambda i, j: (i, 0)),
        out_shape=jax.ShapeDtypeStruct((n_rows, D_MODEL), F32),
        compiler_params=_cparams(("parallel", "arbitrary")), name="final_norm")(
            x, *moe_ops, g.reshape(1, D_MODEL))


def _route(lg):
    lane = lax.broadcasted_iota(I32, lg.shape, 1)
    is_g = lane < N_GROUPS
    gl = jnp.where(is_g, lg, NEG)
    gmax = jnp.max(gl, axis=-1, keepdims=True)
    gsum = jnp.sum(jnp.where(is_g, jnp.exp(gl - gmax), 0.0), axis=-1, keepdims=True)
    gate = 1.0 / gsum
    gidx = jnp.min(jnp.where(gl == gmax, lane, LANES), axis=-1, keepdims=True)
    ecol = lane - N_GROUPS
    in_group = (ecol >= 0) & (ecol < N_EXPERTS) & ((ecol >> 3) == gidx)
    el = jnp.where(in_group, lg, NEG)
    v1 = jnp.max(el, axis=-1, keepdims=True)
    i1 = jnp.min(jnp.where(el == v1, lane, LANES), axis=-1, keepdims=True)
    el2 = jnp.where(lane == i1, NEG, el)
    v2 = jnp.max(el2, axis=-1, keepdims=True)
    i2 = jnp.min(jnp.where(el2 == v2, lane, LANES), axis=-1, keepdims=True)
    e2 = jnp.exp(v2 - v1)
    w1 = gate / (1.0 + e2)
    w2 = gate * e2 / (1.0 + e2)
    eid = jnp.where(lane == 0, i1 - N_GROUPS, jnp.where(lane == 1, i2 - N_GROUPS, 0))
    wts = jnp.where(lane == 0, w1, jnp.where(lane == 1, w2, 0.0))
    return eid, wts


def _mm_route_kernel(a_ref, w_ref, h_ref, g_ref, wr_ref, br_ref,
                     h1_ref, xn_ref, e_ref, p_ref, acc_ref, *, nk, tm):
    k = pl.program_id(1)
    rc = _row_chunk(tm)

    @pl.when(k == 0)
    def _():
        acc_ref[...] = jnp.zeros_like(acc_ref)

    acc_ref[...] += jnp.dot(a_ref[...], w_ref[...], preferred_element_type=F32)

    @pl.when(k == nk - 1)
    def _():
        def body(r, carry):
            rows = pl.ds(pl.multiple_of(r * rc, rc), rc)
            h1 = h_ref[rows, :] + acc_ref[rows, :]
            h1_ref[rows, :] = h1
            xn = _rms(h1, g_ref[...])
            xn_ref[rows, :] = xn
            lg = jnp.dot(xn.astype(BF16), wr_ref[...], preferred_element_type=F32) + br_ref[...]
            eid, wts = _route(lg)
            e_ref[rows, :] = eid
            p_ref[rows, :] = wts
            return carry

        lax.fori_loop(0, tm // rc, body, 0)


def matmul_route(a, w, h, g, wr, br, *, tm, tk):
    n_rows, kdim = a.shape
    assert n_rows % tm == 0 and kdim % tk == 0
    nk = kdim // tk
    row_spec = pl.BlockSpec((tm, D_MODEL), lambda i, k: (i, 0))
    lane_spec = pl.BlockSpec((tm, LANES), lambda i, k: (i, 0))
    return pl.pallas_call(
        functools.partial(_mm_route_kernel, nk=nk, tm=tm), grid=(n_rows // tm, nk),
        in_specs=[pl.BlockSpec((tm, tk), lambda i, k: (i, k)),
                  pl.BlockSpec((tk, D_MODEL), lambda i, k: (k, 0)),
                  row_spec,
                  pl.BlockSpec((1, D_MODEL), lambda i, k: (0, 0)),
                  pl.BlockSpec((D_MODEL, LANES), lambda i, k: (0, 0)),
                  pl.BlockSpec((1, LANES), lambda i, k: (0, 0))],
        out_specs=[row_spec, row_spec, lane_spec, lane_spec],
        out_shape=[jax.ShapeDtypeStruct((n_rows, D_MODEL), F32),
                   jax.ShapeDtypeStruct((n_rows, D_MODEL), F32),
                   jax.ShapeDtypeStruct((n_rows, LANES), I32),
                   jax.ShapeDtypeStruct((n_rows, LANES), F32)],
        scratch_shapes=[pltpu.VMEM((tm, D_MODEL), F32)],
        compiler_params=_cparams(("parallel", "arbitrary")), name="matmul_route")(
            a, w, h, g.reshape(1, D_MODEL), wr, br)


def _split3(v):
    v1 = v.astype(BF16)
    r1 = v - v1.astype(F32)
    v2 = r1.astype(BF16)
    v3 = (r1 - v2.astype(F32)).astype(BF16)
    return jnp.concatenate([v1, v2, v3], axis=1)


def _ssd_kernel(z_ref, x_ref, b_ref, c_ref, dt_ref, cw_ref, cb_ref, dtb_ref, alog_ref, dsk_ref,
                nw_ref, e3_ref, s0_ref, c0_ref, y_ref, so_ref,
                state_ref, xp_ref, cumt_ref, dtt_ref, ecum_ref, edt_ref, *, valid_len, n_chunks):
    blk = SSD_BLOCK
    c = pl.program_id(1)

    @pl.when(c == 0)
    def _():
        state_ref[...] = s0_ref[0].T
        xp_ref[0:8, :] = c0_ref[0]

    @pl.when(c > 0)
    def _():
        xp_ref[0:8, :] = xp_ref[blk:blk + 8, :]

    xp_ref[8:8 + blk, 0:D_INNER] = x_ref[...]
    xp_ref[8:8 + blk, D_INNER:D_INNER + GN] = b_ref[...]
    xp_ref[8:8 + blk, D_INNER + GN:CONV_DIM] = c_ref[...]

    row = lax.broadcasted_iota(I32, (blk, LANES), 0)
    lane = lax.broadcasted_iota(I32, (blk, LANES), 1)
    dtr = dt_ref[...] + dtb_ref[...]
    dt = jnp.maximum(dtr, 0.0) + jnp.log1p(jnp.exp(-jnp.abs(dtr)))
    if valid_len < blk:
        dt = jnp.where(row < valid_len, dt, 0.0)
    cum = dt * (-jnp.exp(alog_ref[...]))
    sh = 1
    while sh < blk:
        cum = cum + jnp.where(row >= sh, pltpu.roll(cum, sh, 0), 0.0)
        sh *= 2
    cumt_ref[...] = cum.T
    dtt_ref[...] = dt.T
    ecum_ref[...] = jnp.dot(_split3(cum), e3_ref[...], preferred_element_type=F32)
    edt_ref[...] = jnp.dot(_split3(dt), e3_ref[...], preferred_element_type=F32)

    causal = lane <= row
    left = lane < SSD_HEAD_DIM

    def conv(cols):
        acc = cb_ref[:, cols]
        for k in range(SSD_CONV):
            acc = acc + xp_ref[pl.ds(8 - (SSD_CONV - 1) + k, blk), cols] * cw_ref[k:k + 1, cols]
        return acc * _sigmoid(acc)

    def group(g, carry):
        cols = pl.ds(pl.multiple_of(g * GROUP_W, GROUP_W), GROUP_W)
        xg = conv(cols)
        bg = conv(pl.ds(pl.multiple_of(D_INNER + g * SSD_STATE, SSD_STATE), SSD_STATE))
        cg = conv(pl.ds(pl.multiple_of(D_INNER + GN + g * SSD_STATE, SSD_STATE), SSD_STATE))
        cb16, bb16 = cg.astype(BF16), bg.astype(BF16)
        cbm = lax.dot_general(cb16, bb16, (((1,), (1,)), ((), ())), preferred_element_type=F32)
        sg = state_ref[:, cols]
        ecg = ecum_ref[:, cols]
        y_off = jnp.dot(cb16, sg.astype(BF16), preferred_element_type=F32) * jnp.exp(ecg)
        parts = []
        for pr in range(GROUP_W // LANES):
            xpair = xg[:, pr * LANES:(pr + 1) * LANES]
            ec = ecg[:, pr * LANES:(pr + 1) * LANES]
            ec_sw = pltpu.roll(ec, SSD_HEAD_DIM, 1)
            acc = jnp.zeros((blk, LANES), F32)
            for q in range(2):
                head = g * (GROUP_W // SSD_HEAD_DIM) + pr * 2 + q
                cum_l = jnp.where(left, ec, ec_sw) if q == 0 else jnp.where(left, ec_sw, ec)
                cum_s = cumt_ref[pl.ds(head, 1), :]
                decay = jnp.exp(jnp.where(causal, cum_l - cum_s, NEG))
                wmat = (decay * cbm * dtt_ref[pl.ds(head, 1), :]).astype(BF16)
                xm = jnp.where(left if q == 0 else jnp.logical_not(left), xpair, 0.0).astype(BF16)
                acc = acc + jnp.dot(wmat, xm, preferred_element_type=F32)
            parts.append(acc)
        y = jnp.concatenate(parts, axis=1) + y_off
        y = y + xg * dsk_ref[:, cols]
        zg = z_ref[:, cols]
        y = y * (zg * _sigmoid(zg))
        y = y * lax.rsqrt(jnp.mean(y * y, axis=-1, keepdims=True) + EPS) * nw_ref[:, cols]
        y_ref[:, cols] = y.astype(BF16)
        cl = ecg[blk - 1:blk, :]
        xt = (xg * (jnp.exp(cl - ecg) * edt_ref[:, cols])).astype(BF16)
        h_in = lax.dot_general(bb16, xt, (((0,), (0,)), ((), ())), preferred_element_type=F32)
        state_ref[:, cols] = sg * jnp.exp(cl) + h_in
        return carry

    lax.fori_loop(0, SSD_GROUPS, group, 0)

    @pl.when(c == n_chunks - 1)
    def _():
        so_ref[0] = state_ref[...].T


def ssd_scan(zx, dt, p, s0, c0, *, n_batch, n_chunks, valid_len, shared_init):
    blk = SSD_BLOCK
    n_rows = n_batch * n_chunks * blk
    assert zx.shape == (n_rows, ZX_DIM)
    rowi = lambda b, c: b * n_chunks + c
    init = (lambda b, c: (0, 0, 0)) if shared_init else (lambda b, c: (b, 0, 0))
    const = lambda b, c: (0, 0)
    in_specs = [
        pl.BlockSpec((blk, D_INNER), lambda b, c: (rowi(b, c), 0)),
        pl.BlockSpec((blk, D_INNER), lambda b, c: (rowi(b, c), 1)),
        pl.BlockSpec((blk, GN), lambda b, c: (rowi(b, c), 2 * D_INNER // GN)),
        pl.BlockSpec((blk, GN), lambda b, c: (rowi(b, c), 2 * D_INNER // GN + 1)),
        pl.BlockSpec((blk, LANES), lambda b, c: (rowi(b, c), 0)),
        pl.BlockSpec((8, CONV_DIM), const),
        pl.BlockSpec((1, CONV_DIM), const),
        pl.BlockSpec((1, LANES), const),
        pl.BlockSpec((1, LANES), const),
        pl.BlockSpec((1, D_INNER), const),
        pl.BlockSpec((1, D_INNER), const),
        pl.BlockSpec((3 * LANES, D_INNER), const),
        pl.BlockSpec((1, D_INNER, SSD_STATE), init),
        pl.BlockSpec((1, 8, CONV_DIM), init),
    ]
    out_specs = [pl.BlockSpec((blk, D_INNER), lambda b, c: (rowi(b, c), 0)),
                 pl.BlockSpec((1, D_INNER, SSD_STATE), lambda b, c: (b, 0, 0))]
    out_shape = [jax.ShapeDtypeStruct((n_rows, D_INNER), BF16),
                 jax.ShapeDtypeStruct((n_batch, D_INNER, SSD_STATE), F32)]
    scratch = [pltpu.VMEM((SSD_STATE, D_INNER), F32),
               pltpu.VMEM((blk + 8, CONV_DIM), F32),
               pltpu.VMEM((LANES, blk), F32),
               pltpu.VMEM((LANES, blk), F32),
               pltpu.VMEM((blk, D_INNER), F32),
               pltpu.VMEM((blk, D_INNER), F32)]
    kern = functools.partial(_ssd_kernel, valid_len=valid_len, n_chunks=n_chunks)
    return pl.pallas_call(
        kern, grid=(n_batch, n_chunks), in_specs=in_specs, out_specs=out_specs,
        out_shape=out_shape, scratch_shapes=scratch,
        compiler_params=_cparams(("parallel", "arbitrary")), name="ssd_scan")(
            zx, zx, zx, zx, dt, p["conv_w"], p["conv_b"], p["dt_bias"], p["a_log"], p["d_skip"],
            p["norm_w"], p["expand"], s0, c0)


def _attn_core(q, kk, vv, sink_ref, kmask):
    n_q = q.shape[0]
    scale = ATTN_HEAD_DIM ** -0.5
    left = lax.broadcasted_iota(I32, (n_q, LANES), 1) < ATTN_HEAD_DIM
    nt = (((1,), (1,)), ((), ()))
    outs = []
    for pr in range(KV_DIM // LANES):
        kp = kk[:, pr * LANES:(pr + 1) * LANES].astype(BF16)
        vp = vv[:, pr * LANES:(pr + 1) * LANES].astype(BF16)
        qs, sinks = [], []
        for r in range(Q_PER_KV):
            qb = q[:, (pr * Q_PER_KV + r) * LANES:(pr * Q_PER_KV + r + 1) * LANES]
            qs += [jnp.where(left, qb, 0.0), jnp.where(left, 0.0, qb)]
            for half in range(2):
                sinks.append(jnp.full((n_q, 1), sink_ref[(pr * Q_PER_KV + r) * 2 + half], F32))
        qstack = jnp.concatenate(qs, axis=0).astype(BF16)
        sink = jnp.concatenate(sinks, axis=0)
        s = lax.dot_general(qstack, kp, nt, preferred_element_type=F32) * scale
        if kmask is not None:
            s = jnp.where(kmask, s, NEG)
        m = jnp.maximum(jnp.max(s, axis=-1, keepdims=True), sink)
        p = jnp.exp(s - m)
        den = jnp.sum(p, axis=-1, keepdims=True) + jnp.exp(sink - m)
        p = p * (1.0 / den)
        o = jnp.dot(p.astype(BF16), vp, preferred_element_type=F32)
        for r in range(Q_PER_KV):
            outs.append(jnp.where(left, o[(2 * r) * n_q:(2 * r + 1) * n_q],
                                  o[(2 * r + 1) * n_q:(2 * r + 2) * n_q]))
    return jnp.concatenate(outs, axis=1)


def _attn_band_kernel(sink_ref, q_ref, k_ref, v_ref, mk_ref, mv_ref, o_ref):
    c = pl.program_id(1)
    band = WINDOW + CHUNK
    first = jnp.maximum(c - WINDOW // CHUNK, 0)
    start = pl.multiple_of(first * CHUNK, CHUNK)
    kk = jnp.concatenate([mk_ref[...], k_ref[pl.ds(start, band), :]], axis=0)
    vv = jnp.concatenate([mv_ref[...], v_ref[pl.ds(start, band), :]], axis=0)
    j = lax.broadcasted_iota(I32, (1, N_META + band), 1)
    key_chunk = first + jnp.maximum(j - N_META, 0) // CHUNK
    kmask = (j < N_META) | (key_chunk <= c)
    o_ref[...] = _attn_core(q_ref[...], kk, vv, sink_ref, kmask).astype(BF16)


def attn_band(qkv, meta_k, meta_v, sinks, *, n_batch, seq):
    n_chunks = seq // CHUNK
    assert seq >= WINDOW + CHUNK
    kcol = Q_DIM // KV_DIM
    grid_spec = pltpu.PrefetchScalarGridSpec(
        num_scalar_prefetch=1, grid=(n_batch, n_chunks),
        in_specs=[pl.BlockSpec((CHUNK, Q_DIM), lambda b, c, s: (b * n_chunks + c, 0)),
                  pl.BlockSpec((seq, KV_DIM), lambda b, c, s: (b, kcol)),
                  pl.BlockSpec((seq, KV_DIM), lambda b, c, s: (b, kcol + 1)),
                  pl.BlockSpec((N_META, KV_DIM), lambda b, c, s: (0, 0)),
                  pl.BlockSpec((N_META, KV_DIM), lambda b, c, s: (0, 0))],
        out_specs=pl.BlockSpec((CHUNK, Q_DIM), lambda b, c, s: (b * n_chunks + c, 0)))
    return pl.pallas_call(
        _attn_band_kernel, grid_spec=grid_spec,
        out_shape=jax.ShapeDtypeStruct((n_batch * seq, Q_DIM), BF16),
        compiler_params=_cparams(("parallel", "arbitrary")), name="attn_band")(
            sinks, qkv, qkv, qkv, meta_k, meta_v)


def _attn_full_kernel(sink_ref, q_ref, k_ref, v_ref, o_ref):
    o_ref[...] = _attn_core(q_ref[...], k_ref[...], v_ref[...], sink_ref, None).astype(BF16)


def attn_full(q, kk, vv, sinks):
    n_batch, n_q, _ = q.shape
    n_k = kk.shape[1]
    grid_spec = pltpu.PrefetchScalarGridSpec(
        num_scalar_prefetch=1, grid=(n_batch,),
        in_specs=[pl.BlockSpec((None, n_q, Q_DIM), lambda b, s: (b, 0, 0)),
                  pl.BlockSpec((None, n_k, KV_DIM), lambda b, s: (b, 0, 0)),
                  pl.BlockSpec((None, n_k, KV_DIM), lambda b, s: (b, 0, 0))],
        out_specs=pl.BlockSpec((None, n_q, Q_DIM), lambda b, s: (b, 0, 0)))
    return pl.pallas_call(
        _attn_full_kernel, grid_spec=grid_spec,
        out_shape=jax.ShapeDtypeStruct((n_batch, n_q, Q_DIM), BF16),
        compiler_params=_cparams(("parallel",)), name="attn_full")(sinks, q, kk, vv)


def _expert_ffn(x16, w_in, w_out):
    gu = jnp.dot(x16, w_in, preferred_element_type=F32)
    a, b = gu[:, :D_EXPERT], gu[:, D_EXPERT:]
    hmid = a * _sigmoid(a) * b
    return jnp.dot(hmid.astype(BF16), w_out, preferred_element_type=F32)


def _moe_dense_kernel(x_ref, e_ref, p_ref, wi_ref, wo_ref, o_ref):
    e = pl.program_id(0)

    @pl.when(e == 0)
    def _():
        o_ref[...] = jnp.zeros_like(o_ref)

    eid, wts = e_ref[...], p_ref[...]
    sel0, sel1 = eid[:, 0:1] == e, eid[:, 1:2] == e
    coef = jnp.where(sel0, wts[:, 0:1], jnp.where(sel1, wts[:, 1:2], 0.0))
    y = _expert_ffn(x_ref[...].astype(BF16), wi_ref[0], wo_ref[0])
    o_ref[...] += jnp.where(sel0 | sel1, y * coef, 0.0)


def moe_dense(xn, eid, wts, w_in, w_out):
    n_rows = xn.shape[0]
    full = lambda e: (0, 0)
    return pl.pallas_call(
        _moe_dense_kernel, grid=(N_EXPERTS,),
        in_specs=[pl.BlockSpec((n_rows, D_MODEL), full), pl.BlockSpec((n_rows, LANES), full),
                  pl.BlockSpec((n_rows, LANES), full),
                  pl.BlockSpec((1, D_MODEL, 2 * D_EXPERT), lambda e: (e, 0, 0)),
                  pl.BlockSpec((1, D_EXPERT, D_MODEL), lambda e: (e, 0, 0))],
        out_specs=pl.BlockSpec((n_rows, D_MODEL), full),
        out_shape=jax.ShapeDtypeStruct((n_rows, D_MODEL), F32),
        compiler_params=_cparams(("arbitrary",)), name="moe_dense")(xn, eid, wts, w_in, w_out)


def _moe_sorted_kernel(blk_e_ref, blk_n_ref, code_ref, code_next_ref, x_hbm, wi_ref, wo_ref, y_hbm,
                       xbuf, obuf, gsem, ssem, *, n_blocks, n_tok):
    i = pl.program_id(0)
    slot = i % 2

    def gather_copy(codes, r, s):
        tok = codes[0, 0, r] >> 1
        return pltpu.make_async_copy(x_hbm.at[pl.ds(tok, 1), :], xbuf.at[s, pl.ds(r, 1), :], gsem.at[s])

    def scatter_copy(codes, r, s):
        code = codes[0, 0, r]
        dst = (code & 1) * n_tok + (code >> 1)
        return pltpu.make_async_copy(obuf.at[s, pl.ds(r, 1), :], y_hbm.at[pl.ds(dst, 1), :], ssem.at[s])

    def start_gathers(codes, n, s):
        def body(r, carry):
            gather_copy(codes, r, s).start()
            return carry
        lax.fori_loop(0, n, body, 0)

    def wait_rows(copy_fn, n, s):
        def body(r, carry):
            copy_fn(r, s).wait()
            return carry
        lax.fori_loop(0, n, body, 0)

    n_cur = blk_n_ref[i]

    @pl.when(i == 0)
    def _():
        xbuf[...] = jnp.zeros_like(xbuf)
        start_gathers(code_ref, n_cur, 0)

    wait_rows(lambda r, s: gather_copy(code_ref, 0, s), n_cur, slot)

    @pl.when(i + 1 < n_blocks)
    def _():
        start_gathers(code_next_ref, blk_n_ref[jnp.minimum(i + 1, n_blocks - 1)], 1 - slot)

    @pl.when(i >= 2)
    def _():
        wait_rows(lambda r, s: scatter_copy(code_ref, 0, s), blk_n_ref[jnp.maximum(i - 2, 0)], slot)

    @pl.when(n_cur > 0)
    def _():
        obuf[slot] = _expert_ffn(xbuf[slot].astype(BF16), wi_ref[0], wo_ref[0])

        def body(r, carry):
            scatter_copy(code_ref, r, slot).start()
            return carry
        lax.fori_loop(0, n_cur, body, 0)

    @pl.when(i == n_blocks - 1)
    def _():
        if n_blocks >= 2:
            wait_rows(lambda r, s: scatter_copy(code_ref, 0, s), blk_n_ref[n_blocks - 2], 1 - slot)
        wait_rows(lambda r, s: scatter_copy(code_ref, 0, s), n_cur, slot)


def moe_sorted(xn, w_in, w_out, blk_e, blk_n, codes, *, tm):
    n_tok = xn.shape[0]
    n_blocks = blk_e.shape[0]
    grid_spec = pltpu.PrefetchScalarGridSpec(
        num_scalar_prefetch=2, grid=(n_blocks,),
        in_specs=[pl.BlockSpec((1, 1, tm), lambda i, be, bn: (i, 0, 0), memory_space=pltpu.SMEM),
                  pl.BlockSpec((1, 1, tm), lambda i, be, bn: (jnp.minimum(i + 1, n_blocks - 1), 0, 0),
                               memory_space=pltpu.SMEM),
                  pl.BlockSpec(memory_space=pl.ANY),
                  pl.BlockSpec((1, D_MODEL, 2 * D_EXPERT), lambda i, be, bn: (be[i], 0, 0)),
                  pl.BlockSpec((1, D_EXPERT, D_MODEL), lambda i, be, bn: (be[i], 0, 0))],
        out_specs=pl.BlockSpec(memory_space=pl.ANY),
        scratch_shapes=[pltpu.VMEM((2, tm, D_MODEL), F32), pltpu.VMEM((2, tm, D_MODEL), F32),
                        pltpu.SemaphoreType.DMA((2,)), pltpu.SemaphoreType.DMA((2,))])
    return pl.pallas_call(
        functools.partial(_moe_sorted_kernel, n_blocks=n_blocks, n_tok=n_tok), grid_spec=grid_spec,
        out_shape=jax.ShapeDtypeStruct((2 * n_tok, D_MODEL), F32),
        compiler_params=_cparams(("arbitrary",)), name="moe_sorted")(
            blk_e, blk_n, codes, codes, xn, w_in, w_out)


def route_tables(eid, *, tm):
    n_tok = eid.shape[0]
    n_asg = 2 * n_tok
    flat = eid.reshape(-1)
    skey = jnp.sort(flat * n_asg + jnp.arange(n_asg, dtype=I32))
    order = skey % n_asg
    counts = jnp.sum((flat[:, None] == jnp.arange(N_EXPERTS, dtype=I32)[None, :]).astype(I32), axis=0)
    starts = jnp.cumsum(counts) - counts
    padded = (counts + tm - 1) // tm * tm
    pend = jnp.cumsum(padded)
    pstarts = pend - padded
    n_blocks = -(-(n_asg + N_EXPERTS * (tm - 1)) // tm)
    blk_start = jnp.arange(n_blocks, dtype=I32) * tm
    blk_e = jnp.minimum(jnp.sum((pend[None, :] <= blk_start[:, None]).astype(I32), axis=1), N_EXPERTS - 1)
    blk_off = blk_start - pstarts[blk_e]
    blk_n = jnp.clip(counts[blk_e] - blk_off, 0, tm)
    src = (starts[blk_e] + blk_off)[:, None] + jnp.arange(tm, dtype=I32)[None, :]
    valid = jnp.arange(tm, dtype=I32)[None, :] < blk_n[:, None]
    codes = jnp.where(valid, order[jnp.clip(src, 0, n_asg - 1)], 0)
    return blk_e.astype(I32), blk_n.astype(I32), codes.reshape(n_blocks, 1, tm).astype(I32)


def _q_head_order():
    order = []
    for pr in range(N_KV_HEADS // 2):
        for r in range(Q_PER_KV):
            for half in range(2):
                order.append((2 * pr + half) * Q_PER_KV + r)
    return jnp.asarray(order, dtype=I32)


def _pad_lanes(a):
    return jnp.pad(a, [(0, 0)] * (a.ndim - 1) + [(0, LANES - a.shape[-1])])


def _ssd_params(conv_w, conv_b, dt_bias, a_log, d_skip, norm_w):
    head_of_col = jnp.arange(D_INNER, dtype=I32) // SSD_HEAD_DIM
    expand = (jnp.arange(LANES, dtype=I32)[:, None] == head_of_col[None, :]).astype(BF16)
    return {
        "conv_w": jnp.pad(conv_w, ((0, 8 - SSD_CONV), (0, 0))),
        "conv_b": conv_b.reshape(1, CONV_DIM),
        "dt_bias": _pad_lanes(dt_bias.reshape(1, SSD_HEADS)),
        "a_log": _pad_lanes(a_log.reshape(1, SSD_HEADS)),
        "d_skip": jnp.repeat(d_skip, SSD_HEAD_DIM).reshape(1, D_INNER),
        "norm_w": norm_w.reshape(1, D_INNER),
        "expand": jnp.concatenate([expand, expand, expand], axis=0),
    }


def _router_params(w_group, b_group, w_router, b_router):
    wr = _pad_lanes(jnp.concatenate([w_group, w_router], axis=1)).astype(BF16)
    br = _pad_lanes(jnp.concatenate([b_group, b_router]).reshape(1, -1)).astype(F32)
    return wr, br


def _pick_tile(n, prefs):
    for t in prefs:
        if n % t == 0:
            return t
    return n


def kernel(x_prompt, x_sample, cache_ssm_state, cache_ssm_conv, cache_attn_k, cache_attn_v, cache_meta_k, cache_meta_v, meta_tokens, norm_mix, norm_ffn, norm_final, ssd_w_in, ssd_conv_w, ssd_conv_b, ssd_dt_bias, ssd_a_log, ssd_d, ssd_norm, ssd_w_out, attn_w_qkv, attn_sinks, attn_w_out, moe_w_group, moe_b_group, moe_w_router, moe_b_router, moe_w_in, moe_w_out):
    n_batch, seq, _ = x_prompt.shape
    dec_batch, dec_seq, _ = x_sample.shape
    n_real = n_batch * seq
    n_small = N_META + dec_batch * dec_seq
    blk = SSD_BLOCK
    assert seq % blk == 0 and dec_seq <= blk and N_META <= blk
    tm_big = _pick_tile(n_real, (1024, 512, 256))
    tm_mid = _pick_tile(n_real, (512, 256))

    h_real = x_prompt.reshape(n_real, D_MODEL)
    h_small = jnp.concatenate([meta_tokens, x_sample.reshape(-1, D_MODEL)], axis=0)

    w_in16 = ssd_w_in[0].astype(BF16)
    w_zx, w_dt = w_in16[:, :ZX_DIM], _pad_lanes(w_in16[:, ZX_DIM:])
    sp = _ssd_params(ssd_conv_w[0], ssd_conv_b[0], ssd_dt_bias[0], ssd_a_log[0], ssd_d[0], ssd_norm[0])
    w_out16 = ssd_w_out[0].astype(BF16)
    wr0, br0 = _router_params(moe_w_group[0], moe_b_group[0], moe_w_router[0], moe_b_router[0])

    zx_s, dt_s = norm_matmul(h_small, norm_mix[0], w_zx, tm=n_small, tn=512, w_extra=w_dt)
    zx_r, dt_r = norm_matmul(h_real, norm_mix[0], w_zx, tm=tm_big, tn=512, w_extra=w_dt)

    n_sb = 1 + dec_batch

    def to_blocks(a, n_valid_meta=N_META):
        meta = jnp.pad(a[:N_META], ((0, blk - N_META), (0, 0)))[None]
        samp = jnp.pad(a[N_META:].reshape(dec_batch, dec_seq, -1), ((0, 0), (0, blk - dec_seq), (0, 0)))
        return jnp.concatenate([meta, samp], axis=0).reshape(n_sb * blk, -1)

    assert dec_seq == N_META, "small-path blocks share one valid length"
    s0_small = jnp.concatenate([jnp.zeros((1, D_INNER, SSD_STATE), F32),
                                cache_ssm_state[0].reshape(dec_batch, D_INNER, SSD_STATE)], axis=0)
    c0_small = jnp.pad(jnp.concatenate([jnp.zeros((1, SSD_CONV - 1, CONV_DIM), F32), cache_ssm_conv[0]], axis=0),
                       ((0, 0), (8 - (SSD_CONV - 1), 0), (0, 0)))
    yn_sb, state_small = ssd_scan(to_blocks(zx_s), to_blocks(dt_s), sp, s0_small, c0_small,
                                  n_batch=n_sb, n_chunks=1, valid_len=dec_seq, shared_init=False)
    yn_sb = yn_sb.reshape(n_sb, blk, D_INNER)
    yn_small = jnp.concatenate([yn_sb[0, :N_META], yn_sb[1:, :dec_seq].reshape(-1, D_INNER)], axis=0)

    c0_real = jnp.pad(zx_s[N_META - (SSD_CONV - 1):N_META, D_INNER:], ((8 - (SSD_CONV - 1), 0), (0, 0)))[None]
    yn_real, state_real = ssd_scan(zx_r, dt_r, sp, state_small[0:1], c0_real,
                                   n_batch=n_batch, n_chunks=seq // blk, valid_len=blk, shared_init=True)

    tk0 = 1024
    h1_s, xn_s, eid_s, wts_s = matmul_route(yn_small, w_out16, h_small, norm_ffn[0], wr0, br0, tm=n_small, tk=tk0)
    h1_r, xn_r, eid_r, wts_r = matmul_route(yn_real, w_out16, h_real, norm_ffn[0], wr0, br0, tm=tm_mid, tk=tk0)

    wi0, wo0 = moe_w_in[0].astype(BF16), moe_w_out[0].astype(BF16)
    moe_s = moe_dense(xn_s, eid_s, wts_s, wi0, wo0)
    blk_e, blk_n, codes = route_tables(eid_r[:, :2], tm=MOE_TM)
    moe_r = moe_sorted(xn_r, wi0, wo0, blk_e, blk_n, codes, tm=MOE_TM)

    head_order = _q_head_order()
    wq = attn_w_qkv[0][:, :Q_DIM].reshape(D_MODEL, N_HEADS, ATTN_HEAD_DIM)[:, head_order].reshape(D_MODEL, Q_DIM)
    w_qkv16 = jnp.concatenate([wq, attn_w_qkv[0][:, Q_DIM:]], axis=1).astype(BF16)
    sinks = attn_sinks[0][head_order].astype(F32)
    w_ao16 = attn_w_out[0].reshape(N_HEADS, ATTN_HEAD_DIM, D_MODEL)[head_order].reshape(Q_DIM, D_MODEL).astype(BF16)
    wr1, br1 = _router_params(moe_w_group[1], moe_b_group[1], moe_w_router[1], moe_b_router[1])

    qkv_s, h2_s = norm_matmul(h1_s, norm_mix[1], w_qkv16, tm=n_small, tn=512, mode="sum", moe=(moe_s,),
                              write_h=True)
    qkv_r, h2_r = norm_matmul(h1_r, norm_mix[1], w_qkv16, tm=tm_mid, tn=512, mode="pair",
                              moe=(moe_r, wts_r), write_h=True)

    k_meta, v_meta = qkv_s[:N_META, Q_DIM:Q_DIM + KV_DIM], qkv_s[:N_META, Q_DIM + KV_DIM:]
    k_new = qkv_s[N_META:, Q_DIM:Q_DIM + KV_DIM].reshape(dec_batch, dec_seq, KV_DIM)
    v_new = qkv_s[N_META:, Q_DIM + KV_DIM:].reshape(dec_batch, dec_seq, KV_DIM)
    o_meta = attn_full(qkv_s[None, :N_META, :Q_DIM], k_meta[None], v_meta[None], sinks)
    kk_s = jnp.concatenate([cache_meta_k[0].reshape(dec_batch, N_META, KV_DIM),
                            cache_attn_k[0].reshape(dec_batch, WINDOW, KV_DIM), k_new], axis=1)
    vv_s = jnp.concatenate([cache_meta_v[0].reshape(dec_batch, N_META, KV_DIM),
                            cache_attn_v[0].reshape(dec_batch, WINDOW, KV_DIM), v_new], axis=1)
    o_samp = attn_full(qkv_s[N_META:, :Q_DIM].reshape(dec_batch, dec_seq, Q_DIM), kk_s, vv_s, sinks)
    o_small = jnp.concatenate([o_meta.reshape(N_META, Q_DIM), o_samp.reshape(-1, Q_DIM)], axis=0)
    o_real = attn_band(qkv_r, k_meta, v_meta, sinks, n_batch=n_batch, seq=seq)

    h3_s, xn_s, eid_s, wts_s = matmul_route(o_small, w_ao16, h2_s, norm_ffn[1], wr1, br1, tm=n_small, tk=1024)
    h3_r, xn_r, eid_r, wts_r = matmul_route(o_real, w_ao16, h2_r, norm_ffn[1], wr1, br1, tm=tm_mid, tk=1024)

    wi1, wo1 = moe_w_in[1].astype(BF16), moe_w_out[1].astype(BF16)
    moe_s = moe_dense(xn_s, eid_s, wts_s, wi1, wo1)
    blk_e, blk_n, codes = route_tables(eid_r[:, :2], tm=MOE_TM)
    moe_r = moe_sorted(xn_r, wi1, wo1, blk_e, blk_n, codes, tm=MOE_TM)

    y_small = final_norm(h3_s, norm_final, tm=n_small, mode="sum", moe=(moe_s,))
    y_real = final_norm(h3_r, norm_final, tm=tm_mid, mode="pair", moe=(moe_r, wts_r))

    kv_shape = (N_KV_HEADS, ATTN_HEAD_DIM)
    qkv_r3 = qkv_r.reshape(n_batch, seq, -1)
    zx_r3 = zx_r.reshape(n_batch, seq, ZX_DIM)
    zx_samp = zx_s[N_META:].reshape(dec_batch, dec_seq, ZX_DIM)
    return (
        y_real.reshape(n_batch, seq, D_MODEL),
        y_small[N_META:].reshape(dec_batch, dec_seq, D_MODEL),
        state_real.reshape(1, n_batch, SSD_HEADS, SSD_HEAD_DIM, SSD_STATE),
        zx_r3[None, :, seq - (SSD_CONV - 1):, D_INNER:],
        qkv_r3[:, seq - WINDOW:, Q_DIM:Q_DIM + KV_DIM].reshape((1, n_batch, WINDOW) + kv_shape),
        qkv_r3[:, seq - WINDOW:, Q_DIM + KV_DIM:].reshape((1, n_batch, WINDOW) + kv_shape),
        jnp.broadcast_to(k_meta.reshape((1, 1, N_META) + kv_shape), (1, n_batch, N_META) + kv_shape),
        jnp.broadcast_to(v_meta.reshape((1, 1, N_META) + kv_shape), (1, n_batch, N_META) + kv_shape),
        state_small[1:].reshape(1, dec_batch, SSD_HEADS, SSD_HEAD_DIM, SSD_STATE),
        zx_samp[None, :, dec_seq - (SSD_CONV - 1):, D_INNER:],
        k_new.reshape((1, dec_batch, dec_seq) + kv_shape),
        v_new.reshape((1, dec_batch, dec_seq) + kv_shape),
    )
```

```python
import functools

import jax
import jax.numpy as jnp
from jax import lax
from jax.experimental import pallas as pl
from jax.experimental.pallas import tpu as pltpu

F32, BF16, I32 = jnp.float32, jnp.bfloat16, jnp.int32

D_MODEL = 2048
N_META = 16
EPS = 1e-5
D_INNER = 4096
SSD_HEAD_DIM = 64
SSD_HEADS = 64
SSD_GROUPS = 8
SSD_STATE = 128
SSD_CONV = 4
GN = SSD_GROUPS * SSD_STATE
CONV_DIM = D_INNER + 2 * GN
ZX_DIM = D_INNER + CONV_DIM
GROUP_W = D_INNER // SSD_GROUPS
WINDOW = 128
CHUNK = 64
ATTN_HEAD_DIM = 64
N_HEADS = 32
N_KV_HEADS = 8
Q_PER_KV = N_HEADS // N_KV_HEADS
Q_DIM = N_HEADS * ATTN_HEAD_DIM
KV_DIM = N_KV_HEADS * ATTN_HEAD_DIM
N_GROUPS = 4
EXPERTS_PER_GROUP = 8
N_EXPERTS = 32
D_EXPERT = 512

LANES = 128
SSD_BLOCK = 128
NEG = -1e30
VMEM_LIMIT = 56 * 1024 * 1024
MOE_TM = 256
MOE_NC = 256
ATTN_LAG = 32


def _cparams(sem):
    return pltpu.CompilerParams(dimension_semantics=sem, vmem_limit_bytes=VMEM_LIMIT)


def _sigmoid(x):
    return 1.0 / (1.0 + jnp.exp(-x))


def _rms(h, g):
    var = jnp.mean(h * h, axis=-1, keepdims=True)
    return h * lax.rsqrt(var + EPS) * g


def _row_chunk(tm):
    for rc in (256, 128, 64, 32, 16, 8):
        if tm % rc == 0:
            return rc
    raise ValueError(tm)


_N_MOE_REFS = {"none": 0, "sum": 1, "pair": 3}


def _combine_rows(x_ref, moe_refs, mode, rows):
    h = x_ref[rows, :]
    if mode == "sum":
        h = h + moe_refs[0][rows, :]
    elif mode == "pair":
        y0, y1, w = moe_refs
        wv = w[rows, :]
        h = h + (y0[rows, :] * wv[:, 0:1] + y1[rows, :] * wv[:, 1:2])
    return h


def _norm_mm_kernel(*refs, mode, write_h, has_extra, tm):
    it = iter(refs)
    x_ref = next(it)
    moe_refs = [next(it) for _ in range(_N_MOE_REFS[mode])]
    g_ref, w_ref = next(it), next(it)
    we_ref = next(it) if has_extra else None
    o_ref = next(it)
    h_ref = next(it) if write_h else None
    e_ref = next(it) if has_extra else None
    xn_ref = next(it)
    rc = _row_chunk(tm)

    @pl.when(pl.program_id(1) == 0)
    def _():
        def body(r, carry):
            rows = pl.ds(pl.multiple_of(r * rc, rc), rc)
            h = _combine_rows(x_ref, moe_refs, mode, rows)
            if write_h:
                h_ref[rows, :] = h
            xn_ref[rows, :] = _rms(h, g_ref[...]).astype(BF16)
            return carry

        lax.fori_loop(0, tm // rc, body, 0)
        if has_extra:
            e_ref[...] = jnp.dot(xn_ref[...], we_ref[...], preferred_element_type=F32)

    o_ref[...] = jnp.dot(xn_ref[...], w_ref[...], preferred_element_type=F32)


def _moe_in_specs(mode, moe, tm, n_rows):
    if mode == "none":
        return [], []
    if mode == "sum":
        return [pl.BlockSpec((tm, D_MODEL), lambda i, j: (i, 0))], [moe[0]]
    y2, wts = moe
    off = n_rows // tm
    specs = [pl.BlockSpec((tm, D_MODEL), lambda i, j: (i, 0)),
             pl.BlockSpec((tm, D_MODEL), lambda i, j: (i + off, 0)),
             pl.BlockSpec((tm, LANES), lambda i, j: (i, 0))]
    return specs, [y2, y2, wts]


def norm_matmul(x, g, w, *, tm, tn, mode="none", moe=(), w_extra=None, write_h=False):
    n_rows, n_out = x.shape[0], w.shape[1]
    assert n_rows % tm == 0 and n_out % tn == 0
    moe_specs, moe_ops = _moe_in_specs(mode, moe, tm, n_rows)
    in_specs = [pl.BlockSpec((tm, D_MODEL), lambda i, j: (i, 0))] + moe_specs + [
        pl.BlockSpec((1, D_MODEL), lambda i, j: (0, 0)),
        pl.BlockSpec((D_MODEL, tn), lambda i, j: (0, j))]
    operands = [x] + moe_ops + [g.reshape(1, D_MODEL), w]
    out_shape = [jax.ShapeDtypeStruct((n_rows, n_out), F32)]
    out_specs = [pl.BlockSpec((tm, tn), lambda i, j: (i, j))]
    if w_extra is not None:
        in_specs.append(pl.BlockSpec((D_MODEL, LANES), lambda i, j: (0, 0)))
        operands.append(w_extra)
    if write_h:
        out_shape.append(jax.ShapeDtypeStruct((n_rows, D_MODEL), F32))
        out_specs.append(pl.BlockSpec((tm, D_MODEL), lambda i, j: (i, 0)))
    if w_extra is not None:
        out_shape.append(jax.ShapeDtypeStruct((n_rows, LANES), F32))
        out_specs.append(pl.BlockSpec((tm, LANES), lambda i, j: (i, 0)))
    kern = functools.partial(_norm_mm_kernel, mode=mode, write_h=write_h,
                             has_extra=w_extra is not None, tm=tm)
    return pl.pallas_call(
        kern, grid=(n_rows // tm, n_out // tn), in_specs=in_specs, out_specs=out_specs,
        out_shape=out_shape, scratch_shapes=[pltpu.VMEM((tm, D_MODEL), BF16)],
        compiler_params=_cparams(("parallel", "arbitrary")), name="norm_matmul")(*operands)


def _final_norm_kernel(*refs, mode, tm):
    x_ref = refs[0]
    moe_refs = refs[1:1 + _N_MOE_REFS[mode]]
    g_ref, o_ref = refs[-2], refs[-1]
    rc = _row_chunk(tm)

    def body(r, carry):
        rows = pl.ds(pl.multiple_of(r * rc, rc), rc)
        o_ref[rows, :] = _rms(_combine_rows(x_ref, moe_refs, mode, rows), g_ref[...])
        return carry

    lax.fori_loop(0, tm // rc, body, 0)


def final_norm(x, g, *, tm, mode, moe):
    n_rows = x.shape[0]
    moe_specs, moe_ops = _moe_in_specs(mode, moe, tm, n_rows)
    in_specs = [pl.BlockSpec((tm, D_MODEL), lambda i, j: (i, 0))] + moe_specs + [
        pl.BlockSpec((1, D_MODEL), lambda i, j: (0, 0))]
    return pl.pallas_call(
        functools.partial(_final_norm_kernel, mode=mode, tm=tm), grid=(n_rows // tm, 1),
        in_specs=in_specs, out_specs=pl.BlockSpec((tm, D_MODEL), lambda i, j: (i, 0)),
        out_shape=jax.ShapeDtypeStruct((n_rows, D_MODEL), F32),
        compiler_params=_cparams(("parallel", "arbitrary")), name="final_norm")(
            x, *moe_ops, g.reshape(1, D_MODEL))


def _route(lg):
    lane = lax.broadcasted_iota(I32, lg.shape, 1)
    is_g = lane < N_GROUPS
    gl = jnp.where(is_g, lg, NEG)
    gmax = jnp.max(gl, axis=-1, keepdims=True)
    gsum = jnp.sum(jnp.where(is_g, jnp.exp(gl - gmax), 0.0), axis=-1, keepdims=True)
    gate = 1.0 / gsum
    gidx = jnp.min(jnp.where(gl == gmax, lane, LANES), axis=-1, keepdims=True)
    ecol = lane - N_GROUPS
    in_group = (ecol >= 0) & (ecol < N_EXPERTS) & ((ecol >> 3) == gidx)
    el = jnp.where(in_group, lg, NEG)
    v1 = jnp.max(el, axis=-1, keepdims=True)
    i1 = jnp.min(jnp.where(el == v1, lane, LANES), axis=-1, keepdims=True)
    el2 = jnp.where(lane == i1, NEG, el)
    v2 = jnp.max(el2, axis=-1, keepdims=True)
    i2 = jnp.min(jnp.where(el2 == v2, lane, LANES), axis=-1, keepdims=True)
    e2 = jnp.exp(v2 - v1)
    w1 = gate / (1.0 + e2)
    w2 = gate * e2 / (1.0 + e2)
    eid = jnp.where(lane == 0, i1 - N_GROUPS, jnp.where(lane == 1, i2 - N_GROUPS, 0))
    wts = jnp.where(lane == 0, w1, jnp.where(lane == 1, w2, 0.0))
    return eid, wts


def _mm_route_kernel(a_ref, w_ref, h_ref, g_ref, wr_ref, br_ref,
                     h1_ref, xn_ref, e_ref, p_ref, acc_ref, *, nk, tm):
    k = pl.program_id(1)
    rc = _row_chunk(tm)

    @pl.when(k == 0)
    def _():
        acc_ref[...] = jnp.zeros_like(acc_ref)

    acc_ref[...] += jnp.dot(a_ref[...], w_ref[...], preferred_element_type=F32)

    @pl.when(k == nk - 1)
    def _():
        def body(r, carry):
            rows = pl.ds(pl.multiple_of(r * rc, rc), rc)
            h1 = h_ref[rows, :] + acc_ref[rows, :]
            h1_ref[rows, :] = h1
            xn = _rms(h1, g_ref[...])
            xn_ref[rows, :] = xn
            lg = jnp.dot(xn.astype(BF16), wr_ref[...], preferred_element_type=F32) + br_ref[...]
            eid, wts = _route(lg)
            e_ref[rows, :] = eid
            p_ref[rows, :] = wts
            return carry

        lax.fori_loop(0, tm // rc, body, 0)


def matmul_route(a, w, h, g, wr, br, *, tm, tk):
    n_rows, kdim = a.shape
    assert n_rows % tm == 0 and kdim % tk == 0
    nk = kdim // tk
    row_spec = pl.BlockSpec((tm, D_MODEL), lambda i, k: (i, 0))
    lane_spec = pl.BlockSpec((tm, LANES), lambda i, k: (i, 0))
    return pl.pallas_call(
        functools.partial(_mm_route_kernel, nk=nk, tm=tm), grid=(n_rows // tm, nk),
        in_specs=[pl.BlockSpec((tm, tk), lambda i, k: (i, k)),
                  pl.BlockSpec((tk, D_MODEL), lambda i, k: (k, 0)),
                  row_spec,
                  pl.BlockSpec((1, D_MODEL), lambda i, k: (0, 0)),
                  pl.BlockSpec((D_MODEL, LANES), lambda i, k: (0, 0)),
                  pl.BlockSpec((1, LANES), lambda i, k: (0, 0))],
        out_specs=[row_spec, row_spec, lane_spec, lane_spec],
        out_shape=[jax.ShapeDtypeStruct((n_rows, D_MODEL), F32),
                   jax.ShapeDtypeStruct((n_rows, D_MODEL), F32),
                   jax.ShapeDtypeStruct((n_rows, LANES), I32),
                   jax.ShapeDtypeStruct((n_rows, LANES), F32)],
        scratch_shapes=[pltpu.VMEM((tm, D_MODEL), F32)],
        compiler_params=_cparams(("parallel", "arbitrary")), name="matmul_route")(
            a, w, h, g.reshape(1, D_MODEL), wr, br)


def _expand_heads(v, eg):
    v1 = v.astype(BF16).astype(F32)
    r1 = v - v1
    v2 = r1.astype(BF16).astype(F32)
    v3 = (r1 - v2).astype(BF16).astype(F32)
    parts = jnp.concatenate([v1, v2, v3, jnp.zeros_like(v)], axis=0).astype(BF16)
    return lax.dot_general(parts, eg, (((0,), (0,)), ((), ())), preferred_element_type=F32)


def _ssd_kernel(z_ref, x_ref, b_ref, c_ref, dt_ref, cw_ref, cb_ref, dtb_ref, alog_ref, dsk_ref,
                nw_ref, eg_ref, s0_ref, c0_ref, y_ref, so_ref,
                state_ref, xp_ref, cumt_ref, dtt_ref, *, valid_len, n_chunks):
    blk = SSD_BLOCK
    c = pl.program_id(1)

    @pl.when(c == 0)
    def _():
        state_ref[...] = s0_ref[0].T
        xp_ref[0:8, :] = c0_ref[0]

    @pl.when(c > 0)
    def _():
        xp_ref[0:8, :] = xp_ref[blk:blk + 8, :]

    xp_ref[8:8 + blk, 0:D_INNER] = x_ref[...]
    xp_ref[8:8 + blk, D_INNER:D_INNER + GN] = b_ref[...]
    xp_ref[8:8 + blk, D_INNER + GN:CONV_DIM] = c_ref[...]

    row = lax.broadcasted_iota(I32, (blk, LANES), 0)
    lane = lax.broadcasted_iota(I32, (blk, LANES), 1)
    dtr = dt_ref[...] + dtb_ref[...]
    dt = jnp.maximum(dtr, 0.0) + jnp.log1p(jnp.exp(-jnp.abs(dtr)))
    if valid_len < blk:
        dt = jnp.where(row < valid_len, dt, 0.0)
    cum = dt * (-jnp.exp(alog_ref[...]))
    sh = 1
    while sh < blk:
        cum = cum + jnp.where(row >= sh, pltpu.roll(cum, sh, 0), 0.0)
        sh *= 2
    cumt_ref[...] = cum.T
    dtt_ref[...] = dt.T

    causal = lane <= row
    left = lane < SSD_HEAD_DIM

    def conv(cols):
        acc = cb_ref[:, cols]
        for k in range(SSD_CONV):
            acc = acc + xp_ref[pl.ds(8 - (SSD_CONV - 1) + k, blk), cols] * cw_ref[k:k + 1, cols]
        return acc * _sigmoid(acc)

    def group(g, carry):
        cols = pl.ds(pl.multiple_of(g * GROUP_W, GROUP_W), GROUP_W)
        xg = conv(cols)
        bg = conv(pl.ds(pl.multiple_of(D_INNER + g * SSD_STATE, SSD_STATE), SSD_STATE))
        cg = conv(pl.ds(pl.multiple_of(D_INNER + GN + g * SSD_STATE, SSD_STATE), SSD_STATE))
        cb16, bb16 = cg.astype(BF16), bg.astype(BF16)
        cbm = lax.dot_general(cb16, bb16, (((1,), (1,)), ((), ())), preferred_element_type=F32)
        sg = state_ref[:, cols]
        heads = pl.ds(pl.multiple_of(g * (GROUP_W // SSD_HEAD_DIM), 8), GROUP_W // SSD_HEAD_DIM)
        ecg = _expand_heads(cumt_ref[heads, :], eg_ref[...])
        edg = _expand_heads(dtt_ref[heads, :], eg_ref[...])
        y_off =jnp.dot(cb16, sg.astype(BF16), preferred_element_type=F32) * jnp.exp(ecg)
        parts = []
        for pr in range(GROUP_W // LANES):
            xpair = xg[:, pr * LANES:(pr + 1) * LANES]
            ec = ecg[:, pr * LANES:(pr + 1) * LANES]
            ec_sw = pltpu.roll(ec, SSD_HEAD_DIM, 1)
            acc = jnp.zeros((blk, LANES), F32)
            for q in range(2):
                head = g * (GROUP_W // SSD_HEAD_DIM) + pr * 2 + q
                cum_l = jnp.where(left, ec, ec_sw) if q == 0 else jnp.where(left, ec_sw, ec)
                cum_s = cumt_ref[pl.ds(head, 1), :]
                decay = jnp.exp(jnp.where(causal, cum_l - cum_s, NEG))
                wmat = (decay * cbm * dtt_ref[pl.ds(head, 1), :]).astype(BF16)
                xm = jnp.where(left if q == 0 else jnp.logical_not(left), xpair, 0.0).astype(BF16)
                acc = acc + jnp.dot(wmat, xm, preferred_element_type=F32)
            parts.append(acc)
        y = jnp.concatenate(parts, axis=1) + y_off
        y = y + xg * dsk_ref[:, cols]
        zg = z_ref[:, cols]
        y = y * (zg * _sigmoid(zg))
        y = y * lax.rsqrt(jnp.mean(y * y, axis=-1, keepdims=True) + EPS) * nw_ref[:, cols]
        y_ref[:, cols] = y.astype(BF16)
        cl = ecg[blk - 1:blk, :]
        xt = (xg * (jnp.exp(cl - ecg) * edg)).astype(BF16)
        h_in = lax.dot_general(bb16, xt, (((0,), (0,)), ((), ())), preferred_element_type=F32)
        state_ref[:, cols] = sg * jnp.exp(cl) + h_in
        return carry

    lax.fori_loop(0, SSD_GROUPS, group, 0)

    @pl.when(c == n_chunks - 1)
    def _():
        so_ref[0] = state_ref[...].T


def ssd_scan(zx, dt, p, s0, c0, *, n_batch, n_chunks, valid_len, shared_init):
    blk = SSD_BLOCK
    n_rows = n_batch * n_chunks * blk
    assert zx.shape == (n_rows, ZX_DIM)
    rowi = lambda b, c: b * n_chunks + c
    init = (lambda b, c: (0, 0, 0)) if shared_init else (lambda b, c: (b, 0, 0))
    const = lambda b, c: (0, 0)
    in_specs = [
        pl.BlockSpec((blk, D_INNER), lambda b, c: (rowi(b, c), 0)),
        pl.BlockSpec((blk, D_INNER), lambda b, c: (rowi(b, c), 1)),
        pl.BlockSpec((blk, GN), lambda b, c: (rowi(b, c), 2 * D_INNER // GN)),
        pl.BlockSpec((blk, GN), lambda b, c: (rowi(b, c), 2 * D_INNER // GN + 1)),
        pl.BlockSpec((blk, LANES), lambda b, c: (rowi(b, c), 0)),
        pl.BlockSpec((8, CONV_DIM), const),
        pl.BlockSpec((1, CONV_DIM), const),
        pl.BlockSpec((1, LANES), const),
        pl.BlockSpec((1, LANES), const),
        pl.BlockSpec((1, D_INNER), const),
        pl.BlockSpec((1, D_INNER), const),
        pl.BlockSpec((32, GROUP_W), const),
        pl.BlockSpec((1, D_INNER, SSD_STATE), init),
        pl.BlockSpec((1, 8, CONV_DIM), init),
    ]
    out_specs = [pl.BlockSpec((blk, D_INNER), lambda b, c: (rowi(b, c), 0)),
                 pl.BlockSpec((1, D_INNER, SSD_STATE), lambda b, c: (b, 0, 0))]
    out_shape = [jax.ShapeDtypeStruct((n_rows, D_INNER), BF16),
                 jax.ShapeDtypeStruct((n_batch, D_INNER, SSD_STATE), F32)]
    scratch = [pltpu.VMEM((SSD_STATE, D_INNER), F32),
               pltpu.VMEM((blk + 8, CONV_DIM), F32),
               pltpu.VMEM((LANES, blk), F32),
               pltpu.VMEM((LANES, blk), F32)]
    kern = functools.partial(_ssd_kernel, valid_len=valid_len, n_chunks=n_chunks)
    return pl.pallas_call(
        kern, grid=(n_batch, n_chunks), in_specs=in_specs, out_specs=out_specs,
        out_shape=out_shape, scratch_shapes=scratch,
        compiler_params=_cparams(("parallel", "arbitrary")), name="ssd_scan")(
            zx, zx, zx, zx, dt, p["conv_w"], p["conv_b"], p["dt_bias"], p["a_log"], p["d_skip"],
            p["norm_w"], p["expand"], s0, c0)


def _attn_core(q, kk, vv, sink_ref, kmask):
    n_q = q.shape[0]
    scale = ATTN_HEAD_DIM ** -0.5
    left = lax.broadcasted_iota(I32, (n_q, LANES), 1) < ATTN_HEAD_DIM
    nt = (((1,), (1,)), ((), ()))
    n_pairs = KV_DIM // LANES
    kps = [kk[:, pr * LANES:(pr + 1) * LANES].astype(BF16) for pr in range(n_pairs)]
    vps = [vv[:, pr * LANES:(pr + 1) * LANES].astype(BF16) for pr in range(n_pairs)]

    def scores(h):
        blk, half = divmod(h, 2)
        qb = q[:, blk * LANES:(blk + 1) * LANES]
        qh = jnp.where(left if half == 0 else jnp.logical_not(left), qb, 0.0).astype(BF16)
        s = lax.dot_general(qh, kps[blk // Q_PER_KV], nt, preferred_element_type=F32) * scale
        return s if kmask is None else jnp.where(kmask, s, NEG)

    def attend(h, s):
        sink = sink_ref[h]
        m = jnp.maximum(jnp.max(s, axis=-1, keepdims=True), sink)
        p = jnp.exp(s - m)
        den = jnp.sum(p, axis=-1, keepdims=True) + jnp.exp(sink - m)
        p = p * (1.0 / den)
        return jnp.dot(p.astype(BF16), vps[h // (2 * Q_PER_KV)], preferred_element_type=F32)

    pending, done = {}, {}
    for h in range(N_HEADS + ATTN_LAG):
        if h < N_HEADS:
            pending[h] = scores(h)
        if h >= ATTN_LAG:
            done[h - ATTN_LAG] = attend(h - ATTN_LAG, pending.pop(h - ATTN_LAG))
    outs = [jnp.where(left, done[2 * b], done[2 * b + 1]) for b in range(N_HEADS // 2)]
    return jnp.concatenate(outs, axis=1)


def _attn_band_kernel(sink_ref, q_ref, k_ref, v_ref, mk_ref, mv_ref, o_ref):
    c = pl.program_id(1)
    band = WINDOW + CHUNK
    first = jnp.maximum(c - WINDOW // CHUNK, 0)
    start = pl.multiple_of(first * CHUNK, CHUNK)
    kk = jnp.concatenate([mk_ref[...], k_ref[pl.ds(start, band), :]], axis=0)
    vv = jnp.concatenate([mv_ref[...], v_ref[pl.ds(start, band), :]], axis=0)
    j = lax.broadcasted_iota(I32, (1, N_META + band), 1)
    key_chunk = first + jnp.maximum(j - N_META, 0) // CHUNK
    kmask = (j < N_META) | (key_chunk <= c)
    o_ref[...] = _attn_core(q_ref[...], kk, vv, sink_ref, kmask).astype(BF16)


def attn_band(qkv, meta_k, meta_v, sinks, *, n_batch, seq):
    n_chunks = seq // CHUNK
    assert seq >= WINDOW + CHUNK
    kcol = Q_DIM // KV_DIM
    grid_spec = pltpu.PrefetchScalarGridSpec(
        num_scalar_prefetch=1, grid=(n_batch, n_chunks),
        in_specs=[pl.BlockSpec((CHUNK, Q_DIM), lambda b, c, s: (b * n_chunks + c, 0)),
                  pl.BlockSpec((seq, KV_DIM), lambda b, c, s: (b, kcol)),
                  pl.BlockSpec((seq, KV_DIM), lambda b, c, s: (b, kcol + 1)),
                  pl.BlockSpec((N_META, KV_DIM), lambda b, c, s: (0, 0)),
                  pl.BlockSpec((N_META, KV_DIM), lambda b, c, s: (0, 0))],
        out_specs=pl.BlockSpec((CHUNK, Q_DIM), lambda b, c, s: (b * n_chunks + c, 0)))
    return pl.pallas_call(
        _attn_band_kernel, grid_spec=grid_spec,
        out_shape=jax.ShapeDtypeStruct((n_batch * seq, Q_DIM), BF16),
        compiler_params=_cparams(("parallel", "arbitrary")), name="attn_band")(
            sinks, qkv, qkv, qkv, meta_k, meta_v)


def _attn_full_kernel(sink_ref, q_ref, k_ref, v_ref, o_ref):
    o_ref[...] = _attn_core(q_ref[...], k_ref[...], v_ref[...], sink_ref, None).astype(BF16)


def attn_full(q, kk, vv, sinks):
    n_batch, n_q, _ = q.shape
    n_k = kk.shape[1]
    grid_spec = pltpu.PrefetchScalarGridSpec(
        num_scalar_prefetch=1, grid=(n_batch,),
        in_specs=[pl.BlockSpec((None, n_q, Q_DIM), lambda b, s: (b, 0, 0)),
                  pl.BlockSpec((None, n_k, KV_DIM), lambda b, s: (b, 0, 0)),
                  pl.BlockSpec((None, n_k, KV_DIM), lambda b, s: (b, 0, 0))],
        out_specs=pl.BlockSpec((None, n_q, Q_DIM), lambda b, s: (b, 0, 0)))
    return pl.pallas_call(
        _attn_full_kernel, grid_spec=grid_spec,
        out_shape=jax.ShapeDtypeStruct((n_batch, n_q, Q_DIM), BF16),
        compiler_params=_cparams(("parallel",)), name="attn_full")(sinks, q, kk, vv)


def _expert_ffn(x16, w_in, w_out):
    gu = jnp.dot(x16, w_in, preferred_element_type=F32)
    a, b = gu[:, :D_EXPERT], gu[:, D_EXPERT:]
    hmid = a * _sigmoid(a) * b
    return jnp.dot(hmid.astype(BF16), w_out, preferred_element_type=F32)


def _moe_dense_kernel(x_ref, e_ref, p_ref, wi_ref, wo_ref, o_ref):
    e = pl.program_id(0)

    @pl.when(e == 0)
    def _():
        o_ref[...] = jnp.zeros_like(o_ref)

    eid, wts = e_ref[...], p_ref[...]
    sel0, sel1 = eid[:, 0:1] == e, eid[:, 1:2] == e
    coef = jnp.where(sel0, wts[:, 0:1], jnp.where(sel1, wts[:, 1:2], 0.0))
    y = _expert_ffn(x_ref[...].astype(BF16), wi_ref[0].astype(BF16), wo_ref[0].astype(BF16))
    o_ref[...] += jnp.where(sel0 | sel1, y * coef, 0.0)


def moe_dense(xn, eid, wts, w_in, w_out):
    n_rows = xn.shape[0]
    full = lambda e: (0, 0)
    return pl.pallas_call(
        _moe_dense_kernel, grid=(N_EXPERTS,),
        in_specs=[pl.BlockSpec((n_rows, D_MODEL), full), pl.BlockSpec((n_rows, LANES), full),
                  pl.BlockSpec((n_rows, LANES), full),
                  pl.BlockSpec((1, D_MODEL, 2 * D_EXPERT), lambda e: (e, 0, 0)),
                  pl.BlockSpec((1, D_EXPERT, D_MODEL), lambda e: (e, 0, 0))],
        out_specs=pl.BlockSpec((n_rows, D_MODEL), full),
        out_shape=jax.ShapeDtypeStruct((n_rows, D_MODEL), F32),
        compiler_params=_cparams(("arbitrary",)), name="moe_dense")(xn, eid, wts, w_in, w_out)


def _moe_sorted_kernel(blk_e_ref, n_used_ref, src_ref, src_next_ref, dst_prev_ref, dst_ref, x_hbm, wi_ref,
                       wo_ref, y_hbm, xbuf, obuf, x16_ref, hmid_ref, wi16_ref, wo16_ref, gsem, ssem, *,
                       tm, n_tok):
    i = pl.program_id(0)
    n_used = n_used_ref[0]
    nc = MOE_NC

    def start_gathers(rows_ref, s, lo=0, hi=tm):
        for r in range(lo, hi):
            pltpu.make_async_copy(x_hbm.at[pl.ds(rows_ref[0, 0, r], 1), :], xbuf.at[s, pl.ds(r, 1), :],
                                  gsem.at[s]).start()

    def start_scatters(rows_ref, s, lo=0, hi=tm):
        for r in range(lo, hi):
            pltpu.make_async_copy(obuf.at[s, pl.ds(r, 1), :], y_hbm.at[pl.ds(rows_ref[0, 0, r], 1), :],
                                  ssem.at[s]).start()

    def wait_block(buf, sem, s):
        pltpu.make_async_copy(buf.at[s], buf.at[s], sem.at[s]).wait()

    @pl.when(i == 0)
    def _():
        obuf[...] = jnp.zeros_like(obuf)
        spare = [pltpu.make_async_copy(obuf.at[s], y_hbm.at[pl.ds(2 * n_tok + s * tm, tm), :], ssem.at[s])
                 for s in range(2)]
        for cp in spare:
            cp.start()
        for cp in spare:
            cp.wait()
        start_gathers(src_ref, 0)

    def step(slot):
        wait_block(xbuf, gsem, slot)

        @pl.when(i >= 1)
        def _():
            wait_block(obuf, ssem, slot)

        n_in, n_out = 2 * D_EXPERT // nc, D_MODEL // nc
        g_step, s_step = tm // n_in, tm // n_out
        x16_ref[...] = xbuf[slot].astype(BF16)
        for k in range(D_EXPERT // nc):
            a = jnp.dot(x16_ref[...], wi16_ref[:, k * nc:(k + 1) * nc], preferred_element_type=F32)
            start_gathers(src_next_ref, 1 - slot, 2 * k * g_step, (2 * k + 1) * g_step)
            b = jnp.dot(x16_ref[...], wi16_ref[:, D_EXPERT + k * nc:D_EXPERT + (k + 1) * nc],
                        preferred_element_type=F32)
            start_gathers(src_next_ref, 1 - slot, (2 * k + 1) * g_step, (2 * k + 2) * g_step)
            hmid_ref[:, k * nc:(k + 1) * nc] = (a * _sigmoid(a) * b).astype(BF16)
        for k in range(n_out):
            obuf[slot, :, k * nc:(k + 1) * nc] = jnp.dot(hmid_ref[...], wo16_ref[:, k * nc:(k + 1) * nc],
                                                         preferred_element_type=F32)
            start_scatters(dst_prev_ref, 1 - slot, k * s_step, (k + 1) * s_step)

        @pl.when(i == n_used - 1)
        def _():
            start_scatters(dst_ref, slot)
            wait_block(obuf, ssem, 1 - slot)
            wait_block(obuf, ssem, slot)
            wait_block(xbuf, gsem, 1 - slot)

    @pl.when((i < n_used) & ((i == 0) | (blk_e_ref[i] != blk_e_ref[jnp.maximum(i - 1, 0)])))
    def _():
        wi16_ref[...] = wi_ref[0].astype(BF16)
        wo16_ref[...] = wo_ref[0].astype(BF16)

    for parity in range(2):
        pl.when((i < n_used) & (i % 2 == parity))(functools.partial(step, parity))


def moe_sorted(xn, w_in, w_out, tables, *, tm):
    blk_e, n_used, src, dst = tables
    n_tok = xn.shape[0]
    n_blocks = blk_e.shape[0]
    smem = lambda index_map: pl.BlockSpec((1, 1, tm), index_map, memory_space=pltpu.SMEM)
    nxt = lambda i, be, nu: (jnp.maximum(jnp.minimum(i + 1, nu[0] - 1), 0), 0, 0)
    grid_spec = pltpu.PrefetchScalarGridSpec(
        num_scalar_prefetch=2, grid=(n_blocks,),
        in_specs=[smem(lambda i, be, nu: (i, 0, 0)), smem(nxt),
                  smem(lambda i, be, nu: (i, 0, 0)), smem(lambda i, be, nu: (i + 1, 0, 0)),
                  pl.BlockSpec(memory_space=pl.ANY),
                  pl.BlockSpec((1, D_MODEL, 2 * D_EXPERT), lambda i, be, nu: (be[i], 0, 0)),
                  pl.BlockSpec((1, D_EXPERT, D_MODEL), lambda i, be, nu: (be[i], 0, 0))],
        out_specs=pl.BlockSpec(memory_space=pl.ANY),
        scratch_shapes=[pltpu.VMEM((2, tm, D_MODEL), F32), pltpu.VMEM((2, tm, D_MODEL), F32),
                        pltpu.VMEM((tm, D_MODEL), BF16), pltpu.VMEM((tm, D_EXPERT), BF16),
                        pltpu.VMEM((D_MODEL, 2 * D_EXPERT), BF16), pltpu.VMEM((D_EXPERT, D_MODEL), BF16),
                        pltpu.SemaphoreType.DMA((2,)), pltpu.SemaphoreType.DMA((2,))])
    return pl.pallas_call(
        functools.partial(_moe_sorted_kernel, tm=tm, n_tok=n_tok), grid_spec=grid_spec,
        out_shape=jax.ShapeDtypeStruct((2 * n_tok + 2 * tm, D_MODEL), F32),
        compiler_params=_cparams(("arbitrary",)), name="moe_sorted")(
            blk_e, n_used, src, src, dst, dst, xn, w_in, w_out)


def route_tables(eid, *, tm):
    n_tok = eid.shape[0]
    n_asg = 2 * n_tok
    flat = eid.reshape(-1)
    skey = jnp.sort(flat * n_asg + jnp.arange(n_asg, dtype=I32))
    order = skey % n_asg
    counts = jnp.sum((flat[:, None] == jnp.arange(N_EXPERTS, dtype=I32)[None, :]).astype(I32), axis=0)
    starts = jnp.cumsum(counts) - counts
    padded = (counts + tm - 1) // tm * tm
    pend = jnp.cumsum(padded)
    pstarts = pend - padded
    n_blocks = -(-(n_asg + N_EXPERTS * (tm - 1)) // tm)
    blk = jnp.arange(n_blocks, dtype=I32)
    blk_e = jnp.minimum(jnp.sum((pend[None, :] <= (blk * tm)[:, None]).astype(I32), axis=1), N_EXPERTS - 1)
    blk_off = blk * tm - pstarts[blk_e]
    blk_n = jnp.clip(counts[blk_e] - blk_off, 0, tm)
    r = jnp.arange(tm, dtype=I32)[None, :]
    valid = r < blk_n[:, None]
    code = order[jnp.clip((starts[blk_e] + blk_off)[:, None] + r, 0, n_asg - 1)]
    src = jnp.where(valid, code >> 1, 0)
    spare = 2 * n_tok + (blk % 2)[:, None] * tm + r
    dst = jnp.where(valid, (code & 1) * n_tok + (code >> 1), spare)
    dst = jnp.concatenate([2 * n_tok + tm + r, dst], axis=0)
    n_used = (pend[-1] // tm).reshape(1)
    return (blk_e.astype(I32), n_used.astype(I32), src.reshape(n_blocks, 1, tm).astype(I32),
            dst.reshape(n_blocks + 1, 1, tm).astype(I32))


def _q_head_order():
    order = []
    for pr in range(N_KV_HEADS // 2):
        for r in range(Q_PER_KV):
            for half in range(2):
                order.append((2 * pr + half) * Q_PER_KV + r)
    return jnp.asarray(order, dtype=I32)


def _pad_lanes(a):
    return jnp.pad(a, [(0, 0)] * (a.ndim - 1) + [(0, LANES - a.shape[-1])])


def _ssd_params(conv_w, conv_b, dt_bias, a_log, d_skip, norm_w):
    head_of_col = jnp.arange(GROUP_W, dtype=I32) // SSD_HEAD_DIM
    row = jnp.arange(32, dtype=I32)
    expand = ((row[:, None] % 8 == head_of_col[None, :]) & (row[:, None] < 24)).astype(BF16)
    return {
        "conv_w": jnp.pad(conv_w, ((0, 8 - SSD_CONV), (0, 0))),
        "conv_b": conv_b.reshape(1, CONV_DIM),
        "dt_bias": _pad_lanes(dt_bias.reshape(1, SSD_HEADS)),
        "a_log": _pad_lanes(a_log.reshape(1, SSD_HEADS)),
        "d_skip": jnp.repeat(d_skip, SSD_HEAD_DIM).reshape(1, D_INNER),
        "norm_w": norm_w.reshape(1, D_INNER),
        "expand": expand,
    }


def _router_params(w_group, b_group, w_router, b_router):
    wr = _pad_lanes(jnp.concatenate([w_group, w_router], axis=1)).astype(BF16)
    br = _pad_lanes(jnp.concatenate([b_group, b_router]).reshape(1, -1)).astype(F32)
    return wr, br


def _pick_tile(n, prefs):
    for t in prefs:
        if n % t == 0:
            return t
    return n


def kernel(x_prompt, x_sample, cache_ssm_state, cache_ssm_conv, cache_attn_k, cache_attn_v, cache_meta_k, cache_meta_v, meta_tokens, norm_mix, norm_ffn, norm_final, ssd_w_in, ssd_conv_w, ssd_conv_b, ssd_dt_bias, ssd_a_log, ssd_d, ssd_norm, ssd_w_out, attn_w_qkv, attn_sinks, attn_w_out, moe_w_group, moe_b_group, moe_w_router, moe_b_router, moe_w_in, moe_w_out):
    n_batch, seq, _ = x_prompt.shape
    dec_batch, dec_seq, _ = x_sample.shape
    n_real = n_batch * seq
    n_small = N_META + dec_batch * dec_seq
    blk = SSD_BLOCK
    assert seq % blk == 0 and dec_seq <= blk and N_META <= blk
    tm_big = _pick_tile(n_real, (1024, 512, 256))
    tm_mid = _pick_tile(n_real, (512, 256))

    h_real = x_prompt.reshape(n_real, D_MODEL)
    h_small = jnp.concatenate([meta_tokens, x_sample.reshape(-1, D_MODEL)], axis=0)

    w_in16 = ssd_w_in[0].astype(BF16)
    w_zx, w_dt = w_in16[:, :ZX_DIM], _pad_lanes(w_in16[:, ZX_DIM:])
    sp = _ssd_params(ssd_conv_w[0], ssd_conv_b[0], ssd_dt_bias[0], ssd_a_log[0], ssd_d[0], ssd_norm[0])
    w_out16 = ssd_w_out[0].astype(BF16)
    wr0, br0 = _router_params(moe_w_group[0], moe_b_group[0], moe_w_router[0], moe_b_router[0])

    zx_s, dt_s = norm_matmul(h_small, norm_mix[0], w_zx, tm=n_small, tn=512, w_extra=w_dt)
    zx_r, dt_r = norm_matmul(h_real, norm_mix[0], w_zx, tm=tm_big, tn=1024, w_extra=w_dt)

    n_sb = 1 + dec_batch

    def to_blocks(a, n_valid_meta=N_META):
        meta = jnp.pad(a[:N_META], ((0, blk - N_META), (0, 0)))[None]
        samp = jnp.pad(a[N_META:].reshape(dec_batch, dec_seq, -1), ((0, 0), (0, blk - dec_seq), (0, 0)))
        return jnp.concatenate([meta, samp], axis=0).reshape(n_sb * blk, -1)

    assert dec_seq == N_META, "small-path blocks share one valid length"
    s0_small = jnp.concatenate([jnp.zeros((1, D_INNER, SSD_STATE), F32),
                                cache_ssm_state[0].reshape(dec_batch, D_INNER, SSD_STATE)], axis=0)
    c0_small = jnp.pad(jnp.concatenate([jnp.zeros((1, SSD_CONV - 1, CONV_DIM), F32), cache_ssm_conv[0]], axis=0),
                       ((0, 0), (8 - (SSD_CONV - 1), 0), (0, 0)))
    yn_sb, state_small = ssd_scan(to_blocks(zx_s), to_blocks(dt_s), sp, s0_small, c0_small,
                                  n_batch=n_sb, n_chunks=1, valid_len=dec_seq, shared_init=False)
    yn_sb = yn_sb.reshape(n_sb, blk, D_INNER)
    yn_small = jnp.concatenate([yn_sb[0, :N_META], yn_sb[1:, :dec_seq].reshape(-1, D_INNER)], axis=0)

    c0_real = jnp.pad(zx_s[N_META - (SSD_CONV - 1):N_META, D_INNER:], ((8 - (SSD_CONV - 1), 0), (0, 0)))[None]
    yn_real, state_real = ssd_scan(zx_r, dt_r, sp, state_small[0:1], c0_real,
                                   n_batch=n_batch, n_chunks=seq // blk, valid_len=blk, shared_init=True)

    tk0 = 2048
    h1_s, xn_s, eid_s, wts_s = matmul_route(yn_small, w_out16, h_small, norm_ffn[0], wr0, br0, tm=n_small, tk=tk0)
    h1_r, xn_r, eid_r, wts_r = matmul_route(yn_real, w_out16, h_real, norm_ffn[0], wr0, br0, tm=tm_mid, tk=tk0)

    wi0, wo0 = moe_w_in[0], moe_w_out[0]
    moe_s = moe_dense(xn_s, eid_s, wts_s, wi0, wo0)
    moe_r = moe_sorted(xn_r, wi0, wo0, route_tables(eid_r[:, :2], tm=MOE_TM), tm=MOE_TM)

    head_order = _q_head_order()
    wq = attn_w_qkv[0][:, :Q_DIM].reshape(D_MODEL, N_HEADS, ATTN_HEAD_DIM)[:, head_order].reshape(D_MODEL, Q_DIM)
    w_qkv16 = jnp.concatenate([wq, attn_w_qkv[0][:, Q_DIM:]], axis=1).astype(BF16)
    sinks = attn_sinks[0][head_order].astype(F32)
    w_ao16 = attn_w_out[0].reshape(N_HEADS, ATTN_HEAD_DIM, D_MODEL)[head_order].reshape(Q_DIM, D_MODEL).astype(BF16)
    wr1, br1 = _router_params(moe_w_group[1], moe_b_group[1], moe_w_router[1], moe_b_router[1])

    qkv_s, h2_s = norm_matmul(h1_s, norm_mix[1], w_qkv16, tm=n_small, tn=512, mode="sum", moe=(moe_s,),
                              write_h=True)
    qkv_r, h2_r = norm_matmul(h1_r, norm_mix[1], w_qkv16, tm=tm_mid, tn=1024, mode="pair",
                              moe=(moe_r, wts_r), write_h=True)

    k_meta, v_meta = qkv_s[:N_META, Q_DIM:Q_DIM + KV_DIM], qkv_s[:N_META, Q_DIM + KV_DIM:]
    k_new = qkv_s[N_META:, Q_DIM:Q_DIM + KV_DIM].reshape(dec_batch, dec_seq, KV_DIM)
    v_new = qkv_s[N_META:, Q_DIM + KV_DIM:].reshape(dec_batch, dec_seq, KV_DIM)
    o_meta = attn_full(qkv_s[None, :N_META, :Q_DIM], k_meta[None], v_meta[None], sinks)
    kk_s = jnp.concatenate([cache_meta_k[0].reshape(dec_batch, N_META, KV_DIM),
                            cache_attn_k[0].reshape(dec_batch, WINDOW, KV_DIM), k_new], axis=1)
    vv_s = jnp.concatenate([cache_meta_v[0].reshape(dec_batch, N_META, KV_DIM),
                            cache_attn_v[0].reshape(dec_batch, WINDOW, KV_DIM), v_new], axis=1)
    o_samp = attn_full(qkv_s[N_META:, :Q_DIM].reshape(dec_batch, dec_seq, Q_DIM), kk_s, vv_s, sinks)
    o_small = jnp.concatenate([o_meta.reshape(N_META, Q_DIM), o_samp.reshape(-1, Q_DIM)], axis=0)
    o_real = attn_band(qkv_r, k_meta, v_meta, sinks, n_batch=n_batch, seq=seq)

    h3_s, xn_s, eid_s, wts_s = matmul_route(o_small, w_ao16, h2_s, norm_ffn[1], wr1, br1, tm=n_small, tk=1024)
    h3_r, xn_r, eid_r, wts_r = matmul_route(o_real, w_ao16, h2_r, norm_ffn[1], wr1, br1, tm=tm_mid, tk=2048)

    wi1, wo1 = moe_w_in[1], moe_w_out[1]
    moe_s = moe_dense(xn_s, eid_s, wts_s, wi1, wo1)
    moe_r = moe_sorted(xn_r, wi1, wo1, route_tables(eid_r[:, :2], tm=MOE_TM), tm=MOE_TM)

    y_small = final_norm(h3_s, norm_final, tm=n_small, mode="sum", moe=(moe_s,))
    y_real = final_norm(h3_r, norm_final, tm=tm_mid, mode="pair", moe=(moe_r, wts_r))

    kv_shape = (N_KV_HEADS, ATTN_HEAD_DIM)
    qkv_r3 = qkv_r.reshape(n_batch, seq, -1)
    zx_r3 = zx_r.reshape(n_batch, seq, ZX_DIM)
    zx_samp = zx_s[N_META:].reshape(dec_batch, dec_seq, ZX_DIM)
    return (
        y_real.reshape(n_batch, seq, D_MODEL),
        y_small[N_META:].reshape(dec_batch, dec_seq, D_MODEL),
        state_real.reshape(1, n_batch, SSD_HEADS, SSD_HEAD_DIM, SSD_STATE),
        zx_r3[None, :, seq - (SSD_CONV - 1):, D_INNER:],
        qkv_r3[:, seq - WINDOW:, Q_DIM:Q_DIM + KV_DIM].reshape((1, n_batch, WINDOW) + kv_shape),
        qkv_r3[:, seq - WINDOW:, Q_DIM + KV_DIM:].reshape((1, n_batch, WINDOW) + kv_shape),
        jnp.broadcast_to(k_meta.reshape((1, 1, N_META) + kv_shape), (1, n_batch, N_META) + kv_shape),
        jnp.broadcast_to(v_meta.reshape((1, 1, N_META) + kv_shape), (1, n_batch, N_META) + kv_shape),
        state_small[1:].reshape(1, dec_batch, SSD_HEADS, SSD_HEAD_DIM, SSD_STATE),
        zx_samp[None, :, dec_seq - (SSD_CONV - 1):, D_INNER:],
        k_new.reshape((1, dec_batch, dec_seq) + kv_shape),
        v_new.reshape((1, dec_batch, dec_seq) + kv_shape),
    )
```

```python
import functools

import jax
import jax.numpy as jnp
from jax import lax
from jax.experimental import pallas as pl
from jax.experimental.pallas import tpu as pltpu

F32, BF16, I32 = jnp.float32, jnp.bfloat16, jnp.int32

D_MODEL = 2048
N_META = 16
EPS = 1e-5
D_INNER = 4096
SSD_HEAD_DIM = 64
SSD_HEADS = 64
SSD_GROUPS = 8
SSD_STATE = 128
SSD_CONV = 4
GN = SSD_GROUPS * SSD_STATE
CONV_DIM = D_INNER + 2 * GN
ZX_DIM = D_INNER + CONV_DIM
GROUP_W = D_INNER // SSD_GROUPS
WINDOW = 128
CHUNK = 64
ATTN_HEAD_DIM = 64
N_HEADS = 32
N_KV_HEADS = 8
Q_PER_KV = N_HEADS // N_KV_HEADS
Q_DIM = N_HEADS * ATTN_HEAD_DIM
KV_DIM = N_KV_HEADS * ATTN_HEAD_DIM
N_GROUPS = 4
EXPERTS_PER_GROUP = 8
N_EXPERTS = 32
D_EXPERT = 512

LANES = 128
SSD_BLOCK = 128
NEG = -1e30
VMEM_LIMIT = 56 * 1024 * 1024
MOE_TM = 256
MOE_NC = 256
ATTN_CHUNKS = 4
ATTN_LAG = ATTN_CHUNKS * N_HEADS


def _cparams(sem):
    return pltpu.CompilerParams(dimension_semantics=sem, vmem_limit_bytes=VMEM_LIMIT)


def _resident_spec(block_shape, index_map, *, resident):
    if resident:
        return pl.BlockSpec(block_shape, index_map, pipeline_mode=pl.Buffered(1))
    return pl.BlockSpec(block_shape, index_map)


def _sigmoid(x):
    return 1.0 / (1.0 + jnp.exp(-x))


def _rms(h, g):
    var = jnp.mean(h * h, axis=-1, keepdims=True)
    return h * lax.rsqrt(var + EPS) * g


def _row_chunk(tm):
    for rc in (256, 128, 64, 32, 16, 8):
        if tm % rc == 0:
            return rc
    raise ValueError(tm)


_N_MOE_REFS = {"none": 0, "sum": 1, "pair": 3}


def _combine_rows(x_ref, moe_refs, mode, rows):
    h = x_ref[rows, :]
    if mode == "sum":
        h = h + moe_refs[0][rows, :]
    elif mode == "pair":
        y0, y1, w = moe_refs
        wv = w[rows, :]
        h = h + (y0[rows, :] * wv[:, 0:1] + y1[rows, :] * wv[:, 1:2])
    return h


def _norm_mm_kernel(*refs, mode, write_h, has_extra, tm):
    it = iter(refs)
    x_ref = next(it)
    moe_refs = [next(it) for _ in range(_N_MOE_REFS[mode])]
    g_ref, w_ref = next(it), next(it)
    we_ref = next(it) if has_extra else None
    o_ref = next(it)
    h_ref = next(it) if write_h else None
    e_ref = next(it) if has_extra else None
    xn_ref = next(it)
    rc = _row_chunk(tm)

    @pl.when(pl.program_id(1) == 0)
    def _():
        def body(r, carry):
            rows = pl.ds(pl.multiple_of(r * rc, rc), rc)
            h = _combine_rows(x_ref, moe_refs, mode, rows)
            if write_h:
                h_ref[rows, :] = h
            xn_ref[rows, :] = _rms(h, g_ref[...]).astype(BF16)
            return carry

        lax.fori_loop(0, tm // rc, body, 0)
        if has_extra:
            e_ref[...] = jnp.dot(xn_ref[...], we_ref[...], preferred_element_type=F32)

    o_ref[...] = jnp.dot(xn_ref[...], w_ref[...], preferred_element_type=F32)


def _moe_in_specs(mode, moe, tm, n_rows):
    if mode == "none":
        return [], []
    if mode == "sum":
        return [pl.BlockSpec((tm, D_MODEL), lambda i, j: (i, 0))], [moe[0]]
    y2, wts = moe
    off = n_rows // tm
    specs = [pl.BlockSpec((tm, D_MODEL), lambda i, j: (i, 0)),
             pl.BlockSpec((tm, D_MODEL), lambda i, j: (i + off, 0)),
             pl.BlockSpec((tm, LANES), lambda i, j: (i, 0))]
    return specs, [y2, y2, wts]


def norm_matmul(x, g, w, *, tm, tn, mode="none", moe=(), w_extra=None, write_h=False):
    n_rows, n_out = x.shape[0], w.shape[1]
    assert n_rows % tm == 0 and n_out % tn == 0
    moe_specs, moe_ops = _moe_in_specs(mode, moe, tm, n_rows)
    in_specs = [pl.BlockSpec((tm, D_MODEL), lambda i, j: (i, 0))] + moe_specs + [
        pl.BlockSpec((1, D_MODEL), lambda i, j: (0, 0)),
        _resident_spec((D_MODEL, tn), lambda i, j: (0, j), resident=n_out == tn)]
    operands = [x] + moe_ops + [g.reshape(1, D_MODEL), w]
    out_shape = [jax.ShapeDtypeStruct((n_rows, n_out), F32)]
    out_specs = [pl.BlockSpec((tm, tn), lambda i, j: (i, j))]
    if w_extra is not None:
        in_specs.append(pl.BlockSpec((D_MODEL, LANES), lambda i, j: (0, 0)))
        operands.append(w_extra)
    if write_h:
        out_shape.append(jax.ShapeDtypeStruct((n_rows, D_MODEL), F32))
        out_specs.append(pl.BlockSpec((tm, D_MODEL), lambda i, j: (i, 0)))
    if w_extra is not None:
        out_shape.append(jax.ShapeDtypeStruct((n_rows, LANES), F32))
        out_specs.append(pl.BlockSpec((tm, LANES), lambda i, j: (i, 0)))
    kern = functools.partial(_norm_mm_kernel, mode=mode, write_h=write_h,
                             has_extra=w_extra is not None, tm=tm)
    return pl.pallas_call(
        kern, grid=(n_rows // tm, n_out // tn), in_specs=in_specs, out_specs=out_specs,
        out_shape=out_shape, scratch_shapes=[pltpu.VMEM((tm, D_MODEL), BF16)],
        compiler_params=_cparams(("parallel", "arbitrary")), name="norm_matmul")(*operands)


def _final_norm_kernel(*refs, mode, tm):
    x_ref = refs[0]
    moe_refs = refs[1:1 + _N_MOE_REFS[mode]]
    g_ref, o_ref = refs[-2], refs[-1]
    rc = _row_chunk(tm)

    def body(r, carry):
        rows = pl.ds(pl.multiple_of(r * rc, rc), rc)
        o_ref[rows, :] = _rms(_combine_rows(x_ref, moe_refs, mode, rows), g_ref[...])
        return carry

    lax.fori_loop(0, tm // rc, body, 0)


def final_norm(x, g, *, tm, mode, moe):
    n_rows = x.shape[0]
    moe_specs, moe_ops = _moe_in_specs(mode, moe, tm, n_rows)
    in_specs = [pl.BlockSpec((tm, D_MODEL), lambda i, j: (i, 0))] + moe_specs + [
        pl.BlockSpec((1, D_MODEL), lambda i, j: (0, 0))]
    return pl.pallas_call(
        functools.partial(_final_norm_kernel, mode=mode, tm=tm), grid=(n_rows // tm, 1),
        in_specs=in_specs, out_specs=pl.BlockSpec((tm, D_MODEL), lambda i, j: (i, 0)),
        out_shape=jax.ShapeDtypeStruct((n_rows, D_MODEL), F32),
        compiler_params=_cparams(("parallel", "arbitrary")), name="final_norm")(
            x, *moe_ops, g.reshape(1, D_MODEL))


def _route(lg):
    lane = lax.broadcasted_iota(I32, lg.shape, 1)
    is_g = lane < N_GROUPS
    gl = jnp.where(is_g, lg, NEG)
    gmax = jnp.max(gl, axis=-1, keepdims=True)
    gsum = jnp.sum(jnp.where(is_g, jnp.exp(gl - gmax), 0.0), axis=-1, keepdims=True)
    gate = 1.0 / gsum
    gidx = jnp.min(jnp.where(gl == gmax, lane, LANES), axis=-1, keepdims=True)
    ecol = lane - N_GROUPS
    in_group = (ecol >= 0) & (ecol < N_EXPERTS) & ((ecol >> 3) == gidx)
    el = jnp.where(in_group, lg, NEG)
    v1 = jnp.max(el, axis=-1, keepdims=True)
    i1 = jnp.min(jnp.where(el == v1, lane, LANES), axis=-1, keepdims=True)
    el2 = jnp.where(lane == i1, NEG, el)
    v2 = jnp.max(el2, axis=-1, keepdims=True)
    i2 = jnp.min(jnp.where(el2 == v2, lane, LANES), axis=-1, keepdims=True)
    e2 = jnp.exp(v2 - v1)
    w1 = gate / (1.0 + e2)
    w2 = gate * e2 / (1.0 + e2)
    eid = jnp.where(lane == 0, i1 - N_GROUPS, jnp.where(lane == 1, i2 - N_GROUPS, 0))
    wts = jnp.where(lane == 0, w1, jnp.where(lane == 1, w2, 0.0))
    return eid, wts


def _mm_route_kernel(a_ref, w_ref, h_ref, g_ref, wr_ref, br_ref,
                     h1_ref, xn_ref, e_ref, p_ref, acc_ref, *, nk, tm):
    k = pl.program_id(1)
    rc = _row_chunk(tm)

    @pl.when(k == 0)
    def _():
        acc_ref[...] = jnp.zeros_like(acc_ref)

    acc_ref[...] += jnp.dot(a_ref[...], w_ref[...], preferred_element_type=F32)

    @pl.when(k == nk - 1)
    def _():
        def body(r, carry):
            rows = pl.ds(pl.multiple_of(r * rc, rc), rc)
            h1 = h_ref[rows, :] + acc_ref[rows, :]
            h1_ref[rows, :] = h1
            xn = _rms(h1, g_ref[...])
            xn_ref[rows, :] = xn
            lg = jnp.dot(xn.astype(BF16), wr_ref[...], preferred_element_type=F32) + br_ref[...]
            eid, wts = _route(lg)
            e_ref[rows, :] = eid
            p_ref[rows, :] = wts
            return carry

        lax.fori_loop(0, tm // rc, body, 0)


def matmul_route(a, w, h, g, wr, br, *, tm, tk):
    n_rows, kdim = a.shape
    assert n_rows % tm == 0 and kdim % tk == 0
    nk = kdim // tk
    row_spec = pl.BlockSpec((tm, D_MODEL), lambda i, k: (i, 0))
    lane_spec = pl.BlockSpec((tm, LANES), lambda i, k: (i, 0))
    return pl.pallas_call(
        functools.partial(_mm_route_kernel, nk=nk, tm=tm), grid=(n_rows // tm, nk),
        in_specs=[pl.BlockSpec((tm, tk), lambda i, k: (i, k)),
                  _resident_spec((tk, D_MODEL), lambda i, k: (k, 0), resident=nk == 1),
                  row_spec,
                  pl.BlockSpec((1, D_MODEL), lambda i, k: (0, 0)),
                  pl.BlockSpec((D_MODEL, LANES), lambda i, k: (0, 0)),
                  pl.BlockSpec((1, LANES), lambda i, k: (0, 0))],
        out_specs=[row_spec, row_spec, lane_spec, lane_spec],
        out_shape=[jax.ShapeDtypeStruct((n_rows, D_MODEL), F32),
                   jax.ShapeDtypeStruct((n_rows, D_MODEL), F32),
                   jax.ShapeDtypeStruct((n_rows, LANES), I32),
                   jax.ShapeDtypeStruct((n_rows, LANES), F32)],
        scratch_shapes=[pltpu.VMEM((tm, D_MODEL), F32)],
        compiler_params=_cparams(("parallel", "arbitrary")), name="matmul_route")(
            a, w, h, g.reshape(1, D_MODEL), wr, br)


def _expand_heads(v, eg):
    v1 = v.astype(BF16).astype(F32)
    r1 = v - v1
    v2 = r1.astype(BF16).astype(F32)
    v3 = (r1 - v2).astype(BF16).astype(F32)
    parts = jnp.concatenate([v1, v2, v3, jnp.zeros_like(v)], axis=0).astype(BF16)
    return lax.dot_general(parts, eg, (((0,), (0,)), ((), ())), preferred_element_type=F32)


def _ssd_kernel(z_ref, x_ref, b_ref, c_ref, dt_ref, cw_ref, cb_ref, dtb_ref, alog_ref, dsk_ref,
                nw_ref, eg_ref, s0_ref, c0_ref, y_ref, so_ref,
                state_ref, xp_ref, cumt_ref, dtt_ref, *, valid_len, n_chunks):
    blk = SSD_BLOCK
    c = pl.program_id(1)

    @pl.when(c == 0)
    def _():
        state_ref[...] = s0_ref[0].T
        xp_ref[0:8, :] = c0_ref[0]

    @pl.when(c > 0)
    def _():
        xp_ref[0:8, :] = xp_ref[blk:blk + 8, :]

    xp_ref[8:8 + blk, 0:D_INNER] = x_ref[...]
    xp_ref[8:8 + blk, D_INNER:D_INNER + GN] = b_ref[...]
    xp_ref[8:8 + blk, D_INNER + GN:CONV_DIM] = c_ref[...]

    row = lax.broadcasted_iota(I32, (blk, LANES), 0)
    lane = lax.broadcasted_iota(I32, (blk, LANES), 1)
    dtr = dt_ref[...] + dtb_ref[...]
    dt = jnp.maximum(dtr, 0.0) + jnp.log1p(jnp.exp(-jnp.abs(dtr)))
    if valid_len < blk:
        dt = jnp.where(row < valid_len, dt, 0.0)
    cum = dt * (-jnp.exp(alog_ref[...]))
    sh = 1
    while sh < blk:
        cum = cum + jnp.where(row >= sh, pltpu.roll(cum, sh, 0), 0.0)
        sh *= 2
    cumt_ref[...] = cum.T
    dtt_ref[...] = dt.T

    causal = lane <= row
    left = lane < SSD_HEAD_DIM

    def conv(cols):
        acc = cb_ref[:, cols]
        for k in range(SSD_CONV):
            acc = acc + xp_ref[pl.ds(8 - (SSD_CONV - 1) + k, blk), cols] * cw_ref[k:k + 1, cols]
        return acc * _sigmoid(acc)

    def group(g, carry):
        cols = pl.ds(pl.multiple_of(g * GROUP_W, GROUP_W), GROUP_W)
        xg = conv(cols)
        bg = conv(pl.ds(pl.multiple_of(D_INNER + g * SSD_STATE, SSD_STATE), SSD_STATE))
        cg = conv(pl.ds(pl.multiple_of(D_INNER + GN + g * SSD_STATE, SSD_STATE), SSD_STATE))
        cb16, bb16 = cg.astype(BF16), bg.astype(BF16)
        cbm = lax.dot_general(cb16, bb16, (((1,), (1,)), ((), ())), preferred_element_type=F32)
        sg = state_ref[:, cols]
        heads = pl.ds(pl.multiple_of(g * (GROUP_W // SSD_HEAD_DIM), 8), GROUP_W // SSD_HEAD_DIM)
        ecg = _expand_heads(cumt_ref[heads, :], eg_ref[...])
        edg = _expand_heads(dtt_ref[heads, :], eg_ref[...])
        y_off =jnp.dot(cb16, sg.astype(BF16), preferred_element_type=F32) * jnp.exp(ecg)
        parts = []
        for pr in range(GROUP_W // LANES):
            xpair = xg[:, pr * LANES:(pr + 1) * LANES]
            ec = ecg[:, pr * LANES:(pr + 1) * LANES]
            ec_sw = pltpu.roll(ec, SSD_HEAD_DIM, 1)
            acc = jnp.zeros((blk, LANES), F32)
            for q in range(2):
                head = g * (GROUP_W // SSD_HEAD_DIM) + pr * 2 + q
                cum_l = jnp.where(left, ec, ec_sw) if q == 0 else jnp.where(left, ec_sw, ec)
                cum_s = cumt_ref[pl.ds(head, 1), :]
                decay = jnp.exp(jnp.where(causal, cum_l - cum_s, NEG))
                wmat = (decay * cbm * dtt_ref[pl.ds(head, 1), :]).astype(BF16)
                xm = jnp.where(left if q == 0 else jnp.logical_not(left), xpair, 0.0).astype(BF16)
                acc = acc + jnp.dot(wmat, xm, preferred_element_type=F32)
            parts.append(acc)
        y = jnp.concatenate(parts, axis=1) + y_off
        y = y + xg * dsk_ref[:, cols]
        zg = z_ref[:, cols]
        y = y * (zg * _sigmoid(zg))
        y = y * lax.rsqrt(jnp.mean(y * y, axis=-1, keepdims=True) + EPS) * nw_ref[:, cols]
        y_ref[:, cols] = y.astype(BF16)
        cl = ecg[blk - 1:blk, :]
        xt = (xg * (jnp.exp(cl - ecg) * edg)).astype(BF16)
        h_in = lax.dot_general(bb16, xt, (((0,), (0,)), ((), ())), preferred_element_type=F32)
        state_ref[:, cols] = sg * jnp.exp(cl) + h_in
        return carry

    lax.fori_loop(0, SSD_GROUPS, group, 0, unroll=2)

    @pl.when(c == n_chunks - 1)
    def _():
        so_ref[0] = state_ref[...].T


def ssd_scan(zx, dt, p, s0, c0, *, n_batch, n_chunks, valid_len, shared_init):
    blk = SSD_BLOCK
    n_rows = n_batch * n_chunks * blk
    assert zx.shape == (n_rows, ZX_DIM)
    rowi = lambda b, c: b * n_chunks + c
    init = (lambda b, c: (0, 0, 0)) if shared_init else (lambda b, c: (b, 0, 0))
    const = lambda b, c: (0, 0)
    in_specs = [
        pl.BlockSpec((blk, D_INNER), lambda b, c: (rowi(b, c), 0)),
        pl.BlockSpec((blk, D_INNER), lambda b, c: (rowi(b, c), 1)),
        pl.BlockSpec((blk, GN), lambda b, c: (rowi(b, c), 2 * D_INNER // GN)),
        pl.BlockSpec((blk, GN), lambda b, c: (rowi(b, c), 2 * D_INNER // GN + 1)),
        pl.BlockSpec((blk, LANES), lambda b, c: (rowi(b, c), 0)),
        pl.BlockSpec((8, CONV_DIM), const),
        pl.BlockSpec((1, CONV_DIM), const),
        pl.BlockSpec((1, LANES), const),
        pl.BlockSpec((1, LANES), const),
        pl.BlockSpec((1, D_INNER), const),
        pl.BlockSpec((1, D_INNER), const),
        pl.BlockSpec((32, GROUP_W), const),
        pl.BlockSpec((1, D_INNER, SSD_STATE), init),
        pl.BlockSpec((1, 8, CONV_DIM), init),
    ]
    out_specs = [pl.BlockSpec((blk, D_INNER), lambda b, c: (rowi(b, c), 0)),
                 pl.BlockSpec((1, D_INNER, SSD_STATE), lambda b, c: (b, 0, 0))]
    out_shape = [jax.ShapeDtypeStruct((n_rows, D_INNER), BF16),
                 jax.ShapeDtypeStruct((n_batch, D_INNER, SSD_STATE), F32)]
    scratch = [pltpu.VMEM((SSD_STATE, D_INNER), F32),
               pltpu.VMEM((blk + 8, CONV_DIM), F32),
               pltpu.VMEM((LANES, blk), F32),
               pltpu.VMEM((LANES, blk), F32)]
    kern = functools.partial(_ssd_kernel, valid_len=valid_len, n_chunks=n_chunks)
    return pl.pallas_call(
        kern, grid=(n_batch, n_chunks), in_specs=in_specs, out_specs=out_specs,
        out_shape=out_shape, scratch_shapes=scratch,
        compiler_params=_cparams(("parallel", "arbitrary")), name="ssd_scan")(
            zx, zx, zx, zx, dt, p["conv_w"], p["conv_b"], p["dt_bias"], p["a_log"], p["d_skip"],
            p["norm_w"], p["expand"], s0, c0)


def _attn_core(items, sink_ref):
    n_q = items[0][0].shape[0]
    scale = ATTN_HEAD_DIM ** -0.5
    left = lax.broadcasted_iota(I32, (n_q, LANES), 1) < ATTN_HEAD_DIM
    nt = (((1,), (1,)), ((), ()))
    n_pairs = KV_DIM // LANES
    kps = [[kk[:, pr * LANES:(pr + 1) * LANES].astype(BF16) for pr in range(n_pairs)] for _, kk, _, _ in items]
    vps = [[vv[:, pr * LANES:(pr + 1) * LANES].astype(BF16) for pr in range(n_pairs)] for _, _, vv, _ in items]

    def scores(t, h):
        q, kmask = items[t][0], items[t][3]
        blk, half = divmod(h, 2)
        qb = q[:, blk * LANES:(blk + 1) * LANES]
        qh = jnp.where(left if half == 0 else jnp.logical_not(left), qb, 0.0).astype(BF16)
        s = lax.dot_general(qh, kps[t][blk // Q_PER_KV], nt, preferred_element_type=F32) * scale
        return s if kmask is None else jnp.where(kmask, s, NEG)

    def attend(t, h, s):
        sink = sink_ref[h]
        m = jnp.maximum(jnp.max(s, axis=-1, keepdims=True), sink)
        p = jnp.exp(s - m)
        den = jnp.sum(p, axis=-1, keepdims=True) + jnp.exp(sink - m)
        p = p * (1.0 / den)
        return jnp.dot(p.astype(BF16), vps[t][h // (2 * Q_PER_KV)], preferred_element_type=F32)

    work = [(t, h) for t in range(len(items)) for h in range(N_HEADS)]
    pending, done = {}, {}
    for n in range(len(work) + ATTN_LAG):
        if n < len(work):
            pending[work[n]] = scores(*work[n])
        if n >= ATTN_LAG:
            key = work[n - ATTN_LAG]
            done[key] = attend(*key, pending.pop(key))
    return [jnp.concatenate([jnp.where(left, done[(t, 2 * b)], done[(t, 2 * b + 1)])
                             for b in range(N_HEADS // 2)], axis=1) for t in range(len(items))]


def _attn_band_kernel(sink_ref, q_ref, k_ref, v_ref, mk_ref, mv_ref, o_ref):
    band = WINDOW + CHUNK
    items = []
    for sub in range(ATTN_CHUNKS):
        c = pl.program_id(1) * ATTN_CHUNKS + sub
        first = jnp.maximum(c - WINDOW // CHUNK, 0)
        start = pl.multiple_of(first * CHUNK, CHUNK)
        kk = jnp.concatenate([mk_ref[...], k_ref[pl.ds(start, band), :]], axis=0)
        vv = jnp.concatenate([mv_ref[...], v_ref[pl.ds(start, band), :]], axis=0)
        j = lax.broadcasted_iota(I32, (1, N_META + band), 1)
        key_chunk = first + jnp.maximum(j - N_META, 0) // CHUNK
        kmask = (j < N_META) | (key_chunk <= c)
        items.append((q_ref[sub * CHUNK:(sub + 1) * CHUNK, :], kk, vv, kmask))
    for sub, o in enumerate(_attn_core(items, sink_ref)):
        o_ref[sub * CHUNK:(sub + 1) * CHUNK, :] = o.astype(BF16)


def attn_band(qkv, meta_k, meta_v, sinks, *, n_batch, seq):
    rows = ATTN_CHUNKS * CHUNK
    n_chunks = seq // rows
    assert seq >= WINDOW + CHUNK and seq % rows == 0
    kcol = Q_DIM // KV_DIM
    grid_spec = pltpu.PrefetchScalarGridSpec(
        num_scalar_prefetch=1, grid=(n_batch, n_chunks),
        in_specs=[pl.BlockSpec((rows, Q_DIM), lambda b, c, s: (b * n_chunks + c, 0)),
                  pl.BlockSpec((seq, KV_DIM), lambda b, c, s: (b, kcol)),
                  pl.BlockSpec((seq, KV_DIM), lambda b, c, s: (b, kcol + 1)),
                  pl.BlockSpec((N_META, KV_DIM), lambda b, c, s: (0, 0)),
                  pl.BlockSpec((N_META, KV_DIM), lambda b, c, s: (0, 0))],
        out_specs=pl.BlockSpec((rows, Q_DIM), lambda b, c, s: (b * n_chunks + c, 0)))
    return pl.pallas_call(
        _attn_band_kernel, grid_spec=grid_spec,
        out_shape=jax.ShapeDtypeStruct((n_batch * seq, Q_DIM), BF16),
        compiler_params=_cparams(("parallel", "arbitrary")), name="attn_band")(
            sinks, qkv, qkv, qkv, meta_k, meta_v)


def _attn_full_kernel(sink_ref, q_ref, k_ref, v_ref, o_ref):
    o_ref[...] = _attn_core([(q_ref[...], k_ref[...], v_ref[...], None)], sink_ref)[0].astype(BF16)


def attn_full(q, kk, vv, sinks):
    n_batch, n_q, _ = q.shape
    n_k = kk.shape[1]
    grid_spec = pltpu.PrefetchScalarGridSpec(
        num_scalar_prefetch=1, grid=(n_batch,),
        in_specs=[pl.BlockSpec((None, n_q, Q_DIM), lambda b, s: (b, 0, 0)),
                  pl.BlockSpec((None, n_k, KV_DIM), lambda b, s: (b, 0, 0)),
                  pl.BlockSpec((None, n_k, KV_DIM), lambda b, s: (b, 0, 0))],
        out_specs=pl.BlockSpec((None, n_q, Q_DIM), lambda b, s: (b, 0, 0)))
    return pl.pallas_call(
        _attn_full_kernel, grid_spec=grid_spec,
        out_shape=jax.ShapeDtypeStruct((n_batch, n_q, Q_DIM), BF16),
        compiler_params=_cparams(("parallel",)), name="attn_full")(sinks, q, kk, vv)


def _expert_ffn(x16, w_in, w_out):
    gu = jnp.dot(x16, w_in, preferred_element_type=F32)
    a, b = gu[:, :D_EXPERT], gu[:, D_EXPERT:]
    hmid = a * _sigmoid(a) * b
    return jnp.dot(hmid.astype(BF16), w_out, preferred_element_type=F32)


def _moe_dense_kernel(x_ref, e_ref, p_ref, wi_ref, wo_ref, o_ref):
    e = pl.program_id(0)

    @pl.when(e == 0)
    def _():
        o_ref[...] = jnp.zeros_like(o_ref)

    eid, wts = e_ref[...], p_ref[...]
    sel0, sel1 = eid[:, 0:1] == e, eid[:, 1:2] == e
    coef = jnp.where(sel0, wts[:, 0:1], jnp.where(sel1, wts[:, 1:2], 0.0))
    y = _expert_ffn(x_ref[...].astype(BF16), wi_ref[0].astype(BF16), wo_ref[0].astype(BF16))
    o_ref[...] += jnp.where(sel0 | sel1, y * coef, 0.0)


def moe_dense(xn, eid, wts, w_in, w_out, *, layer):
    n_rows = xn.shape[0]
    full = lambda e: (0, 0)
    return pl.pallas_call(
        _moe_dense_kernel, grid=(N_EXPERTS,),
        in_specs=[pl.BlockSpec((n_rows, D_MODEL), full), pl.BlockSpec((n_rows, LANES), full),
                  pl.BlockSpec((n_rows, LANES), full),
                  pl.BlockSpec((None, 1, D_MODEL, 2 * D_EXPERT), lambda e: (layer, e, 0, 0)),
                  pl.BlockSpec((None, 1, D_EXPERT, D_MODEL), lambda e: (layer, e, 0, 0))],
        out_specs=pl.BlockSpec((n_rows, D_MODEL), full),
        out_shape=jax.ShapeDtypeStruct((n_rows, D_MODEL), F32),
        compiler_params=_cparams(("arbitrary",)), name="moe_dense")(xn, eid, wts, w_in, w_out)


def _moe_sorted_kernel(blk_e_ref, n_used_ref, src_ref, src_next_ref, dst_prev_ref, dst_ref, x_hbm, wi_ref,
                       wo_ref, y_hbm, xbuf, obuf, x16_ref, hmid_ref, wi16_ref, wo16_ref, gsem, ssem, *,
                       tm, n_tok):
    i = pl.program_id(0)
    n_used = n_used_ref[0]
    nc = MOE_NC

    def start_gathers(rows_ref, s, lo=0, hi=tm):
        for r in range(lo, hi):
            pltpu.make_async_copy(x_hbm.at[pl.ds(rows_ref[0, 0, r], 1), :], xbuf.at[s, pl.ds(r, 1), :],
                                  gsem.at[s]).start()

    def start_scatters(rows_ref, s, lo=0, hi=tm):
        for r in range(lo, hi):
            pltpu.make_async_copy(obuf.at[s, pl.ds(r, 1), :], y_hbm.at[pl.ds(rows_ref[0, 0, r], 1), :],
                                  ssem.at[s]).start()

    def wait_block(buf, sem, s):
        pltpu.make_async_copy(buf.at[s], buf.at[s], sem.at[s]).wait()

    @pl.when(i == 0)
    def _():
        obuf[...] = jnp.zeros_like(obuf)
        spare = [pltpu.make_async_copy(obuf.at[s], y_hbm.at[pl.ds(2 * n_tok + s * tm, tm), :], ssem.at[s])
                 for s in range(2)]
        for cp in spare:
            cp.start()
        for cp in spare:
            cp.wait()
        start_gathers(src_ref, 0)

    def step(slot):
        wait_block(xbuf, gsem, slot)

        @pl.when(i >= 1)
        def _():
            wait_block(obuf, ssem, slot)

        n_in, n_out = 2 * D_EXPERT // nc, D_MODEL // nc
        g_step, s_step = tm // n_in, tm // n_out
        x16_ref[...] = xbuf[slot].astype(BF16)
        for k in range(D_EXPERT // nc):
            a = jnp.dot(x16_ref[...], wi16_ref[:, k * nc:(k + 1) * nc], preferred_element_type=F32)
            start_gathers(src_next_ref, 1 - slot, 2 * k * g_step, (2 * k + 1) * g_step)
            b = jnp.dot(x16_ref[...], wi16_ref[:, D_EXPERT + k * nc:D_EXPERT + (k + 1) * nc],
                        preferred_element_type=F32)
            start_gathers(src_next_ref, 1 - slot, (2 * k + 1) * g_step, (2 * k + 2) * g_step)
            hmid_ref[:, k * nc:(k + 1) * nc] = (a * _sigmoid(a) * b).astype(BF16)
        for k in range(n_out):
            obuf[slot, :, k * nc:(k + 1) * nc] = jnp.dot(hmid_ref[...], wo16_ref[:, k * nc:(k + 1) * nc],
                                                         preferred_element_type=F32)
            start_scatters(dst_prev_ref, 1 - slot, k * s_step, (k + 1) * s_step)

        @pl.when(i == n_used - 1)
        def _():
            start_scatters(dst_ref, slot)
            wait_block(obuf, ssem, 1 - slot)
            wait_block(obuf, ssem, slot)
            wait_block(xbuf, gsem, 1 - slot)

    @pl.when((i < n_used) & ((i == 0) | (blk_e_ref[i] != blk_e_ref[jnp.maximum(i - 1, 0)])))
    def _():
        wi16_ref[...] = wi_ref[0].astype(BF16)
        wo16_ref[...] = wo_ref[0].astype(BF16)

    for parity in range(2):
        pl.when((i < n_used) & (i % 2 == parity))(functools.partial(step, parity))


def moe_sorted(xn, w_in, w_out, tables, *, tm, layer):
    blk_e, n_used, src, dst = tables
    n_tok = xn.shape[0]
    n_blocks = blk_e.shape[0]
    smem = lambda index_map: pl.BlockSpec((1, 1, tm), index_map, memory_space=pltpu.SMEM)
    nxt = lambda i, be, nu: (jnp.maximum(jnp.minimum(i + 1, nu[0] - 1), 0), 0, 0)
    grid_spec = pltpu.PrefetchScalarGridSpec(
        num_scalar_prefetch=2, grid=(n_blocks,),
        in_specs=[smem(lambda i, be, nu: (i, 0, 0)), smem(nxt),
                  smem(lambda i, be, nu: (i, 0, 0)), smem(lambda i, be, nu: (i + 1, 0, 0)),
                  pl.BlockSpec(memory_space=pl.ANY),
                  pl.BlockSpec((None, 1, D_MODEL, 2 * D_EXPERT), lambda i, be, nu: (layer, be[i], 0, 0)),
                  pl.BlockSpec((None, 1, D_EXPERT, D_MODEL), lambda i, be, nu: (layer, be[i], 0, 0))],
        out_specs=pl.BlockSpec(memory_space=pl.ANY),
        scratch_shapes=[pltpu.VMEM((2, tm, D_MODEL), F32), pltpu.VMEM((2, tm, D_MODEL), F32),
                        pltpu.VMEM((tm, D_MODEL), BF16), pltpu.VMEM((tm, D_EXPERT), BF16),
                        pltpu.VMEM((D_MODEL, 2 * D_EXPERT), BF16), pltpu.VMEM((D_EXPERT, D_MODEL), BF16),
                        pltpu.SemaphoreType.DMA((2,)), pltpu.SemaphoreType.DMA((2,))])
    return pl.pallas_call(
        functools.partial(_moe_sorted_kernel, tm=tm, n_tok=n_tok), grid_spec=grid_spec,
        out_shape=jax.ShapeDtypeStruct((2 * n_tok + 2 * tm, D_MODEL), F32),
        compiler_params=_cparams(("arbitrary",)), name="moe_sorted")(
            blk_e, n_used, src, src, dst, dst, xn, w_in, w_out)


def route_tables(eid, *, tm):
    n_tok = eid.shape[0]
    n_asg = 2 * n_tok
    flat = eid.reshape(-1)
    skey = jnp.sort(flat * n_asg + jnp.arange(n_asg, dtype=I32))
    order = skey % n_asg
    counts = jnp.sum((flat[:, None] == jnp.arange(N_EXPERTS, dtype=I32)[None, :]).astype(I32), axis=0)
    starts = jnp.cumsum(counts) - counts
    padded = (counts + tm - 1) // tm * tm
    pend = jnp.cumsum(padded)
    pstarts = pend - padded
    n_blocks = -(-(n_asg + N_EXPERTS * (tm - 1)) // tm)
    blk = jnp.arange(n_blocks, dtype=I32)
    blk_e = jnp.minimum(jnp.sum((pend[None, :] <= (blk * tm)[:, None]).astype(I32), axis=1), N_EXPERTS - 1)
    blk_off = blk * tm - pstarts[blk_e]
    blk_n = jnp.clip(counts[blk_e] - blk_off, 0, tm)
    r = jnp.arange(tm, dtype=I32)[None, :]
    valid = r < blk_n[:, None]
    code = order[jnp.clip((starts[blk_e] + blk_off)[:, None] + r, 0, n_asg - 1)]
    src = jnp.where(valid, code >> 1, 0)
    spare = 2 * n_tok + (blk % 2)[:, None] * tm + r
    dst = jnp.where(valid, (code & 1) * n_tok + (code >> 1), spare)
    dst = jnp.concatenate([2 * n_tok + tm + r, dst], axis=0)
    n_used = (pend[-1] // tm).reshape(1)
    return (blk_e.astype(I32), n_used.astype(I32), src.reshape(n_blocks, 1, tm).astype(I32),
            dst.reshape(n_blocks + 1, 1, tm).astype(I32))


def _q_head_order():
    order = []
    for pr in range(N_KV_HEADS // 2):
        for r in range(Q_PER_KV):
            for half in range(2):
                order.append((2 * pr + half) * Q_PER_KV + r)
    return jnp.asarray(order, dtype=I32)


def _pad_lanes(a):
    return jnp.pad(a, [(0, 0)] * (a.ndim - 1) + [(0, LANES - a.shape[-1])])


def _ssd_params(conv_w, conv_b, dt_bias, a_log, d_skip, norm_w):
    head_of_col = jnp.arange(GROUP_W, dtype=I32) // SSD_HEAD_DIM
    row = jnp.arange(32, dtype=I32)
    expand = ((row[:, None] % 8 == head_of_col[None, :]) & (row[:, None] < 24)).astype(BF16)
    return {
        "conv_w": jnp.pad(conv_w, ((0, 8 - SSD_CONV), (0, 0))),
        "conv_b": conv_b.reshape(1, CONV_DIM),
        "dt_bias": _pad_lanes(dt_bias.reshape(1, SSD_HEADS)),
        "a_log": _pad_lanes(a_log.reshape(1, SSD_HEADS)),
        "d_skip": jnp.repeat(d_skip, SSD_HEAD_DIM).reshape(1, D_INNER),
        "norm_w": norm_w.reshape(1, D_INNER),
        "expand": expand,
    }


def _router_params(w_group, b_group, w_router, b_router):
    wr = _pad_lanes(jnp.concatenate([w_group, w_router], axis=1)).astype(BF16)
    br = _pad_lanes(jnp.concatenate([b_group, b_router]).reshape(1, -1)).astype(F32)
    return wr, br


def _pick_tile(n, prefs):
    for t in prefs:
        if n % t == 0:
            return t
    return n


def kernel(x_prompt, x_sample, cache_ssm_state, cache_ssm_conv, cache_attn_k, cache_attn_v, cache_meta_k, cache_meta_v, meta_tokens, norm_mix, norm_ffn, norm_final, ssd_w_in, ssd_conv_w, ssd_conv_b, ssd_dt_bias, ssd_a_log, ssd_d, ssd_norm, ssd_w_out, attn_w_qkv, attn_sinks, attn_w_out, moe_w_group, moe_b_group, moe_w_router, moe_b_router, moe_w_in, moe_w_out):
    n_batch, seq, _ = x_prompt.shape
    dec_batch, dec_seq, _ = x_sample.shape
    n_real = n_batch * seq
    n_small = N_META + dec_batch * dec_seq
    blk = SSD_BLOCK
    assert seq % blk == 0 and dec_seq <= blk and N_META <= blk
    tm_big = _pick_tile(n_real, (1024, 512, 256))
    tm_mid = _pick_tile(n_real, (512, 256))
    tm_res = _pick_tile(n_real, (256,))

    h_real = x_prompt.reshape(n_real, D_MODEL)
    h_small = jnp.concatenate([meta_tokens, x_sample.reshape(-1, D_MODEL)], axis=0)

    w_in16 = ssd_w_in[0].astype(BF16)
    w_zx, w_dt = w_in16[:, :ZX_DIM], _pad_lanes(w_in16[:, ZX_DIM:])
    sp = _ssd_params(ssd_conv_w[0], ssd_conv_b[0], ssd_dt_bias[0], ssd_a_log[0], ssd_d[0], ssd_norm[0])
    w_out16 = ssd_w_out[0].astype(BF16)
    wr0, br0 = _router_params(moe_w_group[0], moe_b_group[0], moe_w_router[0], moe_b_router[0])

    zx_s, dt_s = norm_matmul(h_small, norm_mix[0], w_zx, tm=n_small, tn=512, w_extra=w_dt)
    zx_r, dt_r = norm_matmul(h_real, norm_mix[0], w_zx, tm=tm_big, tn=1024, w_extra=w_dt)

    n_sb = 1 + dec_batch

    def to_blocks(a, n_valid_meta=N_META):
        meta = jnp.pad(a[:N_META], ((0, blk - N_META), (0, 0)))[None]
        samp = jnp.pad(a[N_META:].reshape(dec_batch, dec_seq, -1), ((0, 0), (0, blk - dec_seq), (0, 0)))
        return jnp.concatenate([meta, samp], axis=0).reshape(n_sb * blk, -1)

    assert dec_seq == N_META, "small-path blocks share one valid length"
    s0_small = jnp.concatenate([jnp.zeros((1, D_INNER, SSD_STATE), F32),
                                cache_ssm_state[0].reshape(dec_batch, D_INNER, SSD_STATE)], axis=0)
    c0_small = jnp.pad(jnp.concatenate([jnp.zeros((1, SSD_CONV - 1, CONV_DIM), F32), cache_ssm_conv[0]], axis=0),
                       ((0, 0), (8 - (SSD_CONV - 1), 0), (0, 0)))
    yn_sb, state_small = ssd_scan(to_blocks(zx_s), to_blocks(dt_s), sp, s0_small, c0_small,
                                  n_batch=n_sb, n_chunks=1, valid_len=dec_seq, shared_init=False)
    yn_sb = yn_sb.reshape(n_sb, blk, D_INNER)
    yn_small = jnp.concatenate([yn_sb[0, :N_META], yn_sb[1:, :dec_seq].reshape(-1, D_INNER)], axis=0)

    c0_real = jnp.pad(zx_s[N_META - (SSD_CONV - 1):N_META, D_INNER:], ((8 - (SSD_CONV - 1), 0), (0, 0)))[None]
    yn_real, state_real = ssd_scan(zx_r, dt_r, sp, state_small[0:1], c0_real,
                                   n_batch=n_batch, n_chunks=seq // blk, valid_len=blk, shared_init=True)

    tk0 = 2048
    h1_s, xn_s, eid_s, wts_s = matmul_route(yn_small, w_out16, h_small, norm_ffn[0], wr0, br0, tm=n_small, tk=tk0)
    h1_r, xn_r, eid_r, wts_r = matmul_route(yn_real, w_out16, h_real, norm_ffn[0], wr0, br0, tm=tm_res,
                                            tk=D_INNER)

    moe_s = moe_dense(xn_s, eid_s, wts_s, moe_w_in, moe_w_out, layer=0)
    moe_r = moe_sorted(xn_r, moe_w_in, moe_w_out, route_tables(eid_r[:, :2], tm=MOE_TM), tm=MOE_TM, layer=0)

    head_order = _q_head_order()
    wq = attn_w_qkv[0][:, :Q_DIM].reshape(D_MODEL, N_HEADS, ATTN_HEAD_DIM)[:, head_order].reshape(D_MODEL, Q_DIM)
    w_qkv16 = jnp.concatenate([wq, attn_w_qkv[0][:, Q_DIM:]], axis=1).astype(BF16)
    sinks = attn_sinks[0][head_order].astype(F32)
    w_ao16 = attn_w_out[0].reshape(N_HEADS, ATTN_HEAD_DIM, D_MODEL)[head_order].reshape(Q_DIM, D_MODEL).astype(BF16)
    wr1, br1 = _router_params(moe_w_group[1], moe_b_group[1], moe_w_router[1], moe_b_router[1])

    qkv_s, h2_s = norm_matmul(h1_s, norm_mix[1], w_qkv16, tm=n_small, tn=512, mode="sum", moe=(moe_s,),
                              write_h=True)
    qkv_r, h2_r = norm_matmul(h1_r, norm_mix[1], w_qkv16, tm=tm_res, tn=Q_DIM + 2 * KV_DIM, mode="pair",
                              moe=(moe_r, wts_r), write_h=True)

    k_meta, v_meta = qkv_s[:N_META, Q_DIM:Q_DIM + KV_DIM], qkv_s[:N_META, Q_DIM + KV_DIM:]
    k_new = qkv_s[N_META:, Q_DIM:Q_DIM + KV_DIM].reshape(dec_batch, dec_seq, KV_DIM)
    v_new = qkv_s[N_META:, Q_DIM + KV_DIM:].reshape(dec_batch, dec_seq, KV_DIM)
    o_meta = attn_full(qkv_s[None, :N_META, :Q_DIM], k_meta[None], v_meta[None], sinks)
    kk_s = jnp.concatenate([cache_meta_k[0].reshape(dec_batch, N_META, KV_DIM),
                            cache_attn_k[0].reshape(dec_batch, WINDOW, KV_DIM), k_new], axis=1)
    vv_s = jnp.concatenate([cache_meta_v[0].reshape(dec_batch, N_META, KV_DIM),
                            cache_attn_v[0].reshape(dec_batch, WINDOW, KV_DIM), v_new], axis=1)
    o_samp = attn_full(qkv_s[N_META:, :Q_DIM].reshape(dec_batch, dec_seq, Q_DIM), kk_s, vv_s, sinks)
    o_small = jnp.concatenate([o_meta.reshape(N_META, Q_DIM), o_samp.reshape(-1, Q_DIM)], axis=0)
    o_real = attn_band(qkv_r, k_meta, v_meta, sinks, n_batch=n_batch, seq=seq)

    h3_s, xn_s, eid_s, wts_s = matmul_route(o_small, w_ao16, h2_s, norm_ffn[1], wr1, br1, tm=n_small, tk=1024)
    h3_r, xn_r, eid_r, wts_r = matmul_route(o_real, w_ao16, h2_r, norm_ffn[1], wr1, br1, tm=tm_mid, tk=2048)

    moe_s = moe_dense(xn_s, eid_s, wts_s, moe_w_in, moe_w_out, layer=1)
    moe_r = moe_sorted(xn_r, moe_w_in, moe_w_out, route_tables(eid_r[:, :2], tm=MOE_TM), tm=MOE_TM, layer=1)

    y_small = final_norm(h3_s, norm_final, tm=n_small, mode="sum", moe=(moe_s,))
    y_real = final_norm(h3_r, norm_final, tm=tm_mid, mode="pair", moe=(moe_r, wts_r))

    kv_shape = (N_KV_HEADS, ATTN_HEAD_DIM)
    qkv_r3 = qkv_r.reshape(n_batch, seq, -1)
    zx_r3 = zx_r.reshape(n_batch, seq, ZX_DIM)
    zx_samp = zx_s[N_META:].reshape(dec_batch, dec_seq, ZX_DIM)
    return (
        y_real.reshape(n_batch, seq, D_MODEL),
        y_small[N_META:].reshape(dec_batch, dec_seq, D_MODEL),
        state_real.reshape(1, n_batch, SSD_HEADS, SSD_HEAD_DIM, SSD_STATE),
        zx_r3[None, :, seq - (SSD_CONV - 1):, D_INNER:],
        qkv_r3[:, seq - WINDOW:, Q_DIM:Q_DIM + KV_DIM].reshape((1, n_batch, WINDOW) + kv_shape),
        qkv_r3[:, seq - WINDOW:, Q_DIM + KV_DIM:].reshape((1, n_batch, WINDOW) + kv_shape),
        jnp.broadcast_to(k_meta.reshape((1, 1, N_META) + kv_shape), (1, n_batch, N_META) + kv_shape),
        jnp.broadcast_to(v_meta.reshape((1, 1, N_META) + kv_shape), (1, n_batch, N_META) + kv_shape),
        state_small[1:].reshape(1, dec_batch, SSD_HEADS, SSD_HEAD_DIM, SSD_STATE),
        zx_samp[None, :, dec_seq - (SSD_CONV - 1):, D_INNER:],
        k_new.reshape((1, dec_batch, dec_seq) + kv_shape),
        v_new.reshape((1, dec_batch, dec_seq) + kv_shape),
    )
```

```python
import functools

import jax
import jax.numpy as jnp
from jax import lax
from jax.experimental import pallas as pl
from jax.experimental.pallas import tpu as pltpu

F32, BF16, I32 = jnp.float32, jnp.bfloat16, jnp.int32

D_MODEL = 2048
N_META = 16
EPS = 1e-5
D_INNER = 4096
SSD_HEAD_DIM = 64
SSD_HEADS = 64
SSD_GROUPS = 8
SSD_STATE = 128
SSD_CONV = 4
GN = SSD_GROUPS * SSD_STATE
CONV_DIM = D_INNER + 2 * GN
ZX_DIM = D_INNER + CONV_DIM
GROUP_W = D_INNER // SSD_GROUPS
WINDOW = 128
CHUNK = 64
ATTN_HEAD_DIM = 64
N_HEADS = 32
N_KV_HEADS = 8
Q_PER_KV = N_HEADS // N_KV_HEADS
Q_DIM = N_HEADS * ATTN_HEAD_DIM
KV_DIM = N_KV_HEADS * ATTN_HEAD_DIM
N_GROUPS = 4
EXPERTS_PER_GROUP = 8
N_EXPERTS = 32
D_EXPERT = 512

LANES = 128
SSD_BLOCK = 128
NEG = -1e30
VMEM_LIMIT = 56 * 1024 * 1024
MOE_TM = 256
MOE_NC = 256
ATTN_CHUNKS = 4
ATTN_LAG = ATTN_CHUNKS * N_HEADS


def _cparams(sem):
    return pltpu.CompilerParams(dimension_semantics=sem, vmem_limit_bytes=VMEM_LIMIT)


def _resident_spec(block_shape, index_map, *, resident):
    if resident:
        return pl.BlockSpec(block_shape, index_map, pipeline_mode=pl.Buffered(1))
    return pl.BlockSpec(block_shape, index_map)


def _sigmoid(x):
    return 1.0 / (1.0 + jnp.exp(-x))


def _rms(h, g):
    var = jnp.mean(h * h, axis=-1, keepdims=True)
    return h * lax.rsqrt(var + EPS) * g


def _row_chunk(tm):
    for rc in (256, 128, 64, 32, 16, 8):
        if tm % rc == 0:
            return rc
    raise ValueError(tm)


_N_MOE_REFS = {"none": 0, "sum": 1, "pair": 3}


def _combine_rows(x_ref, moe_refs, mode, rows):
    h = x_ref[rows, :]
    if mode == "sum":
        h = h + moe_refs[0][rows, :]
    elif mode == "pair":
        y0, y1, w = moe_refs
        wv = w[rows, :]
        h = h + (y0[rows, :] * wv[:, 0:1] + y1[rows, :] * wv[:, 1:2])
    return h


def _norm_mm_kernel(*refs, mode, write_h, has_extra, tm):
    it = iter(refs)
    x_ref = next(it)
    moe_refs = [next(it) for _ in range(_N_MOE_REFS[mode])]
    g_ref, w_ref = next(it), next(it)
    we_ref = next(it) if has_extra else None
    o_ref = next(it)
    h_ref = next(it) if write_h else None
    e_ref = next(it) if has_extra else None
    xn_ref = next(it)
    rc = _row_chunk(tm)

    @pl.when(pl.program_id(1) == 0)
    def _():
        def body(r, carry):
            rows = pl.ds(pl.multiple_of(r * rc, rc), rc)
            h = _combine_rows(x_ref, moe_refs, mode, rows)
            if write_h:
                h_ref[rows, :] = h
            xn_ref[rows, :] = _rms(h, g_ref[...]).astype(BF16)
            return carry

        lax.fori_loop(0, tm // rc, body, 0)
        if has_extra:
            e_ref[...] = jnp.dot(xn_ref[...], we_ref[...], preferred_element_type=F32)

    o_ref[...] = jnp.dot(xn_ref[...], w_ref[...], preferred_element_type=F32)


def _moe_in_specs(mode, moe, tm, n_rows):
    if mode == "none":
        return [], []
    if mode == "sum":
        return [pl.BlockSpec((tm, D_MODEL), lambda i, j: (i, 0))], [moe[0]]
    y2, wts = moe
    off = n_rows // tm
    specs = [pl.BlockSpec((tm, D_MODEL), lambda i, j: (i, 0)),
             pl.BlockSpec((tm, D_MODEL), lambda i, j: (i + off, 0)),
             pl.BlockSpec((tm, LANES), lambda i, j: (i, 0))]
    return specs, [y2, y2, wts]


def norm_matmul(x, g, w, *, tm, tn, mode="none", moe=(), w_extra=None, write_h=False):
    n_rows, n_out = x.shape[0], w.shape[1]
    assert n_rows % tm == 0 and n_out % tn == 0
    moe_specs, moe_ops = _moe_in_specs(mode, moe, tm, n_rows)
    in_specs = [pl.BlockSpec((tm, D_MODEL), lambda i, j: (i, 0))] + moe_specs + [
        pl.BlockSpec((1, D_MODEL), lambda i, j: (0, 0)),
        _resident_spec((D_MODEL, tn), lambda i, j: (0, j), resident=n_out == tn)]
    operands = [x] + moe_ops + [g.reshape(1, D_MODEL), w]
    out_shape = [jax.ShapeDtypeStruct((n_rows, n_out), F32)]
    out_specs = [pl.BlockSpec((tm, tn), lambda i, j: (i, j))]
    if w_extra is not None:
        in_specs.append(pl.BlockSpec((D_MODEL, LANES), lambda i, j: (0, 0)))
        operands.append(w_extra)
    if write_h:
        out_shape.append(jax.ShapeDtypeStruct((n_rows, D_MODEL), F32))
        out_specs.append(pl.BlockSpec((tm, D_MODEL), lambda i, j: (i, 0)))
    if w_extra is not None:
        out_shape.append(jax.ShapeDtypeStruct((n_rows, LANES), F32))
        out_specs.append(pl.BlockSpec((tm, LANES), lambda i, j: (i, 0)))
    kern = functools.partial(_norm_mm_kernel, mode=mode, write_h=write_h,
                             has_extra=w_extra is not None, tm=tm)
    return pl.pallas_call(
        kern, grid=(n_rows // tm, n_out // tn), in_specs=in_specs, out_specs=out_specs,
        out_shape=out_shape, scratch_shapes=[pltpu.VMEM((tm, D_MODEL), BF16)],
        compiler_params=_cparams(("parallel", "arbitrary")), name="norm_matmul")(*operands)


def _final_norm_kernel(*refs, mode, tm):
    x_ref = refs[0]
    moe_refs = refs[1:1 + _N_MOE_REFS[mode]]
    g_ref, o_ref = refs[-2], refs[-1]
    rc = _row_chunk(tm)

    def body(r, carry):
        rows = pl.ds(pl.multiple_of(r * rc, rc), rc)
        o_ref[rows, :] = _rms(_combine_rows(x_ref, moe_refs, mode, rows), g_ref[...])
        return carry

    lax.fori_loop(0, tm // rc, body, 0)


def final_norm(x, g, *, tm, mode, moe):
    n_rows = x.shape[0]
    moe_specs, moe_ops = _moe_in_specs(mode, moe, tm, n_rows)
    in_specs = [pl.BlockSpec((tm, D_MODEL), lambda i, j: (i, 0))] + moe_specs + [
        pl.BlockSpec((1, D_MODEL), lambda i, j: (0, 0))]
    return pl.pallas_call(
        functools.partial(_final_norm_kernel, mode=mode, tm=tm), grid=(n_rows // tm, 1),
        in_specs=in_specs, out_specs=pl.BlockSpec((tm, D_MODEL), lambda i, j: (i, 0)),
        out_shape=jax.ShapeDtypeStruct((n_rows, D_MODEL), F32),
        compiler_params=_cparams(("parallel", "arbitrary")), name="final_norm")(
            x, *moe_ops, g.reshape(1, D_MODEL))


def _route(lg):
    lane = lax.broadcasted_iota(I32, lg.shape, 1)
    is_g = lane < N_GROUPS
    gl = jnp.where(is_g, lg, NEG)
    gmax = jnp.max(gl, axis=-1, keepdims=True)
    gsum = jnp.sum(jnp.where(is_g, jnp.exp(gl - gmax), 0.0), axis=-1, keepdims=True)
    gate = 1.0 / gsum
    gidx = jnp.min(jnp.where(gl == gmax, lane, LANES), axis=-1, keepdims=True)
    ecol = lane - N_GROUPS
    in_group = (ecol >= 0) & (ecol < N_EXPERTS) & ((ecol >> 3) == gidx)
    el = jnp.where(in_group, lg, NEG)
    v1 = jnp.max(el, axis=-1, keepdims=True)
    i1 = jnp.min(jnp.where(el == v1, lane, LANES), axis=-1, keepdims=True)
    el2 = jnp.where(lane == i1, NEG, el)
    v2 = jnp.max(el2, axis=-1, keepdims=True)
    i2 = jnp.min(jnp.where(el2 == v2, lane, LANES), axis=-1, keepdims=True)
    e2 = jnp.exp(v2 - v1)
    w1 = gate / (1.0 + e2)
    w2 = gate * e2 / (1.0 + e2)
    eid = jnp.where(lane == 0, i1 - N_GROUPS, jnp.where(lane == 1, i2 - N_GROUPS, 0))
    wts = jnp.where(lane == 0, w1, jnp.where(lane == 1, w2, 0.0))
    return eid, wts


def _mm_route_kernel(a_ref, w_ref, h_ref, g_ref, wr_ref, br_ref,
                     h1_ref, xn_ref, e_ref, p_ref, acc_ref, *, nk, tm):
    k = pl.program_id(1)
    rc = _row_chunk(tm)

    @pl.when(k == 0)
    def _():
        acc_ref[...] = jnp.zeros_like(acc_ref)

    acc_ref[...] += jnp.dot(a_ref[...], w_ref[...], preferred_element_type=F32)

    @pl.when(k == nk - 1)
    def _():
        def body(r, carry):
            rows = pl.ds(pl.multiple_of(r * rc, rc), rc)
            h1 = h_ref[rows, :] + acc_ref[rows, :]
            h1_ref[rows, :] = h1
            xn = _rms(h1, g_ref[...])
            xn_ref[rows, :] = xn
            lg = jnp.dot(xn.astype(BF16), wr_ref[...], preferred_element_type=F32) + br_ref[...]
            eid, wts = _route(lg)
            e_ref[rows, :] = eid
            p_ref[rows, :] = wts
            return carry

        lax.fori_loop(0, tm // rc, body, 0)


def matmul_route(a, w, h, g, wr, br, *, tm, tk):
    n_rows, kdim = a.shape
    assert n_rows % tm == 0 and kdim % tk == 0
    nk = kdim // tk
    row_spec = pl.BlockSpec((tm, D_MODEL), lambda i, k: (i, 0))
    lane_spec = pl.BlockSpec((tm, LANES), lambda i, k: (i, 0))
    return pl.pallas_call(
        functools.partial(_mm_route_kernel, nk=nk, tm=tm), grid=(n_rows // tm, nk),
        in_specs=[pl.BlockSpec((tm, tk), lambda i, k: (i, k)),
                  _resident_spec((tk, D_MODEL), lambda i, k: (k, 0), resident=nk == 1),
                  row_spec,
                  pl.BlockSpec((1, D_MODEL), lambda i, k: (0, 0)),
                  pl.BlockSpec((D_MODEL, LANES), lambda i, k: (0, 0)),
                  pl.BlockSpec((1, LANES), lambda i, k: (0, 0))],
        out_specs=[row_spec, row_spec, lane_spec, lane_spec],
        out_shape=[jax.ShapeDtypeStruct((n_rows, D_MODEL), F32),
                   jax.ShapeDtypeStruct((n_rows, D_MODEL), F32),
                   jax.ShapeDtypeStruct((n_rows, LANES), I32),
                   jax.ShapeDtypeStruct((n_rows, LANES), F32)],
        scratch_shapes=[pltpu.VMEM((tm, D_MODEL), F32)],
        compiler_params=_cparams(("parallel", "arbitrary")), name="matmul_route")(
            a, w, h, g.reshape(1, D_MODEL), wr, br)


def _expand_heads(v, eg):
    v1 = v.astype(BF16).astype(F32)
    r1 = v - v1
    v2 = r1.astype(BF16).astype(F32)
    v3 = (r1 - v2).astype(BF16).astype(F32)
    parts = jnp.concatenate([v1, v2, v3, jnp.zeros_like(v)], axis=0).astype(BF16)
    return lax.dot_general(parts, eg, (((0,), (0,)), ((), ())), preferred_element_type=F32)


def _ssd_kernel(z_ref, x_ref, b_ref, c_ref, dt_ref, cw_ref, cb_ref, dtb_ref, alog_ref, dsk_ref,
                nw_ref, eg_ref, s0_ref, c0_ref, y_ref, so_ref,
                state_ref, xp_ref, cumt_ref, dtt_ref, *, valid_len, n_chunks):
    blk = SSD_BLOCK
    c = pl.program_id(1)

    @pl.when(c == 0)
    def _():
        state_ref[...] = s0_ref[0].T
        xp_ref[0:8, :] = c0_ref[0]

    @pl.when(c > 0)
    def _():
        xp_ref[0:8, :] = xp_ref[blk:blk + 8, :]

    xp_ref[8:8 + blk, 0:D_INNER] = x_ref[...]
    xp_ref[8:8 + blk, D_INNER:D_INNER + GN] = b_ref[...]
    xp_ref[8:8 + blk, D_INNER + GN:CONV_DIM] = c_ref[...]

    row = lax.broadcasted_iota(I32, (blk, LANES), 0)
    lane = lax.broadcasted_iota(I32, (blk, LANES), 1)
    dtr = dt_ref[...] + dtb_ref[...]
    dt = jnp.maximum(dtr, 0.0) + jnp.log1p(jnp.exp(-jnp.abs(dtr)))
    if valid_len < blk:
        dt = jnp.where(row < valid_len, dt, 0.0)
    cum = dt * (-jnp.exp(alog_ref[...]))
    sh = 1
    while sh < blk:
        cum = cum + jnp.where(row >= sh, pltpu.roll(cum, sh, 0), 0.0)
        sh *= 2
    cumt_ref[...] = cum.T
    dtt_ref[...] = dt.T

    causal = lane <= row
    left = lane < SSD_HEAD_DIM

    def conv(cols):
        acc = cb_ref[:, cols]
        for k in range(SSD_CONV):
            acc = acc + xp_ref[pl.ds(8 - (SSD_CONV - 1) + k, blk), cols] * cw_ref[k:k + 1, cols]
        return acc * _sigmoid(acc)

    def group(g, carry):
        cols = pl.ds(pl.multiple_of(g * GROUP_W, GROUP_W), GROUP_W)
        xg = conv(cols)
        bg = conv(pl.ds(pl.multiple_of(D_INNER + g * SSD_STATE, SSD_STATE), SSD_STATE))
        cg = conv(pl.ds(pl.multiple_of(D_INNER + GN + g * SSD_STATE, SSD_STATE), SSD_STATE))
        cb16, bb16 = cg.astype(BF16), bg.astype(BF16)
        cbm = lax.dot_general(cb16, bb16, (((1,), (1,)), ((), ())), preferred_element_type=F32)
        sg = state_ref[:, cols]
        heads = pl.ds(pl.multiple_of(g * (GROUP_W // SSD_HEAD_DIM), 8), GROUP_W // SSD_HEAD_DIM)
        ecg = _expand_heads(cumt_ref[heads, :], eg_ref[...])
        edg = _expand_heads(dtt_ref[heads, :], eg_ref[...])
        y_off =jnp.dot(cb16, sg.astype(BF16), preferred_element_type=F32) * jnp.exp(ecg)
        parts = []
        for pr in range(GROUP_W // LANES):
            xpair = xg[:, pr * LANES:(pr + 1) * LANES]
            ec = ecg[:, pr * LANES:(pr + 1) * LANES]
            ec_sw = pltpu.roll(ec, SSD_HEAD_DIM, 1)
            acc = jnp.zeros((blk, LANES), F32)
            for q in range(2):
                head = g * (GROUP_W // SSD_HEAD_DIM) + pr * 2 + q
                cum_l = jnp.where(left, ec, ec_sw) if q == 0 else jnp.where(left, ec_sw, ec)
                cum_s = cumt_ref[pl.ds(head, 1), :]
                decay = jnp.exp(jnp.where(causal, cum_l - cum_s, NEG))
                wmat = (decay * cbm * dtt_ref[pl.ds(head, 1), :]).astype(BF16)
                xm = jnp.where(left if q == 0 else jnp.logical_not(left), xpair, 0.0).astype(BF16)
                acc = acc + jnp.dot(wmat, xm, preferred_element_type=F32)
            parts.append(acc)
        y = jnp.concatenate(parts, axis=1) + y_off
        y = y + xg * dsk_ref[:, cols]
        zg = z_ref[:, cols]
        y = y * (zg * _sigmoid(zg))
        y = y * lax.rsqrt(jnp.mean(y * y, axis=-1, keepdims=True) + EPS) * nw_ref[:, cols]
        y_ref[:, cols] = y.astype(BF16)
        cl = ecg[blk - 1:blk, :]
        xt = (xg * (jnp.exp(cl - ecg) * edg)).astype(BF16)
        h_in = lax.dot_general(bb16, xt, (((0,), (0,)), ((), ())), preferred_element_type=F32)
        state_ref[:, cols] = sg * jnp.exp(cl) + h_in
        return carry

    lax.fori_loop(0, SSD_GROUPS, group, 0, unroll=2)

    @pl.when(c == n_chunks - 1)
    def _():
        so_ref[0] = state_ref[...].T


def ssd_scan(zx, dt, p, s0, c0, *, n_batch, n_chunks, valid_len, shared_init):
    blk = SSD_BLOCK
    n_rows = n_batch * n_chunks * blk
    assert zx.shape == (n_rows, ZX_DIM)
    rowi = lambda b, c: b * n_chunks + c
    init = (lambda b, c: (0, 0, 0)) if shared_init else (lambda b, c: (b, 0, 0))
    const = lambda b, c: (0, 0)
    in_specs = [
        pl.BlockSpec((blk, D_INNER), lambda b, c: (rowi(b, c), 0)),
        pl.BlockSpec((blk, D_INNER), lambda b, c: (rowi(b, c), 1)),
        pl.BlockSpec((blk, GN), lambda b, c: (rowi(b, c), 2 * D_INNER // GN)),
        pl.BlockSpec((blk, GN), lambda b, c: (rowi(b, c), 2 * D_INNER // GN + 1)),
        pl.BlockSpec((blk, LANES), lambda b, c: (rowi(b, c), 0)),
        pl.BlockSpec((8, CONV_DIM), const),
        pl.BlockSpec((1, CONV_DIM), const),
        pl.BlockSpec((1, LANES), const),
        pl.BlockSpec((1, LANES), const),
        pl.BlockSpec((1, D_INNER), const),
        pl.BlockSpec((1, D_INNER), const),
        pl.BlockSpec((32, GROUP_W), const),
        pl.BlockSpec((1, D_INNER, SSD_STATE), init),
        pl.BlockSpec((1, 8, CONV_DIM), init),
    ]
    out_specs = [pl.BlockSpec((blk, D_INNER), lambda b, c: (rowi(b, c), 0)),
                 pl.BlockSpec((1, D_INNER, SSD_STATE), lambda b, c: (b, 0, 0))]
    out_shape = [jax.ShapeDtypeStruct((n_rows, D_INNER), BF16),
                 jax.ShapeDtypeStruct((n_batch, D_INNER, SSD_STATE), F32)]
    scratch = [pltpu.VMEM((SSD_STATE, D_INNER), F32),
               pltpu.VMEM((blk + 8, CONV_DIM), F32),
               pltpu.VMEM((LANES, blk), F32),
               pltpu.VMEM((LANES, blk), F32)]
    kern = functools.partial(_ssd_kernel, valid_len=valid_len, n_chunks=n_chunks)
    return pl.pallas_call(
        kern, grid=(n_batch, n_chunks), in_specs=in_specs, out_specs=out_specs,
        out_shape=out_shape, scratch_shapes=scratch,
        compiler_params=_cparams(("parallel", "arbitrary")), name="ssd_scan")(
            zx, zx, zx, zx, dt, p["conv_w"], p["conv_b"], p["dt_bias"], p["a_log"], p["d_skip"],
            p["norm_w"], p["expand"], s0, c0)


def _attn_core(items, sink_ref):
    n_q = items[0][0].shape[0]
    scale = ATTN_HEAD_DIM ** -0.5
    left = lax.broadcasted_iota(I32, (n_q, LANES), 1) < ATTN_HEAD_DIM
    nt = (((1,), (1,)), ((), ()))
    n_pairs = KV_DIM // LANES
    kps = [[kk[:, pr * LANES:(pr + 1) * LANES].astype(BF16) for pr in range(n_pairs)] for _, kk, _, _ in items]
    vps = [[vv[:, pr * LANES:(pr + 1) * LANES].astype(BF16) for pr in range(n_pairs)] for _, _, vv, _ in items]

    def scores(t, h):
        q, kmask = items[t][0], items[t][3]
        blk, half = divmod(h, 2)
        qb = q[:, blk * LANES:(blk + 1) * LANES]
        qh = jnp.where(left if half == 0 else jnp.logical_not(left), qb, 0.0).astype(BF16)
        s = lax.dot_general(qh, kps[t][blk // Q_PER_KV], nt, preferred_element_type=F32) * scale
        return s if kmask is None else jnp.where(kmask, s, NEG)

    def attend(t, h, s):
        sink = sink_ref[h]
        m = jnp.maximum(jnp.max(s, axis=-1, keepdims=True), sink)
        p = jnp.exp(s - m)
        den = jnp.sum(p, axis=-1, keepdims=True) + jnp.exp(sink - m)
        p = p * (1.0 / den)
        return jnp.dot(p.astype(BF16), vps[t][h // (2 * Q_PER_KV)], preferred_element_type=F32)

    work = [(t, h) for t in range(len(items)) for h in range(N_HEADS)]
    pending, done = {}, {}
    for n in range(len(work) + ATTN_LAG):
        if n < len(work):
            pending[work[n]] = scores(*work[n])
        if n >= ATTN_LAG:
            key = work[n - ATTN_LAG]
            done[key] = attend(*key, pending.pop(key))
    return [jnp.concatenate([jnp.where(left, done[(t, 2 * b)], done[(t, 2 * b + 1)])
                             for b in range(N_HEADS // 2)], axis=1) for t in range(len(items))]


def _attn_band_kernel(sink_ref, q_ref, k_ref, v_ref, mk_ref, mv_ref, o_ref):
    band = WINDOW + CHUNK
    items = []
    for sub in range(ATTN_CHUNKS):
        c = pl.program_id(1) * ATTN_CHUNKS + sub
        first = jnp.maximum(c - WINDOW // CHUNK, 0)
        start = pl.multiple_of(first * CHUNK, CHUNK)
        kk = jnp.concatenate([mk_ref[...], k_ref[pl.ds(start, band), :]], axis=0)
        vv = jnp.concatenate([mv_ref[...], v_ref[pl.ds(start, band), :]], axis=0)
        j = lax.broadcasted_iota(I32, (1, N_META + band), 1)
        key_chunk = first + jnp.maximum(j - N_META, 0) // CHUNK
        kmask = (j < N_META) | (key_chunk <= c)
        items.append((q_ref[sub * CHUNK:(sub + 1) * CHUNK, :], kk, vv, kmask))
    for sub, o in enumerate(_attn_core(items, sink_ref)):
        o_ref[sub * CHUNK:(sub + 1) * CHUNK, :] = o.astype(BF16)


def attn_band(qkv, meta_k, meta_v, sinks, *, n_batch, seq):
    rows = ATTN_CHUNKS * CHUNK
    n_chunks = seq // rows
    assert seq >= WINDOW + CHUNK and seq % rows == 0
    kcol = Q_DIM // KV_DIM
    grid_spec = pltpu.PrefetchScalarGridSpec(
        num_scalar_prefetch=1, grid=(n_batch, n_chunks),
        in_specs=[pl.BlockSpec((rows, Q_DIM), lambda b, c, s: (b * n_chunks + c, 0)),
                  pl.BlockSpec((seq, KV_DIM), lambda b, c, s: (b, kcol)),
                  pl.BlockSpec((seq, KV_DIM), lambda b, c, s: (b, kcol + 1)),
                  pl.BlockSpec((N_META, KV_DIM), lambda b, c, s: (0, 0)),
                  pl.BlockSpec((N_META, KV_DIM), lambda b, c, s: (0, 0))],
        out_specs=pl.BlockSpec((rows, Q_DIM), lambda b, c, s: (b * n_chunks + c, 0)))
    return pl.pallas_call(
        _attn_band_kernel, grid_spec=grid_spec,
        out_shape=jax.ShapeDtypeStruct((n_batch * seq, Q_DIM), BF16),
        compiler_params=_cparams(("parallel", "arbitrary")), name="attn_band")(
            sinks, qkv, qkv, qkv, meta_k, meta_v)


def _attn_full_kernel(sink_ref, q_ref, k_ref, v_ref, o_ref):
    o_ref[...] = _attn_core([(q_ref[...], k_ref[...], v_ref[...], None)], sink_ref)[0].astype(BF16)


def attn_full(q, kk, vv, sinks):
    n_batch, n_q, _ = q.shape
    n_k = kk.shape[1]
    grid_spec = pltpu.PrefetchScalarGridSpec(
        num_scalar_prefetch=1, grid=(n_batch,),
        in_specs=[pl.BlockSpec((None, n_q, Q_DIM), lambda b, s: (b, 0, 0)),
                  pl.BlockSpec((None, n_k, KV_DIM), lambda b, s: (b, 0, 0)),
                  pl.BlockSpec((None, n_k, KV_DIM), lambda b, s: (b, 0, 0))],
        out_specs=pl.BlockSpec((None, n_q, Q_DIM), lambda b, s: (b, 0, 0)))
    return pl.pallas_call(
        _attn_full_kernel, grid_spec=grid_spec,
        out_shape=jax.ShapeDtypeStruct((n_batch, n_q, Q_DIM), BF16),
        compiler_params=_cparams(("parallel",)), name="attn_full")(sinks, q, kk, vv)


def _expert_ffn(x16, w_in, w_out):
    gu = jnp.dot(x16, w_in, preferred_element_type=F32)
    a, b = gu[:, :D_EXPERT], gu[:, D_EXPERT:]
    hmid = a * _sigmoid(a) * b
    return jnp.dot(hmid.astype(BF16), w_out, preferred_element_type=F32)


def _moe_dense_kernel(x_ref, e_ref, p_ref, wi_ref, wo_ref, o_ref):
    e = pl.program_id(0)

    @pl.when(e == 0)
    def _():
        o_ref[...] = jnp.zeros_like(o_ref)

    eid, wts = e_ref[...], p_ref[...]
    sel0, sel1 = eid[:, 0:1] == e, eid[:, 1:2] == e
    coef = jnp.where(sel0, wts[:, 0:1], jnp.where(sel1, wts[:, 1:2], 0.0))
    y = _expert_ffn(x_ref[...].astype(BF16), wi_ref[0].astype(BF16), wo_ref[0].astype(BF16))
    o_ref[...] += jnp.where(sel0 | sel1, y * coef, 0.0)


def moe_dense(xn, eid, wts, w_in, w_out, *, layer):
    n_rows = xn.shape[0]
    full = lambda e: (0, 0)
    return pl.pallas_call(
        _moe_dense_kernel, grid=(N_EXPERTS,),
        in_specs=[pl.BlockSpec((n_rows, D_MODEL), full), pl.BlockSpec((n_rows, LANES), full),
                  pl.BlockSpec((n_rows, LANES), full),
                  pl.BlockSpec((None, 1, D_MODEL, 2 * D_EXPERT), lambda e: (layer, e, 0, 0)),
                  pl.BlockSpec((None, 1, D_EXPERT, D_MODEL), lambda e: (layer, e, 0, 0))],
        out_specs=pl.BlockSpec((n_rows, D_MODEL), full),
        out_shape=jax.ShapeDtypeStruct((n_rows, D_MODEL), F32),
        compiler_params=_cparams(("arbitrary",)), name="moe_dense")(xn, eid, wts, w_in, w_out)


def _moe_sorted_kernel(blk_e_ref, n_used_ref, src_ref, src_next_ref, dst_prev_ref, dst_ref, x_hbm, wi_ref,
                       wo_ref, y_hbm, xbuf, obuf, x16_ref, hmid_ref, wi16_ref, wo16_ref, gsem, ssem, *,
                       tm, n_tok):
    i = pl.program_id(0)
    n_used = n_used_ref[0]
    nc = MOE_NC

    def start_gathers(rows_ref, s, lo=0, hi=tm):
        for r in range(lo, hi):
            pltpu.make_async_copy(x_hbm.at[pl.ds(rows_ref[0, 0, r], 1), :], xbuf.at[s, pl.ds(r, 1), :],
                                  gsem.at[s]).start()

    def start_scatters(rows_ref, s, lo=0, hi=tm):
        for r in range(lo, hi):
            pltpu.make_async_copy(obuf.at[s, pl.ds(r, 1), :], y_hbm.at[pl.ds(rows_ref[0, 0, r], 1), :],
                                  ssem.at[s]).start()

    def wait_block(buf, sem, s):
        pltpu.make_async_copy(buf.at[s, pl.ds(0, tm)], buf.at[s, pl.ds(0, tm)], sem.at[s]).wait()

    @pl.when(i == 0)
    def _():
        xbuf[...] = jnp.zeros_like(xbuf)
        obuf[...] = jnp.zeros_like(obuf)
        spare = [pltpu.make_async_copy(obuf.at[s], y_hbm.at[pl.ds(2 * n_tok + s * tm, tm), :], ssem.at[s])
                 for s in range(2)]
        for cp in spare:
            cp.start()
        for cp in spare:
            cp.wait()
        start_gathers(src_ref, 0)

    def step(slot):
        wait_block(xbuf, gsem, slot)

        @pl.when(i >= 1)
        def _():
            wait_block(obuf, ssem, slot)

        n_a, n_b = 2 * D_EXPERT // nc, D_MODEL // nc
        t_a, t_b = D_MODEL // nc, D_EXPERT // nc
        first = 8
        unit = (2 * tm - first) / (n_a * t_a + n_b * t_b)
        quota = [first] + [round(unit * t_a)] * n_a + [round(unit * t_b)] * (n_b - 1)
        quota.append(2 * tm - sum(quota))
        issued = [0]

        def issue(lhs_ref):
            lo, hi = issued[0], issued[0] + quota.pop(0)
            issued[0] = hi
            start_scatters(dst_prev_ref, 1 - slot, (lo + 1) // 2, (hi + 1) // 2)
            start_gathers(src_next_ref, 1 - slot, lo // 2, hi // 2)
            if lhs_ref is not None:
                zero = xbuf[1 - slot, tm:tm + 8, 0:LANES]
                zero = jnp.concatenate([zero, zero], axis=0)
                lhs_ref[0:16, 0:LANES] = (lhs_ref[0:16, 0:LANES].astype(F32) + zero).astype(BF16)

        x16_ref[...] = xbuf[slot, 0:tm].astype(BF16)
        for k in range(D_EXPERT // nc):
            issue(x16_ref)
            a = jnp.dot(x16_ref[...], wi16_ref[:, k * nc:(k + 1) * nc], preferred_element_type=F32)
            issue(x16_ref)
            b = jnp.dot(x16_ref[...], wi16_ref[:, D_EXPERT + k * nc:D_EXPERT + (k + 1) * nc],
                        preferred_element_type=F32)
            hmid_ref[:, k * nc:(k + 1) * nc] = (a * _sigmoid(a) * b).astype(BF16)
        for k in range(n_b):
            issue(hmid_ref)
            obuf[slot, :, k * nc:(k + 1) * nc] = jnp.dot(hmid_ref[...], wo16_ref[:, k * nc:(k + 1) * nc],
                                                         preferred_element_type=F32)
        issue(None)
        assert not quota and issued[0] == 2 * tm

        @pl.when(i == n_used - 1)
        def _():
            start_scatters(dst_ref, slot)
            wait_block(obuf, ssem, 1 - slot)
            wait_block(obuf, ssem, slot)
            wait_block(xbuf, gsem, 1 - slot)

    @pl.when((i < n_used) & ((i == 0) | (blk_e_ref[i] != blk_e_ref[jnp.maximum(i - 1, 0)])))
    def _():
        wi16_ref[...] = wi_ref[0].astype(BF16)
        wo16_ref[...] = wo_ref[0].astype(BF16)

    for parity in range(2):
        pl.when((i < n_used) & (i % 2 == parity))(functools.partial(step, parity))


def moe_sorted(xn, w_in, w_out, tables, *, tm, layer):
    blk_e, n_used, src, dst = tables
    n_tok = xn.shape[0]
    n_blocks = blk_e.shape[0]
    smem = lambda index_map: pl.BlockSpec((1, 1, tm), index_map, memory_space=pltpu.SMEM)
    nxt = lambda i, be, nu: (jnp.maximum(jnp.minimum(i + 1, nu[0] - 1), 0), 0, 0)
    grid_spec = pltpu.PrefetchScalarGridSpec(
        num_scalar_prefetch=2, grid=(n_blocks,),
        in_specs=[smem(lambda i, be, nu: (i, 0, 0)), smem(nxt),
                  smem(lambda i, be, nu: (i, 0, 0)), smem(lambda i, be, nu: (i + 1, 0, 0)),
                  pl.BlockSpec(memory_space=pl.ANY),
                  pl.BlockSpec((None, 1, D_MODEL, 2 * D_EXPERT), lambda i, be, nu: (layer, be[i], 0, 0)),
                  pl.BlockSpec((None, 1, D_EXPERT, D_MODEL), lambda i, be, nu: (layer, be[i], 0, 0))],
        out_specs=pl.BlockSpec(memory_space=pl.ANY),
        scratch_shapes=[pltpu.VMEM((2, tm + 8, D_MODEL), F32), pltpu.VMEM((2, tm, D_MODEL), F32),
                        pltpu.VMEM((tm, D_MODEL), BF16), pltpu.VMEM((tm, D_EXPERT), BF16),
                        pltpu.VMEM((D_MODEL, 2 * D_EXPERT), BF16), pltpu.VMEM((D_EXPERT, D_MODEL), BF16),
                        pltpu.SemaphoreType.DMA((2,)), pltpu.SemaphoreType.DMA((2,))])
    return pl.pallas_call(
        functools.partial(_moe_sorted_kernel, tm=tm, n_tok=n_tok), grid_spec=grid_spec,
        out_shape=jax.ShapeDtypeStruct((2 * n_tok + 2 * tm, D_MODEL), F32),
        compiler_params=_cparams(("arbitrary",)), name="moe_sorted")(
            blk_e, n_used, src, src, dst, dst, xn, w_in, w_out)


def route_tables(eid, *, tm):
    n_tok = eid.shape[0]
    n_asg = 2 * n_tok
    flat = eid.reshape(-1)
    skey = jnp.sort(flat * n_asg + jnp.arange(n_asg, dtype=I32))
    order = skey % n_asg
    counts = jnp.sum((flat[:, None] == jnp.arange(N_EXPERTS, dtype=I32)[None, :]).astype(I32), axis=0)
    starts = jnp.cumsum(counts) - counts
    padded = (counts + tm - 1) // tm * tm
    pend = jnp.cumsum(padded)
    pstarts = pend - padded
    n_blocks = -(-(n_asg + N_EXPERTS * (tm - 1)) // tm)
    blk = jnp.arange(n_blocks, dtype=I32)
    ends = jnp.zeros((n_blocks + 1,), I32).at[jnp.minimum(pend // tm, n_blocks)].add(1)
    blk_e = jnp.minimum(jnp.cumsum(ends)[:n_blocks], N_EXPERTS - 1)
    blk_off = blk * tm - pstarts[blk_e]
    blk_n = jnp.clip(counts[blk_e] - blk_off, 0, tm)
    r = jnp.arange(tm, dtype=I32)[None, :]
    valid = r < blk_n[:, None]
    code = order[jnp.clip((starts[blk_e] + blk_off)[:, None] + r, 0, n_asg - 1)]
    src = jnp.where(valid, code >> 1, 0)
    spare = 2 * n_tok + (blk % 2)[:, None] * tm + r
    dst = jnp.where(valid, (code & 1) * n_tok + (code >> 1), spare)
    dst = jnp.concatenate([2 * n_tok + tm + r, dst], axis=0)
    n_used = (pend[-1] // tm).reshape(1)
    return (blk_e.astype(I32), n_used.astype(I32), src.reshape(n_blocks, 1, tm).astype(I32),
            dst.reshape(n_blocks + 1, 1, tm).astype(I32))


def _q_head_order():
    order = []
    for pr in range(N_KV_HEADS // 2):
        for r in range(Q_PER_KV):
            for half in range(2):
                order.append((2 * pr + half) * Q_PER_KV + r)
    return jnp.asarray(order, dtype=I32)


def _pad_lanes(a):
    return jnp.pad(a, [(0, 0)] * (a.ndim - 1) + [(0, LANES - a.shape[-1])])


def _ssd_params(conv_w, conv_b, dt_bias, a_log, d_skip, norm_w):
    head_of_col = jnp.arange(GROUP_W, dtype=I32) // SSD_HEAD_DIM
    row = jnp.arange(32, dtype=I32)
    expand = ((row[:, None] % 8 == head_of_col[None, :]) & (row[:, None] < 24)).astype(BF16)
    return {
        "conv_w": jnp.pad(conv_w, ((0, 8 - SSD_CONV), (0, 0))),
        "conv_b": conv_b.reshape(1, CONV_DIM),
        "dt_bias": _pad_lanes(dt_bias.reshape(1, SSD_HEADS)),
        "a_log": _pad_lanes(a_log.reshape(1, SSD_HEADS)),
        "d_skip": jnp.repeat(d_skip, SSD_HEAD_DIM).reshape(1, D_INNER),
        "norm_w": norm_w.reshape(1, D_INNER),
        "expand": expand,
    }


def _router_params(w_group, b_group, w_router, b_router):
    wr = _pad_lanes(jnp.concatenate([w_group, w_router], axis=1)).astype(BF16)
    br = _pad_lanes(jnp.concatenate([b_group, b_router]).reshape(1, -1)).astype(F32)
    return wr, br


def _pick_tile(n, prefs):
    for t in prefs:
        if n % t == 0:
            return t
    return n


def kernel(x_prompt, x_sample, cache_ssm_state, cache_ssm_conv, cache_attn_k, cache_attn_v, cache_meta_k, cache_meta_v, meta_tokens, norm_mix, norm_ffn, norm_final, ssd_w_in, ssd_conv_w, ssd_conv_b, ssd_dt_bias, ssd_a_log, ssd_d, ssd_norm, ssd_w_out, attn_w_qkv, attn_sinks, attn_w_out, moe_w_group, moe_b_group, moe_w_router, moe_b_router, moe_w_in, moe_w_out):
    n_batch, seq, _ = x_prompt.shape
    dec_batch, dec_seq, _ = x_sample.shape
    n_real = n_batch * seq
    n_small = N_META + dec_batch * dec_seq
    blk = SSD_BLOCK
    assert seq % blk == 0 and dec_seq <= blk and N_META <= blk
    tm_big = _pick_tile(n_real, (1024, 512, 256))
    tm_mid = _pick_tile(n_real, (512, 256))
    tm_res = _pick_tile(n_real, (256,))

    h_real = x_prompt.reshape(n_real, D_MODEL)
    h_small = jnp.concatenate([meta_tokens, x_sample.reshape(-1, D_MODEL)], axis=0)

    w_in16 = ssd_w_in[0].astype(BF16)
    w_zx, w_dt = w_in16[:, :ZX_DIM], _pad_lanes(w_in16[:, ZX_DIM:])
    sp = _ssd_params(ssd_conv_w[0], ssd_conv_b[0], ssd_dt_bias[0], ssd_a_log[0], ssd_d[0], ssd_norm[0])
    w_out16 = ssd_w_out[0].astype(BF16)
    wr0, br0 = _router_params(moe_w_group[0], moe_b_group[0], moe_w_router[0], moe_b_router[0])

    zx_s, dt_s = norm_matmul(h_small, norm_mix[0], w_zx, tm=n_small, tn=512, w_extra=w_dt)
    zx_r, dt_r = norm_matmul(h_real, norm_mix[0], w_zx, tm=tm_big, tn=1024, w_extra=w_dt)

    n_sb = 1 + dec_batch

    def to_blocks(a, n_valid_meta=N_META):
        meta = jnp.pad(a[:N_META], ((0, blk - N_META), (0, 0)))[None]
        samp = jnp.pad(a[N_META:].reshape(dec_batch, dec_seq, -1), ((0, 0), (0, blk - dec_seq), (0, 0)))
        return jnp.concatenate([meta, samp], axis=0).reshape(n_sb * blk, -1)

    assert dec_seq == N_META, "small-path blocks share one valid length"
    s0_small = jnp.concatenate([jnp.zeros((1, D_INNER, SSD_STATE), F32),
                                cache_ssm_state[0].reshape(dec_batch, D_INNER, SSD_STATE)], axis=0)
    c0_small = jnp.pad(jnp.concatenate([jnp.zeros((1, SSD_CONV - 1, CONV_DIM), F32), cache_ssm_conv[0]], axis=0),
                       ((0, 0), (8 - (SSD_CONV - 1), 0), (0, 0)))
    yn_sb, state_small = ssd_scan(to_blocks(zx_s), to_blocks(dt_s), sp, s0_small, c0_small,
                                  n_batch=n_sb, n_chunks=1, valid_len=dec_seq, shared_init=False)
    yn_sb = yn_sb.reshape(n_sb, blk, D_INNER)
    yn_small = jnp.concatenate([yn_sb[0, :N_META], yn_sb[1:, :dec_seq].reshape(-1, D_INNER)], axis=0)

    c0_real = jnp.pad(zx_s[N_META - (SSD_CONV - 1):N_META, D_INNER:], ((8 - (SSD_CONV - 1), 0), (0, 0)))[None]
    yn_real, state_real = ssd_scan(zx_r, dt_r, sp, state_small[0:1], c0_real,
                                   n_batch=n_batch, n_chunks=seq // blk, valid_len=blk, shared_init=True)

    tk0 = 2048
    h1_s, xn_s, eid_s, wts_s = matmul_route(yn_small, w_out16, h_small, norm_ffn[0], wr0, br0, tm=n_small, tk=tk0)
    h1_r, xn_r, eid_r, wts_r = matmul_route(yn_real, w_out16, h_real, norm_ffn[0], wr0, br0, tm=tm_res,
                                            tk=D_INNER)

    moe_s = moe_dense(xn_s, eid_s, wts_s, moe_w_in, moe_w_out, layer=0)
    moe_r = moe_sorted(xn_r, moe_w_in, moe_w_out, route_tables(eid_r[:, :2], tm=MOE_TM), tm=MOE_TM, layer=0)

    head_order = _q_head_order()
    wq = attn_w_qkv[0][:, :Q_DIM].reshape(D_MODEL, N_HEADS, ATTN_HEAD_DIM)[:, head_order].reshape(D_MODEL, Q_DIM)
    w_qkv16 = jnp.concatenate([wq, attn_w_qkv[0][:, Q_DIM:]], axis=1).astype(BF16)
    sinks = attn_sinks[0][head_order].astype(F32)
    w_ao16 = attn_w_out[0].reshape(N_HEADS, ATTN_HEAD_DIM, D_MODEL)[head_order].reshape(Q_DIM, D_MODEL).astype(BF16)
    wr1, br1 = _router_params(moe_w_group[1], moe_b_group[1], moe_w_router[1], moe_b_router[1])

    qkv_s, h2_s = norm_matmul(h1_s, norm_mix[1], w_qkv16, tm=n_small, tn=512, mode="sum", moe=(moe_s,),
                              write_h=True)
    qkv_r, h2_r = norm_matmul(h1_r, norm_mix[1], w_qkv16, tm=tm_res, tn=Q_DIM + 2 * KV_DIM, mode="pair",
                              moe=(moe_r, wts_r), write_h=True)

    k_meta, v_meta = qkv_s[:N_META, Q_DIM:Q_DIM + KV_DIM], qkv_s[:N_META, Q_DIM + KV_DIM:]
    k_new = qkv_s[N_META:, Q_DIM:Q_DIM + KV_DIM].reshape(dec_batch, dec_seq, KV_DIM)
    v_new = qkv_s[N_META:, Q_DIM + KV_DIM:].reshape(dec_batch, dec_seq, KV_DIM)
    o_meta = attn_full(qkv_s[None, :N_META, :Q_DIM], k_meta[None], v_meta[None], sinks)
    kk_s = jnp.concatenate([cache_meta_k[0].reshape(dec_batch, N_META, KV_DIM),
                            cache_attn_k[0].reshape(dec_batch, WINDOW, KV_DIM), k_new], axis=1)
    vv_s = jnp.concatenate([cache_meta_v[0].reshape(dec_batch, N_META, KV_DIM),
                            cache_attn_v[0].reshape(dec_batch, WINDOW, KV_DIM), v_new], axis=1)
    o_samp = attn_full(qkv_s[N_META:, :Q_DIM].reshape(dec_batch, dec_seq, Q_DIM), kk_s, vv_s, sinks)
    o_small = jnp.concatenate([o_meta.reshape(N_META, Q_DIM), o_samp.reshape(-1, Q_DIM)], axis=0)
    o_real = attn_band(qkv_r, k_meta, v_meta, sinks, n_batch=n_batch, seq=seq)

    h3_s, xn_s, eid_s, wts_s = matmul_route(o_small, w_ao16, h2_s, norm_ffn[1], wr1, br1, tm=n_small, tk=1024)
    h3_r, xn_r, eid_r, wts_r = matmul_route(o_real, w_ao16, h2_r, norm_ffn[1], wr1, br1, tm=tm_mid, tk=2048)

    moe_s = moe_dense(xn_s, eid_s, wts_s, moe_w_in, moe_w_out, layer=1)
    moe_r = moe_sorted(xn_r, moe_w_in, moe_w_out, route_tables(eid_r[:, :2], tm=MOE_TM), tm=MOE_TM, layer=1)

    y_small = final_norm(h3_s, norm_final, tm=n_small, mode="sum", moe=(moe_s,))
    y_real = final_norm(h3_r, norm_final, tm=tm_mid, mode="pair", moe=(moe_r, wts_r))

    kv_shape = (N_KV_HEADS, ATTN_HEAD_DIM)
    qkv_r3 = qkv_r.reshape(n_batch, seq, -1)
    zx_r3 = zx_r.reshape(n_batch, seq, ZX_DIM)
    zx_samp = zx_s[N_META:].reshape(dec_batch, dec_seq, ZX_DIM)
    return (
        y_real.reshape(n_batch, seq, D_MODEL),
        y_small[N_META:].reshape(dec_batch, dec_seq, D_MODEL),
        state_real.reshape(1, n_batch, SSD_HEADS, SSD_HEAD_DIM, SSD_STATE),
        zx_r3[None, :, seq - (SSD_CONV - 1):, D_INNER:],
        qkv_r3[:, seq - WINDOW:, Q_DIM:Q_DIM + KV_DIM].reshape((1, n_batch, WINDOW) + kv_shape),
        qkv_r3[:, seq - WINDOW:, Q_DIM + KV_DIM:].reshape((1, n_batch, WINDOW) + kv_shape),
        jnp.broadcast_to(k_meta.reshape((1, 1, N_META) + kv_shape), (1, n_batch, N_META) + kv_shape),
        jnp.broadcast_to(v_meta.reshape((1, 1, N_META) + kv_shape), (1, n_batch, N_META) + kv_shape),
        state_small[1:].reshape(1, dec_batch, SSD_HEADS, SSD_HEAD_DIM, SSD_STATE),
        zx_samp[None, :, dec_seq - (SSD_CONV - 1):, D_INNER:],
        k_new.reshape((1, dec_batch, dec_seq) + kv_shape),
        v_new.reshape((1, dec_batch, dec_seq) + kv_shape),
    )
```

```python
import functools

import jax
import jax.numpy as jnp
from jax import lax
from jax.experimental import pallas as pl
from jax.experimental.pallas import tpu as pltpu

F32, BF16, I32 = jnp.float32, jnp.bfloat16, jnp.int32

D_MODEL = 2048
N_META = 16
EPS = 1e-5
D_INNER = 4096
SSD_HEAD_DIM = 64
SSD_HEADS = 64
SSD_GROUPS = 8
SSD_STATE = 128
SSD_CONV = 4
GN = SSD_GROUPS * SSD_STATE
CONV_DIM = D_INNER + 2 * GN
ZX_DIM = D_INNER + CONV_DIM
GROUP_W = D_INNER // SSD_GROUPS
WINDOW = 128
CHUNK = 64
ATTN_HEAD_DIM = 64
N_HEADS = 32
N_KV_HEADS = 8
Q_PER_KV = N_HEADS // N_KV_HEADS
Q_DIM = N_HEADS * ATTN_HEAD_DIM
KV_DIM = N_KV_HEADS * ATTN_HEAD_DIM
N_GROUPS = 4
EXPERTS_PER_GROUP = 8
N_EXPERTS = 32
D_EXPERT = 512

LANES = 128
SSD_BLOCK = 128
NEG = -1e30
VMEM_LIMIT = 56 * 1024 * 1024
MOE_TM = 256
MOE_NC = 256
IN_NC = 256
ATTN_CHUNKS = 4
ATTN_LAG = ATTN_CHUNKS * N_HEADS


def _cparams(sem):
    return pltpu.CompilerParams(dimension_semantics=sem, vmem_limit_bytes=VMEM_LIMIT)


def _resident_spec(block_shape, index_map, *, resident):
    if resident:
        return pl.BlockSpec(block_shape, index_map, pipeline_mode=pl.Buffered(1))
    return pl.BlockSpec(block_shape, index_map)


def _sigmoid(x):
    return 1.0 / (1.0 + jnp.exp(-x))


def _rms(h, g):
    var = jnp.mean(h * h, axis=-1, keepdims=True)
    return h * lax.rsqrt(var + EPS) * g


def _row_chunk(tm):
    for rc in (256, 128, 64, 32, 16, 8):
        if tm % rc == 0:
            return rc
    raise ValueError(tm)


_N_MOE_REFS = {"none": 0, "sum": 1, "pair": 3}


def _combine_rows(x_ref, moe_refs, mode, rows):
    h = x_ref[rows, :]
    if mode == "sum":
        h = h + moe_refs[0][rows, :]
    elif mode == "pair":
        y0, y1, w = moe_refs
        wv = w[rows, :]
        h = h + (y0[rows, :] * wv[:, 0:1] + y1[rows, :] * wv[:, 1:2])
    return h


def _norm_mm_kernel(*refs, mode, write_h, has_extra, tm):
    it = iter(refs)
    x_ref = next(it)
    moe_refs = [next(it) for _ in range(_N_MOE_REFS[mode])]
    g_ref, w_ref = next(it), next(it)
    we_ref = next(it) if has_extra else None
    o_ref = next(it)
    h_ref = next(it) if write_h else None
    e_ref = next(it) if has_extra else None
    xn_ref = next(it)
    rc = _row_chunk(tm)

    @pl.when(pl.program_id(1) == 0)
    def _():
        def body(r, carry):
            rows = pl.ds(pl.multiple_of(r * rc, rc), rc)
            h = _combine_rows(x_ref, moe_refs, mode, rows)
            if write_h:
                h_ref[rows, :] = h
            xn_ref[rows, :] = _rms(h, g_ref[...]).astype(BF16)
            return carry

        lax.fori_loop(0, tm // rc, body, 0)
        if has_extra:
            e_ref[...] = jnp.dot(xn_ref[...], we_ref[...], preferred_element_type=F32)

    o_ref[...] = jnp.dot(xn_ref[...], w_ref[...], preferred_element_type=F32)


def _moe_in_specs(mode, moe, tm, n_rows):
    if mode == "none":
        return [], []
    if mode == "sum":
        return [pl.BlockSpec((tm, D_MODEL), lambda i, j: (i, 0))], [moe[0]]
    y2, wts = moe
    off = n_rows // tm
    specs = [pl.BlockSpec((tm, D_MODEL), lambda i, j: (i, 0)),
             pl.BlockSpec((tm, D_MODEL), lambda i, j: (i + off, 0)),
             pl.BlockSpec((tm, LANES), lambda i, j: (i, 0))]
    return specs, [y2, y2, wts]


def norm_matmul(x, g, w, *, tm, tn, mode="none", moe=(), w_extra=None, write_h=False):
    n_rows, n_out = x.shape[0], w.shape[1]
    assert n_rows % tm == 0 and n_out % tn == 0
    moe_specs, moe_ops = _moe_in_specs(mode, moe, tm, n_rows)
    in_specs = [pl.BlockSpec((tm, D_MODEL), lambda i, j: (i, 0))] + moe_specs + [
        pl.BlockSpec((1, D_MODEL), lambda i, j: (0, 0)),
        _resident_spec((D_MODEL, tn), lambda i, j: (0, j), resident=n_out == tn)]
    operands = [x] + moe_ops + [g.reshape(1, D_MODEL), w]
    out_shape = [jax.ShapeDtypeStruct((n_rows, n_out), F32)]
    out_specs = [pl.BlockSpec((tm, tn), lambda i, j: (i, j))]
    if w_extra is not None:
        in_specs.append(pl.BlockSpec((D_MODEL, LANES), lambda i, j: (0, 0)))
        operands.append(w_extra)
    if write_h:
        out_shape.append(jax.ShapeDtypeStruct((n_rows, D_MODEL), F32))
        out_specs.append(pl.BlockSpec((tm, D_MODEL), lambda i, j: (i, 0)))
    if w_extra is not None:
        out_shape.append(jax.ShapeDtypeStruct((n_rows, LANES), F32))
        out_specs.append(pl.BlockSpec((tm, LANES), lambda i, j: (i, 0)))
    kern = functools.partial(_norm_mm_kernel, mode=mode, write_h=write_h,
                             has_extra=w_extra is not None, tm=tm)
    return pl.pallas_call(
        kern, grid=(n_rows // tm, n_out // tn), in_specs=in_specs, out_specs=out_specs,
        out_shape=out_shape, scratch_shapes=[pltpu.VMEM((tm, D_MODEL), BF16)],
        compiler_params=_cparams(("parallel", "arbitrary")), name="norm_matmul")(*operands)


def _final_norm_kernel(*refs, mode, tm):
    x_ref = refs[0]
    moe_refs = refs[1:1 + _N_MOE_REFS[mode]]
    g_ref, o_ref = refs[-2], refs[-1]
    rc = _row_chunk(tm)

    def body(r, carry):
        rows = pl.ds(pl.multiple_of(r * rc, rc), rc)
        o_ref[rows, :] = _rms(_combine_rows(x_ref, moe_refs, mode, rows), g_ref[...])
        return carry

    lax.fori_loop(0, tm // rc, body, 0)


def final_norm(x, g, *, tm, mode, moe):
    n_rows = x.shape[0]
    moe_specs, moe_ops = _moe_in_specs(mode, moe, tm, n_rows)
    in_specs = [pl.BlockSpec((tm, D_MODEL), lambda i, j: (i, 0))] + moe_specs + [
        pl.BlockSpec((1, D_MODEL), lambda i, j: (0, 0))]
    return pl.pallas_call(
        functools.partial(_final_norm_kernel, mode=mode, tm=tm), grid=(n_rows // tm, 1),
        in_specs=in_specs, out_specs=pl.BlockSpec((tm, D_MODEL), lambda i, j: (i, 0)),
        out_shape=jax.ShapeDtypeStruct((n_rows, D_MODEL), F32),
        compiler_params=_cparams(("parallel", "arbitrary")), name="final_norm")(
            x, *moe_ops, g.reshape(1, D_MODEL))


def _in_proj_kernel(x_ref, g_ref, w_ref, wdt_ref, cw_ref, cb_ref, c0_ref, o_ref, dt_ref, rt_ref,
                    xn_ref, tail_ref, stage_ref, *, tm, tn, tiles_per_batch, n_z):
    i, j = pl.program_id(0), pl.program_id(1)
    rc = _row_chunk(tm)

    @pl.when(j == 0)
    def _():
        def body(r, carry):
            rows = pl.ds(pl.multiple_of(r * rc, rc), rc)
            xn_ref[rows, :] = _rms(x_ref[rows, :], g_ref[...]).astype(BF16)
            return carry

        lax.fori_loop(0, tm // rc, body, 0)
        dt_ref[...] = jnp.dot(xn_ref[...], wdt_ref[...], preferred_element_type=F32)

    def chunks():
        for c in range(tn // IN_NC):
            cs = slice(c * IN_NC, (c + 1) * IN_NC)
            yield c, cs, jnp.dot(xn_ref[...], w_ref[:, cs], preferred_element_type=F32)

    @pl.when(j < n_z)
    def _():
        for _, cs, raw in chunks():
            o_ref[:, cs] = raw * _sigmoid(raw)

    @pl.when(j >= n_z)
    def _():
        e = j - n_z

        @pl.when(i % tiles_per_batch == 0)
        def _():
            tail_ref[e] = c0_ref[...]

        for c, cs, raw in chunks():
            stage = stage_ref.at[c % 2]
            stage[0:8, :] = tail_ref[e, :, cs]
            stage[8:8 + tm, :] = raw
            acc = cb_ref[:, cs]
            for k in range(SSD_CONV):
                acc = acc + stage[pl.ds(8 - (SSD_CONV - 1) + k, tm), :] * cw_ref[k:k + 1, cs]
            o_ref[:, cs] = acc * _sigmoid(acc)
            last = stage[tm:tm + 8, :]
            tail_ref[e, :, cs] = last
            rt_ref[0, :, cs] = last


def ssd_in_proj(x, g, w, w_dt, p, c0, *, tm, tn, seq):
    n_rows = x.shape[0]
    assert n_rows % tm == 0 and seq % tm == 0 and D_INNER % tn == 0 and CONV_DIM % tn == 0
    n_z = D_INNER // tn
    conv_col = lambda i, j: (0, jnp.maximum(j - n_z, 0))
    kern = functools.partial(_in_proj_kernel, tm=tm, tn=tn, tiles_per_batch=seq // tm, n_z=n_z)
    return pl.pallas_call(
        kern, grid=(n_rows // tm, ZX_DIM // tn),
        in_specs=[pl.BlockSpec((tm, D_MODEL), lambda i, j: (i, 0)),
                  pl.BlockSpec((1, D_MODEL), lambda i, j: (0, 0)),
                  pl.BlockSpec((D_MODEL, tn), lambda i, j: (0, j)),
                  pl.BlockSpec((D_MODEL, LANES), lambda i, j: (0, 0)),
                  pl.BlockSpec((8, tn), conv_col),
                  pl.BlockSpec((1, tn), conv_col),
                  pl.BlockSpec((8, tn), conv_col)],
        out_specs=[pl.BlockSpec((tm, tn), lambda i, j: (i, j)),
                   pl.BlockSpec((tm, LANES), lambda i, j: (i, 0)),
                   pl.BlockSpec((1, 8, tn), lambda i, j: (i, 0, jnp.maximum(j - n_z, 0)))],
        out_shape=[jax.ShapeDtypeStruct((n_rows, ZX_DIM), F32),
                   jax.ShapeDtypeStruct((n_rows, LANES), F32),
                   jax.ShapeDtypeStruct((n_rows // tm, 8, CONV_DIM), F32)],
        scratch_shapes=[pltpu.VMEM((tm, D_MODEL), BF16),
                        pltpu.VMEM((CONV_DIM // tn, 8, tn), F32),
                        pltpu.VMEM((2, tm + 8, IN_NC), F32)],
        compiler_params=_cparams(("arbitrary", "arbitrary")), name="ssd_in_proj")(
            x, g.reshape(1, D_MODEL), w, w_dt, p["conv_w"], p["conv_b"], c0)


def _route(lg):
    lane = lax.broadcasted_iota(I32, lg.shape, 1)
    is_g = lane < N_GROUPS
    gl = jnp.where(is_g, lg, NEG)
    gmax = jnp.max(gl, axis=-1, keepdims=True)
    gsum = jnp.sum(jnp.where(is_g, jnp.exp(gl - gmax), 0.0), axis=-1, keepdims=True)
    gate = 1.0 / gsum
    gidx = jnp.min(jnp.where(gl == gmax, lane, LANES), axis=-1, keepdims=True)
    ecol = lane - N_GROUPS
    in_group = (ecol >= 0) & (ecol < N_EXPERTS) & ((ecol >> 3) == gidx)
    el = jnp.where(in_group, lg, NEG)
    v1 = jnp.max(el, axis=-1, keepdims=True)
    i1 = jnp.min(jnp.where(el == v1, lane, LANES), axis=-1, keepdims=True)
    el2 = jnp.where(lane == i1, NEG, el)
    v2 = jnp.max(el2, axis=-1, keepdims=True)
    i2 = jnp.min(jnp.where(el2 == v2, lane, LANES), axis=-1, keepdims=True)
    e2 = jnp.exp(v2 - v1)
    w1 = gate / (1.0 + e2)
    w2 = gate * e2 / (1.0 + e2)
    eid = jnp.where(lane == 0, i1 - N_GROUPS, jnp.where(lane == 1, i2 - N_GROUPS, 0))
    wts = jnp.where(lane == 0, w1, jnp.where(lane == 1, w2, 0.0))
    return eid, wts


def _mm_route_kernel(a_ref, w_ref, h_ref, g_ref, wr_ref, br_ref,
                     h1_ref, xn_ref, e_ref, p_ref, acc_ref, *, nk, tm):
    k = pl.program_id(1)
    rc = _row_chunk(tm)

    @pl.when(k == 0)
    def _():
        acc_ref[...] = jnp.zeros_like(acc_ref)

    acc_ref[...] += jnp.dot(a_ref[...], w_ref[...], preferred_element_type=F32)

    @pl.when(k == nk - 1)
    def _():
        def body(r, carry):
            rows = pl.ds(pl.multiple_of(r * rc, rc), rc)
            h1 = h_ref[rows, :] + acc_ref[rows, :]
            h1_ref[rows, :] = h1
            xn = _rms(h1, g_ref[...])
            xn_ref[rows, :] = xn
            lg = jnp.dot(xn.astype(BF16), wr_ref[...], preferred_element_type=F32) + br_ref[...]
            eid, wts = _route(lg)
            e_ref[rows, :] = eid
            p_ref[rows, :] = wts
            return carry

        lax.fori_loop(0, tm // rc, body, 0)


def matmul_route(a, w, h, g, wr, br, *, tm, tk):
    n_rows, kdim = a.shape
    assert n_rows % tm == 0 and kdim % tk == 0
    nk = kdim // tk
    row_spec = pl.BlockSpec((tm, D_MODEL), lambda i, k: (i, 0))
    lane_spec = pl.BlockSpec((tm, LANES), lambda i, k: (i, 0))
    return pl.pallas_call(
        functools.partial(_mm_route_kernel, nk=nk, tm=tm), grid=(n_rows // tm, nk),
        in_specs=[pl.BlockSpec((tm, tk), lambda i, k: (i, k)),
                  _resident_spec((tk, D_MODEL), lambda i, k: (k, 0), resident=nk == 1),
                  row_spec,
                  pl.BlockSpec((1, D_MODEL), lambda i, k: (0, 0)),
                  pl.BlockSpec((D_MODEL, LANES), lambda i, k: (0, 0)),
                  pl.BlockSpec((1, LANES), lambda i, k: (0, 0))],
        out_specs=[row_spec, row_spec, lane_spec, lane_spec],
        out_shape=[jax.ShapeDtypeStruct((n_rows, D_MODEL), F32),
                   jax.ShapeDtypeStruct((n_rows, D_MODEL), F32),
                   jax.ShapeDtypeStruct((n_rows, LANES), I32),
                   jax.ShapeDtypeStruct((n_rows, LANES), F32)],
        scratch_shapes=[pltpu.VMEM((tm, D_MODEL), F32)],
        compiler_params=_cparams(("parallel", "arbitrary")), name="matmul_route")(
            a, w, h, g.reshape(1, D_MODEL), wr, br)


def _expand_heads(v, eg):
    v1 = v.astype(BF16).astype(F32)
    r1 = v - v1
    v2 = r1.astype(BF16).astype(F32)
    v3 = (r1 - v2).astype(BF16).astype(F32)
    parts = jnp.concatenate([v1, v2, v3, jnp.zeros_like(v)], axis=0).astype(BF16)
    return lax.dot_general(parts, eg, (((0,), (0,)), ((), ())), preferred_element_type=F32)


def _ssd_kernel(z_ref, x_ref, b_ref, c_ref, dt_ref, cw_ref, cb_ref, dtb_ref, alog_ref, dsk_ref,
                nw_ref, eg_ref, s0_ref, c0_ref, y_ref, so_ref,
                state_ref, xp_ref, cumt_ref, dtt_ref, *, valid_len, n_chunks, conv_done):
    blk = SSD_BLOCK
    c = pl.program_id(1)

    @pl.when(c == 0)
    def _():
        state_ref[...] = s0_ref[0].T
        if not conv_done:
            xp_ref[0:8, :] = c0_ref[0]

    if not conv_done:
        @pl.when(c > 0)
        def _():
            xp_ref[0:8, :] = xp_ref[blk:blk + 8, :]

        xp_ref[8:8 + blk, 0:D_INNER] = x_ref[...]
        xp_ref[8:8 + blk, D_INNER:D_INNER + GN] = b_ref[...]
        xp_ref[8:8 + blk, D_INNER + GN:CONV_DIM] = c_ref[...]

    row = lax.broadcasted_iota(I32, (blk, LANES), 0)
    lane = lax.broadcasted_iota(I32, (blk, LANES), 1)
    dtr = dt_ref[...] + dtb_ref[...]
    dt = jnp.maximum(dtr, 0.0) + jnp.log1p(jnp.exp(-jnp.abs(dtr)))
    if valid_len < blk:
        dt = jnp.where(row < valid_len, dt, 0.0)
    cum = dt * (-jnp.exp(alog_ref[...]))
    sh = 1
    while sh < blk:
        cum = cum + jnp.where(row >= sh, pltpu.roll(cum, sh, 0), 0.0)
        sh *= 2
    cumt_ref[...] = cum.T
    dtt_ref[...] = dt.T

    causal = lane <= row
    left = lane < SSD_HEAD_DIM

    def conv(cols):
        acc = cb_ref[:, cols]
        for k in range(SSD_CONV):
            acc = acc + xp_ref[pl.ds(8 - (SSD_CONV - 1) + k, blk), cols] * cw_ref[k:k + 1, cols]
        return acc * _sigmoid(acc)

    def group(g, carry):
        cols = pl.ds(pl.multiple_of(g * GROUP_W, GROUP_W), GROUP_W)
        if conv_done:
            ncols = pl.ds(pl.multiple_of(g * SSD_STATE, SSD_STATE), SSD_STATE)
            xg, bg, cg = x_ref[:, cols], b_ref[:, ncols], c_ref[:, ncols]
        else:
            xg = conv(cols)
            bg = conv(pl.ds(pl.multiple_of(D_INNER + g * SSD_STATE, SSD_STATE), SSD_STATE))
            cg = conv(pl.ds(pl.multiple_of(D_INNER + GN + g * SSD_STATE, SSD_STATE), SSD_STATE))
        cb16, bb16 = cg.astype(BF16), bg.astype(BF16)
        cbm = lax.dot_general(cb16, bb16, (((1,), (1,)), ((), ())), preferred_element_type=F32)
        sg = state_ref[:, cols]
        heads = pl.ds(pl.multiple_of(g * (GROUP_W // SSD_HEAD_DIM), 8), GROUP_W // SSD_HEAD_DIM)
        ecg = _expand_heads(cumt_ref[heads, :], eg_ref[...])
        edg = _expand_heads(dtt_ref[heads, :], eg_ref[...])
        y_off =jnp.dot(cb16, sg.astype(BF16), preferred_element_type=F32) * jnp.exp(ecg)
        parts = []
        for pr in range(GROUP_W // LANES):
            xpair = xg[:, pr * LANES:(pr + 1) * LANES]
            ec = ecg[:, pr * LANES:(pr + 1) * LANES]
            ec_sw = pltpu.roll(ec, SSD_HEAD_DIM, 1)
            acc = jnp.zeros((blk, LANES), F32)
            for q in range(2):
                head = g * (GROUP_W // SSD_HEAD_DIM) + pr * 2 + q
                cum_l = jnp.where(left, ec, ec_sw) if q == 0 else jnp.where(left, ec_sw, ec)
                cum_s = cumt_ref[pl.ds(head, 1), :]
                decay = jnp.exp(jnp.where(causal, cum_l - cum_s, NEG))
                wmat = (decay * cbm * dtt_ref[pl.ds(head, 1), :]).astype(BF16)
                xm = jnp.where(left if q == 0 else jnp.logical_not(left), xpair, 0.0).astype(BF16)
                acc = acc + jnp.dot(wmat, xm, preferred_element_type=F32)
            parts.append(acc)
        y = jnp.concatenate(parts, axis=1) + y_off
        y = y + xg * dsk_ref[:, cols]
        zg = z_ref[:, cols]
        y = y * (zg if conv_done else zg * _sigmoid(zg))
        y = y * lax.rsqrt(jnp.mean(y * y, axis=-1, keepdims=True) + EPS) * nw_ref[:, cols]
        y_ref[:, cols] = y.astype(BF16)
        cl = ecg[blk - 1:blk, :]
        xt = (xg * (jnp.exp(cl - ecg) * edg)).astype(BF16)
        h_in = lax.dot_general(bb16, xt, (((0,), (0,)), ((), ())), preferred_element_type=F32)
        state_ref[:, cols] = sg * jnp.exp(cl) + h_in
        return carry

    lax.fori_loop(0, SSD_GROUPS, group, 0, unroll=2)

    @pl.when(c == n_chunks - 1)
    def _():
        so_ref[0] = state_ref[...].T


def ssd_scan(zx, dt, p, s0, c0, *, n_batch, n_chunks, valid_len, shared_init, conv_done=False):
    blk = SSD_BLOCK
    n_rows = n_batch * n_chunks * blk
    assert zx.shape == (n_rows, ZX_DIM)
    rowi = lambda b, c: b * n_chunks + c
    init = (lambda b, c: (0, 0, 0)) if shared_init else (lambda b, c: (b, 0, 0))
    const = lambda b, c: (0, 0)
    in_specs = [
        pl.BlockSpec((blk, D_INNER), lambda b, c: (rowi(b, c), 0)),
        pl.BlockSpec((blk, D_INNER), lambda b, c: (rowi(b, c), 1)),
        pl.BlockSpec((blk, GN), lambda b, c: (rowi(b, c), 2 * D_INNER // GN)),
        pl.BlockSpec((blk, GN), lambda b, c: (rowi(b, c), 2 * D_INNER // GN + 1)),
        pl.BlockSpec((blk, LANES), lambda b, c: (rowi(b, c), 0)),
        pl.BlockSpec((8, CONV_DIM), const),
        pl.BlockSpec((1, CONV_DIM), const),
        pl.BlockSpec((1, LANES), const),
        pl.BlockSpec((1, LANES), const),
        pl.BlockSpec((1, D_INNER), const),
        pl.BlockSpec((1, D_INNER), const),
        pl.BlockSpec((32, GROUP_W), const),
        pl.BlockSpec((1, D_INNER, SSD_STATE), init),
        pl.BlockSpec((1, 8, CONV_DIM), init),
    ]
    out_specs = [pl.BlockSpec((blk, D_INNER), lambda b, c: (rowi(b, c), 0)),
                 pl.BlockSpec((1, D_INNER, SSD_STATE), lambda b, c: (b, 0, 0))]
    out_shape = [jax.ShapeDtypeStruct((n_rows, D_INNER), BF16),
                 jax.ShapeDtypeStruct((n_batch, D_INNER, SSD_STATE), F32)]
    scratch = [pltpu.VMEM((SSD_STATE, D_INNER), F32),
               pltpu.VMEM((8, LANES) if conv_done else (blk + 8, CONV_DIM), F32),
               pltpu.VMEM((LANES, blk), F32),
               pltpu.VMEM((LANES, blk), F32)]
    kern = functools.partial(_ssd_kernel, valid_len=valid_len, n_chunks=n_chunks, conv_done=conv_done)
    return pl.pallas_call(
        kern, grid=(n_batch, n_chunks), in_specs=in_specs, out_specs=out_specs,
        out_shape=out_shape, scratch_shapes=scratch,
        compiler_params=_cparams(("parallel", "arbitrary")), name="ssd_scan")(
            zx, zx, zx, zx, dt, p["conv_w"], p["conv_b"], p["dt_bias"], p["a_log"], p["d_skip"],
            p["norm_w"], p["expand"], s0, c0)


def _attn_core(items, sink_ref):
    n_q = items[0][0].shape[0]
    scale = ATTN_HEAD_DIM ** -0.5
    left = lax.broadcasted_iota(I32, (n_q, LANES), 1) < ATTN_HEAD_DIM
    nt = (((1,), (1,)), ((), ()))
    n_pairs = KV_DIM // LANES
    kps = [[kk[:, pr * LANES:(pr + 1) * LANES].astype(BF16) for pr in range(n_pairs)] for _, kk, _, _ in items]
    vps = [[vv[:, pr * LANES:(pr + 1) * LANES].astype(BF16) for pr in range(n_pairs)] for _, _, vv, _ in items]

    def scores(t, h):
        q, kmask = items[t][0], items[t][3]
        blk, half = divmod(h, 2)
        qb = q[:, blk * LANES:(blk + 1) * LANES]
        qh = jnp.where(left if half == 0 else jnp.logical_not(left), qb, 0.0).astype(BF16)
        s = lax.dot_general(qh, kps[t][blk // Q_PER_KV], nt, preferred_element_type=F32) * scale
        return s if kmask is None else jnp.where(kmask, s, NEG)

    def attend(t, h, s):
        sink = sink_ref[h]
        m = jnp.maximum(jnp.max(s, axis=-1, keepdims=True), sink)
        p = jnp.exp(s - m)
        den = jnp.sum(p, axis=-1, keepdims=True) + jnp.exp(sink - m)
        p = p * (1.0 / den)
        return jnp.dot(p.astype(BF16), vps[t][h // (2 * Q_PER_KV)], preferred_element_type=F32)

    work = [(t, h) for t in range(len(items)) for h in range(N_HEADS)]
    pending, done = {}, {}
    for n in range(len(work) + ATTN_LAG):
        if n < len(work):
            pending[work[n]] = scores(*work[n])
        if n >= ATTN_LAG:
            key = work[n - ATTN_LAG]
            done[key] = attend(*key, pending.pop(key))
    return [jnp.concatenate([jnp.where(left, done[(t, 2 * b)], done[(t, 2 * b + 1)])
                             for b in range(N_HEADS // 2)], axis=1) for t in range(len(items))]


def _attn_band_kernel(sink_ref, q_ref, k_ref, v_ref, mk_ref, mv_ref, o_ref):
    band = WINDOW + CHUNK
    items = []
    for sub in range(ATTN_CHUNKS):
        c = pl.program_id(1) * ATTN_CHUNKS + sub
        first = jnp.maximum(c - WINDOW // CHUNK, 0)
        start = pl.multiple_of(first * CHUNK, CHUNK)
        kk = jnp.concatenate([mk_ref[...], k_ref[pl.ds(start, band), :]], axis=0)
        vv = jnp.concatenate([mv_ref[...], v_ref[pl.ds(start, band), :]], axis=0)
        j = lax.broadcasted_iota(I32, (1, N_META + band), 1)
        key_chunk = first + jnp.maximum(j - N_META, 0) // CHUNK
        kmask = (j < N_META) | (key_chunk <= c)
        items.append((q_ref[sub * CHUNK:(sub + 1) * CHUNK, :], kk, vv, kmask))
    for sub, o in enumerate(_attn_core(items, sink_ref)):
        o_ref[sub * CHUNK:(sub + 1) * CHUNK, :] = o.astype(BF16)


def attn_band(qkv, meta_k, meta_v, sinks, *, n_batch, seq):
    rows = ATTN_CHUNKS * CHUNK
    n_chunks = seq // rows
    assert seq >= WINDOW + CHUNK and seq % rows == 0
    kcol = Q_DIM // KV_DIM
    grid_spec = pltpu.PrefetchScalarGridSpec(
        num_scalar_prefetch=1, grid=(n_batch, n_chunks),
        in_specs=[pl.BlockSpec((rows, Q_DIM), lambda b, c, s: (b * n_chunks + c, 0)),
                  pl.BlockSpec((seq, KV_DIM), lambda b, c, s: (b, kcol)),
                  pl.BlockSpec((seq, KV_DIM), lambda b, c, s: (b, kcol + 1)),
                  pl.BlockSpec((N_META, KV_DIM), lambda b, c, s: (0, 0)),
                  pl.BlockSpec((N_META, KV_DIM), lambda b, c, s: (0, 0))],
        out_specs=pl.BlockSpec((rows, Q_DIM), lambda b, c, s: (b * n_chunks + c, 0)))
    return pl.pallas_call(
        _attn_band_kernel, grid_spec=grid_spec,
        out_shape=jax.ShapeDtypeStruct((n_batch * seq, Q_DIM), BF16),
        compiler_params=_cparams(("parallel", "arbitrary")), name="attn_band")(
            sinks, qkv, qkv, qkv, meta_k, meta_v)


def _attn_full_kernel(sink_ref, q_ref, k_ref, v_ref, o_ref):
    o_ref[...] = _attn_core([(q_ref[...], k_ref[...], v_ref[...], None)], sink_ref)[0].astype(BF16)


def attn_full(q, kk, vv, sinks):
    n_batch, n_q, _ = q.shape
    n_k = kk.shape[1]
    grid_spec = pltpu.PrefetchScalarGridSpec(
        num_scalar_prefetch=1, grid=(n_batch,),
        in_specs=[pl.BlockSpec((None, n_q, Q_DIM), lambda b, s: (b, 0, 0)),
                  pl.BlockSpec((None, n_k, KV_DIM), lambda b, s: (b, 0, 0)),
                  pl.BlockSpec((None, n_k, KV_DIM), lambda b, s: (b, 0, 0))],
        out_specs=pl.BlockSpec((None, n_q, Q_DIM), lambda b, s: (b, 0, 0)))
    return pl.pallas_call(
        _attn_full_kernel, grid_spec=grid_spec,
        out_shape=jax.ShapeDtypeStruct((n_batch, n_q, Q_DIM), BF16),
        compiler_params=_cparams(("parallel",)), name="attn_full")(sinks, q, kk, vv)


def _expert_ffn(x16, w_in, w_out):
    gu = jnp.dot(x16, w_in, preferred_element_type=F32)
    a, b = gu[:, :D_EXPERT], gu[:, D_EXPERT:]
    hmid = a * _sigmoid(a) * b
    return jnp.dot(hmid.astype(BF16), w_out, preferred_element_type=F32)


def _moe_dense_kernel(x_ref, e_ref, p_ref, wi_ref, wo_ref, o_ref):
    e = pl.program_id(0)

    @pl.when(e == 0)
    def _():
        o_ref[...] = jnp.zeros_like(o_ref)

    eid, wts = e_ref[...], p_ref[...]
    sel0, sel1 = eid[:, 0:1] == e, eid[:, 1:2] == e
    coef = jnp.where(sel0, wts[:, 0:1], jnp.where(sel1, wts[:, 1:2], 0.0))
    y = _expert_ffn(x_ref[...].astype(BF16), wi_ref[0].astype(BF16), wo_ref[0].astype(BF16))
    o_ref[...] += jnp.where(sel0 | sel1, y * coef, 0.0)


def moe_dense(xn, eid, wts, w_in, w_out, *, layer):
    n_rows = xn.shape[0]
    full = lambda e: (0, 0)
    return pl.pallas_call(
        _moe_dense_kernel, grid=(N_EXPERTS,),
        in_specs=[pl.BlockSpec((n_rows, D_MODEL), full), pl.BlockSpec((n_rows, LANES), full),
                  pl.BlockSpec((n_rows, LANES), full),
                  pl.BlockSpec((None, 1, D_MODEL, 2 * D_EXPERT), lambda e: (layer, e, 0, 0)),
                  pl.BlockSpec((None, 1, D_EXPERT, D_MODEL), lambda e: (layer, e, 0, 0))],
        out_specs=pl.BlockSpec((n_rows, D_MODEL), full),
        out_shape=jax.ShapeDtypeStruct((n_rows, D_MODEL), F32),
        compiler_params=_cparams(("arbitrary",)), name="moe_dense")(xn, eid, wts, w_in, w_out)


def _moe_sorted_kernel(blk_e_ref, n_used_ref, src_ref, src_next_ref, dst_prev_ref, dst_ref, x_hbm, wi_ref,
                       wo_ref, y_hbm, xbuf, obuf, x16_ref, hmid_ref, wi16_ref, wo16_ref, gsem, ssem, *,
                       tm, n_tok):
    i = pl.program_id(0)
    n_used = n_used_ref[0]
    nc = MOE_NC

    def start_gathers(rows_ref, s, lo=0, hi=tm):
        for r in range(lo, hi):
            pltpu.make_async_copy(x_hbm.at[pl.ds(rows_ref[0, 0, r], 1), :], xbuf.at[s, pl.ds(r, 1), :],
                                  gsem.at[s]).start()

    def start_scatters(rows_ref, s, lo=0, hi=tm):
        for r in range(lo, hi):
            pltpu.make_async_copy(obuf.at[s, pl.ds(r, 1), :], y_hbm.at[pl.ds(rows_ref[0, 0, r], 1), :],
                                  ssem.at[s]).start()

    def wait_block(buf, sem, s):
        pltpu.make_async_copy(buf.at[s, pl.ds(0, tm)], buf.at[s, pl.ds(0, tm)], sem.at[s]).wait()

    @pl.when(i == 0)
    def _():
        xbuf[...] = jnp.zeros_like(xbuf)
        obuf[...] = jnp.zeros_like(obuf)
        spare = [pltpu.make_async_copy(obuf.at[s], y_hbm.at[pl.ds(2 * n_tok + s * tm, tm), :], ssem.at[s])
                 for s in range(2)]
        for cp in spare:
            cp.start()
        for cp in spare:
            cp.wait()
        start_gathers(src_ref, 0)

    def step(slot):
        wait_block(xbuf, gsem, slot)

        @pl.when(i >= 1)
        def _():
            wait_block(obuf, ssem, slot)

        n_a, n_b = 2 * D_EXPERT // nc, D_MODEL // nc
        t_a, t_b = D_MODEL // nc, D_EXPERT // nc
        first = 8
        unit = (2 * tm - first) / (n_a * t_a + n_b * t_b)
        quota = [first] + [round(unit * t_a)] * n_a + [round(unit * t_b)] * (n_b - 1)
        quota.append(2 * tm - sum(quota))
        issued = [0]

        def issue(lhs_ref):
            lo, hi = issued[0], issued[0] + quota.pop(0)
            issued[0] = hi
            start_scatters(dst_prev_ref, 1 - slot, (lo + 1) // 2, (hi + 1) // 2)
            start_gathers(src_next_ref, 1 - slot, lo // 2, hi // 2)
            if lhs_ref is not None:
                zero = xbuf[1 - slot, tm:tm + 8, 0:LANES]
                zero = jnp.concatenate([zero, zero], axis=0)
                lhs_ref[0:16, 0:LANES] = (lhs_ref[0:16, 0:LANES].astype(F32) + zero).astype(BF16)

        x16_ref[...] = xbuf[slot, 0:tm].astype(BF16)
        for k in range(D_EXPERT // nc):
            issue(x16_ref)
            a = jnp.dot(x16_ref[...], wi16_ref[:, k * nc:(k + 1) * nc], preferred_element_type=F32)
            issue(x16_ref)
            b = jnp.dot(x16_ref[...], wi16_ref[:, D_EXPERT + k * nc:D_EXPERT + (k + 1) * nc],
                        preferred_element_type=F32)
            hmid_ref[:, k * nc:(k + 1) * nc] = (a * _sigmoid(a) * b).astype(BF16)
        for k in range(n_b):
            issue(hmid_ref)
            obuf[slot, :, k * nc:(k + 1) * nc] = jnp.dot(hmid_ref[...], wo16_ref[:, k * nc:(k + 1) * nc],
                                                         preferred_element_type=F32)
        issue(None)
        assert not quota and issued[0] == 2 * tm

        @pl.when(i == n_used - 1)
        def _():
            start_scatters(dst_ref, slot)
            wait_block(obuf, ssem, 1 - slot)
            wait_block(obuf, ssem, slot)
            wait_block(xbuf, gsem, 1 - slot)

    @pl.when((i < n_used) & ((i == 0) | (blk_e_ref[i] != blk_e_ref[jnp.maximum(i - 1, 0)])))
    def _():
        wi16_ref[...] = wi_ref[0].astype(BF16)
        wo16_ref[...] = wo_ref[0].astype(BF16)

    for parity in range(2):
        pl.when((i < n_used) & (i % 2 == parity))(functools.partial(step, parity))


def moe_sorted(xn, w_in, w_out, tables, *, tm, layer):
    blk_e, n_used, src, dst = tables
    n_tok = xn.shape[0]
    n_blocks = blk_e.shape[0]
    smem = lambda index_map: pl.BlockSpec((1, 1, tm), index_map, memory_space=pltpu.SMEM)
    nxt = lambda i, be, nu: (jnp.maximum(jnp.minimum(i + 1, nu[0] - 1), 0), 0, 0)
    grid_spec = pltpu.PrefetchScalarGridSpec(
        num_scalar_prefetch=2, grid=(n_blocks,),
        in_specs=[smem(lambda i, be, nu: (i, 0, 0)), smem(nxt),
                  smem(lambda i, be, nu: (i, 0, 0)), smem(lambda i, be, nu: (i + 1, 0, 0)),
                  pl.BlockSpec(memory_space=pl.ANY),
                  pl.BlockSpec((None, 1, D_MODEL, 2 * D_EXPERT), lambda i, be, nu: (layer, be[i], 0, 0)),
                  pl.BlockSpec((None, 1, D_EXPERT, D_MODEL), lambda i, be, nu: (layer, be[i], 0, 0))],
        out_specs=pl.BlockSpec(memory_space=pl.ANY),
        scratch_shapes=[pltpu.VMEM((2, tm + 8, D_MODEL), F32), pltpu.VMEM((2, tm, D_MODEL), F32),
                        pltpu.VMEM((tm, D_MODEL), BF16), pltpu.VMEM((tm, D_EXPERT), BF16),
                        pltpu.VMEM((D_MODEL, 2 * D_EXPERT), BF16), pltpu.VMEM((D_EXPERT, D_MODEL), BF16),
                        pltpu.SemaphoreType.DMA((2,)), pltpu.SemaphoreType.DMA((2,))])
    return pl.pallas_call(
        functools.partial(_moe_sorted_kernel, tm=tm, n_tok=n_tok), grid_spec=grid_spec,
        out_shape=jax.ShapeDtypeStruct((2 * n_tok + 2 * tm, D_MODEL), F32),
        compiler_params=_cparams(("arbitrary",)), name="moe_sorted")(
            blk_e, n_used, src, src, dst, dst, xn, w_in, w_out)


def route_tables(eid, *, tm):
    n_tok = eid.shape[0]
    n_asg = 2 * n_tok
    flat = eid.reshape(-1)
    skey = jnp.sort(flat * n_asg + jnp.arange(n_asg, dtype=I32))
    order = skey % n_asg
    counts = jnp.sum((flat[:, None] == jnp.arange(N_EXPERTS, dtype=I32)[None, :]).astype(I32), axis=0)
    starts = jnp.cumsum(counts) - counts
    padded = (counts + tm - 1) // tm * tm
    pend = jnp.cumsum(padded)
    pstarts = pend - padded
    n_blocks = -(-(n_asg + N_EXPERTS * (tm - 1)) // tm)
    blk = jnp.arange(n_blocks, dtype=I32)
    ends = jnp.zeros((n_blocks + 1,), I32).at[jnp.minimum(pend // tm, n_blocks)].add(1)
    blk_e = jnp.minimum(jnp.cumsum(ends)[:n_blocks], N_EXPERTS - 1)
    blk_off = blk * tm - pstarts[blk_e]
    blk_n = jnp.clip(counts[blk_e] - blk_off, 0, tm)
    r = jnp.arange(tm, dtype=I32)[None, :]
    valid = r < blk_n[:, None]
    code = order[jnp.clip((starts[blk_e] + blk_off)[:, None] + r, 0, n_asg - 1)]
    src = jnp.where(valid, code >> 1, 0)
    spare = 2 * n_tok + (blk % 2)[:, None] * tm + r
    dst = jnp.where(valid, (code & 1) * n_tok + (code >> 1), spare)
    dst = jnp.concatenate([2 * n_tok + tm + r, dst], axis=0)
    n_used = (pend[-1] // tm).reshape(1)
    return (blk_e.astype(I32), n_used.astype(I32), src.reshape(n_blocks, 1, tm).astype(I32),
            dst.reshape(n_blocks + 1, 1, tm).astype(I32))


def _q_head_order():
    order = []
    for pr in range(N_KV_HEADS // 2):
        for r in range(Q_PER_KV):
            for half in range(2):
                order.append((2 * pr + half) * Q_PER_KV + r)
    return jnp.asarray(order, dtype=I32)


def _pad_lanes(a):
    return jnp.pad(a, [(0, 0)] * (a.ndim - 1) + [(0, LANES - a.shape[-1])])


def _ssd_params(conv_w, conv_b, dt_bias, a_log, d_skip, norm_w):
    head_of_col = jnp.arange(GROUP_W, dtype=I32) // SSD_HEAD_DIM
    row = jnp.arange(32, dtype=I32)
    expand = ((row[:, None] % 8 == head_of_col[None, :]) & (row[:, None] < 24)).astype(BF16)
    return {
        "conv_w": jnp.pad(conv_w, ((0, 8 - SSD_CONV), (0, 0))),
        "conv_b": conv_b.reshape(1, CONV_DIM),
        "dt_bias": _pad_lanes(dt_bias.reshape(1, SSD_HEADS)),
        "a_log": _pad_lanes(a_log.reshape(1, SSD_HEADS)),
        "d_skip": jnp.repeat(d_skip, SSD_HEAD_DIM).reshape(1, D_INNER),
        "norm_w": norm_w.reshape(1, D_INNER),
        "expand": expand,
    }


def _router_params(w_group, b_group, w_router, b_router):
    wr = _pad_lanes(jnp.concatenate([w_group, w_router], axis=1)).astype(BF16)
    br = _pad_lanes(jnp.concatenate([b_group, b_router]).reshape(1, -1)).astype(F32)
    return wr, br


def _pick_tile(n, prefs):
    for t in prefs:
        if n % t == 0:
            return t
    return n


def kernel(x_prompt, x_sample, cache_ssm_state, cache_ssm_conv, cache_attn_k, cache_attn_v, cache_meta_k, cache_meta_v, meta_tokens, norm_mix, norm_ffn, norm_final, ssd_w_in, ssd_conv_w, ssd_conv_b, ssd_dt_bias, ssd_a_log, ssd_d, ssd_norm, ssd_w_out, attn_w_qkv, attn_sinks, attn_w_out, moe_w_group, moe_b_group, moe_w_router, moe_b_router, moe_w_in, moe_w_out):
    n_batch, seq, _ = x_prompt.shape
    dec_batch, dec_seq, _ = x_sample.shape
    n_real = n_batch * seq
    n_small = N_META + dec_batch * dec_seq
    blk = SSD_BLOCK
    assert seq % blk == 0 and dec_seq <= blk and N_META <= blk
    tm_mid = _pick_tile(n_real, (512, 256))
    tm_res = _pick_tile(n_real, (256,))

    h_real = x_prompt.reshape(n_real, D_MODEL)
    h_small = jnp.concatenate([meta_tokens, x_sample.reshape(-1, D_MODEL)], axis=0)

    w_in16 = ssd_w_in[0].astype(BF16)
    w_zx, w_dt = w_in16[:, :ZX_DIM], _pad_lanes(w_in16[:, ZX_DIM:])
    sp = _ssd_params(ssd_conv_w[0], ssd_conv_b[0], ssd_dt_bias[0], ssd_a_log[0], ssd_d[0], ssd_norm[0])
    w_out16 = ssd_w_out[0].astype(BF16)
    wr0, br0 = _router_params(moe_w_group[0], moe_b_group[0], moe_w_router[0], moe_b_router[0])

    zx_s, dt_s = norm_matmul(h_small, norm_mix[0], w_zx, tm=n_small, tn=512, w_extra=w_dt)

    n_sb = 1 + dec_batch

    def to_blocks(a, n_valid_meta=N_META):
        meta = jnp.pad(a[:N_META], ((0, blk - N_META), (0, 0)))[None]
        samp = jnp.pad(a[N_META:].reshape(dec_batch, dec_seq, -1), ((0, 0), (0, blk - dec_seq), (0, 0)))
        return jnp.concatenate([meta, samp], axis=0).reshape(n_sb * blk, -1)

    assert dec_seq == N_META, "small-path blocks share one valid length"
    s0_small = jnp.concatenate([jnp.zeros((1, D_INNER, SSD_STATE), F32),
                                cache_ssm_state[0].reshape(dec_batch, D_INNER, SSD_STATE)], axis=0)
    c0_small = jnp.pad(jnp.concatenate([jnp.zeros((1, SSD_CONV - 1, CONV_DIM), F32), cache_ssm_conv[0]], axis=0),
                       ((0, 0), (8 - (SSD_CONV - 1), 0), (0, 0)))
    yn_sb, state_small = ssd_scan(to_blocks(zx_s), to_blocks(dt_s), sp, s0_small, c0_small,
                                  n_batch=n_sb, n_chunks=1, valid_len=dec_seq, shared_init=False)
    yn_sb = yn_sb.reshape(n_sb, blk, D_INNER)
    yn_small = jnp.concatenate([yn_sb[0, :N_META], yn_sb[1:, :dec_seq].reshape(-1, D_INNER)], axis=0)

    c0_real = jnp.pad(zx_s[N_META - (SSD_CONV - 1):N_META, D_INNER:], ((8 - (SSD_CONV - 1), 0), (0, 0)))
    tm_in = _pick_tile(seq, (1024, 512, 256, 128))
    zx_r, dt_r, raw_tail = ssd_in_proj(h_real, norm_mix[0], w_zx, w_dt, sp, c0_real, tm=tm_in, tn=1024, seq=seq)
    yn_real, state_real = ssd_scan(zx_r, dt_r, sp, state_small[0:1], c0_real[None],
                                   n_batch=n_batch, n_chunks=seq // blk, valid_len=blk, shared_init=True,
                                   conv_done=True)
    conv_real = raw_tail.reshape(n_batch, seq // tm_in, 8, CONV_DIM)[:, -1, 8 - (SSD_CONV - 1):]

    tk0 = 2048
    h1_s, xn_s, eid_s, wts_s = matmul_route(yn_small, w_out16, h_small, norm_ffn[0], wr0, br0, tm=n_small, tk=tk0)
    h1_r, xn_r, eid_r, wts_r = matmul_route(yn_real, w_out16, h_real, norm_ffn[0], wr0, br0, tm=tm_res,
                                            tk=D_INNER)

    moe_s = moe_dense(xn_s, eid_s, wts_s, moe_w_in, moe_w_out, layer=0)
    moe_r = moe_sorted(xn_r, moe_w_in, moe_w_out, route_tables(eid_r[:, :2], tm=MOE_TM), tm=MOE_TM, layer=0)

    head_order = _q_head_order()
    wq = attn_w_qkv[0][:, :Q_DIM].reshape(D_MODEL, N_HEADS, ATTN_HEAD_DIM)[:, head_order].reshape(D_MODEL, Q_DIM)
    w_qkv16 = jnp.concatenate([wq, attn_w_qkv[0][:, Q_DIM:]], axis=1).astype(BF16)
    sinks = attn_sinks[0][head_order].astype(F32)
    w_ao16 = attn_w_out[0].reshape(N_HEADS, ATTN_HEAD_DIM, D_MODEL)[head_order].reshape(Q_DIM, D_MODEL).astype(BF16)
    wr1, br1 = _router_params(moe_w_group[1], moe_b_group[1], moe_w_router[1], moe_b_router[1])

    qkv_s, h2_s = norm_matmul(h1_s, norm_mix[1], w_qkv16, tm=n_small, tn=512, mode="sum", moe=(moe_s,),
                              write_h=True)
    qkv_r, h2_r = norm_matmul(h1_r, norm_mix[1], w_qkv16, tm=tm_res, tn=Q_DIM + 2 * KV_DIM, mode="pair",
                              moe=(moe_r, wts_r), write_h=True)

    k_meta, v_meta = qkv_s[:N_META, Q_DIM:Q_DIM + KV_DIM], qkv_s[:N_META, Q_DIM + KV_DIM:]
    k_new = qkv_s[N_META:, Q_DIM:Q_DIM + KV_DIM].reshape(dec_batch, dec_seq, KV_DIM)
    v_new = qkv_s[N_META:, Q_DIM + KV_DIM:].reshape(dec_batch, dec_seq, KV_DIM)
    o_meta = attn_full(qkv_s[None, :N_META, :Q_DIM], k_meta[None], v_meta[None], sinks)
    kk_s = jnp.concatenate([cache_meta_k[0].reshape(dec_batch, N_META, KV_DIM),
                            cache_attn_k[0].reshape(dec_batch, WINDOW, KV_DIM), k_new], axis=1)
    vv_s = jnp.concatenate([cache_meta_v[0].reshape(dec_batch, N_META, KV_DIM),
                            cache_attn_v[0].reshape(dec_batch, WINDOW, KV_DIM), v_new], axis=1)
    o_samp = attn_full(qkv_s[N_META:, :Q_DIM].reshape(dec_batch, dec_seq, Q_DIM), kk_s, vv_s, sinks)
    o_small = jnp.concatenate([o_meta.reshape(N_META, Q_DIM), o_samp.reshape(-1, Q_DIM)], axis=0)
    o_real = attn_band(qkv_r, k_meta, v_meta, sinks, n_batch=n_batch, seq=seq)

    h3_s, xn_s, eid_s, wts_s = matmul_route(o_small, w_ao16, h2_s, norm_ffn[1], wr1, br1, tm=n_small, tk=1024)
    h3_r, xn_r, eid_r, wts_r = matmul_route(o_real, w_ao16, h2_r, norm_ffn[1], wr1, br1, tm=tm_mid, tk=2048)

    moe_s = moe_dense(xn_s, eid_s, wts_s, moe_w_in, moe_w_out, layer=1)
    moe_r = moe_sorted(xn_r, moe_w_in, moe_w_out, route_tables(eid_r[:, :2], tm=MOE_TM), tm=MOE_TM, layer=1)

    y_small = final_norm(h3_s, norm_final, tm=n_small, mode="sum", moe=(moe_s,))
    y_real = final_norm(h3_r, norm_final, tm=tm_mid, mode="pair", moe=(moe_r, wts_r))

    kv_shape = (N_KV_HEADS, ATTN_HEAD_DIM)
    qkv_r3 = qkv_r.reshape(n_batch, seq, -1)
    zx_samp = zx_s[N_META:].reshape(dec_batch, dec_seq, ZX_DIM)
    return (
        y_real.reshape(n_batch, seq, D_MODEL),
        y_small[N_META:].reshape(dec_batch, dec_seq, D_MODEL),
        state_real.reshape(1, n_batch, SSD_HEADS, SSD_HEAD_DIM, SSD_STATE),
        conv_real[None],
        qkv_r3[:, seq - WINDOW:, Q_DIM:Q_DIM + KV_DIM].reshape((1, n_batch, WINDOW) + kv_shape),
        qkv_r3[:, seq - WINDOW:, Q_DIM + KV_DIM:].reshape((1, n_batch, WINDOW) + kv_shape),
        jnp.broadcast_to(k_meta.reshape((1, 1, N_META) + kv_shape), (1, n_batch, N_META) + kv_shape),
        jnp.broadcast_to(v_meta.reshape((1, 1, N_META) + kv_shape), (1, n_batch, N_META) + kv_shape),
        state_small[1:].reshape(1, dec_batch, SSD_HEADS, SSD_HEAD_DIM, SSD_STATE),
        zx_samp[None, :, dec_seq - (SSD_CONV - 1):, D_INNER:],
        k_new.reshape((1, dec_batch, dec_seq) + kv_shape),
        v_new.reshape((1, dec_batch, dec_seq) + kv_shape),
    )
```

```python
import functools

import jax
import jax.numpy as jnp
from jax import lax
from jax.experimental import pallas as pl
from jax.experimental.pallas import tpu as pltpu

F32, BF16, I32 = jnp.float32, jnp.bfloat16, jnp.int32

D_MODEL = 2048
N_META = 16
EPS = 1e-5
D_INNER = 4096
SSD_HEAD_DIM = 64
SSD_HEADS = 64
SSD_GROUPS = 8
SSD_STATE = 128
SSD_CONV = 4
GN = SSD_GROUPS * SSD_STATE
CONV_DIM = D_INNER + 2 * GN
ZX_DIM = D_INNER + CONV_DIM
GROUP_W = D_INNER // SSD_GROUPS
WINDOW = 128
CHUNK = 64
ATTN_HEAD_DIM = 64
N_HEADS = 32
N_KV_HEADS = 8
Q_PER_KV = N_HEADS // N_KV_HEADS
Q_DIM = N_HEADS * ATTN_HEAD_DIM
KV_DIM = N_KV_HEADS * ATTN_HEAD_DIM
N_GROUPS = 4
EXPERTS_PER_GROUP = 8
N_EXPERTS = 32
D_EXPERT = 512

LANES = 128
SSD_BLOCK = 128
NEG = -1e30
VMEM_LIMIT = 56 * 1024 * 1024
MOE_TM = 256
MOE_NC = 256
MOE_SLOTS = 3
IN_NC = 256
ATTN_CHUNKS = 4
ATTN_LAG = ATTN_CHUNKS * N_HEADS


def _cparams(sem):
    return pltpu.CompilerParams(dimension_semantics=sem, vmem_limit_bytes=VMEM_LIMIT)


def _resident_spec(block_shape, index_map, *, resident):
    if resident:
        return pl.BlockSpec(block_shape, index_map, pipeline_mode=pl.Buffered(1))
    return pl.BlockSpec(block_shape, index_map)


def _sigmoid(x):
    return 1.0 / (1.0 + jnp.exp(-x))


def _rms(h, g):
    var = jnp.mean(h * h, axis=-1, keepdims=True)
    return h * lax.rsqrt(var + EPS) * g


def _row_chunk(tm):
    for rc in (256, 128, 64, 32, 16, 8):
        if tm % rc == 0:
            return rc
    raise ValueError(tm)


_N_MOE_REFS = {"none": 0, "sum": 1, "pair": 3}


def _combine_rows(x_ref, moe_refs, mode, rows):
    h = x_ref[rows, :]
    if mode == "sum":
        h = h + moe_refs[0][rows, :]
    elif mode == "pair":
        y0, y1, w = moe_refs
        wv = w[rows, :]
        h = h + (y0[rows, :] * wv[:, 0:1] + y1[rows, :] * wv[:, 1:2])
    return h


def _norm_mm_kernel(*refs, mode, write_h, has_extra, tm):
    it = iter(refs)
    x_ref = next(it)
    moe_refs = [next(it) for _ in range(_N_MOE_REFS[mode])]
    g_ref, w_ref = next(it), next(it)
    we_ref = next(it) if has_extra else None
    o_ref = next(it)
    h_ref = next(it) if write_h else None
    e_ref = next(it) if has_extra else None
    xn_ref = next(it)
    rc = _row_chunk(tm)

    @pl.when(pl.program_id(1) == 0)
    def _():
        def body(r, carry):
            rows = pl.ds(pl.multiple_of(r * rc, rc), rc)
            h = _combine_rows(x_ref, moe_refs, mode, rows)
            if write_h:
                h_ref[rows, :] = h
            xn_ref[rows, :] = _rms(h, g_ref[...]).astype(BF16)
            return carry

        lax.fori_loop(0, tm // rc, body, 0)
        if has_extra:
            e_ref[...] = jnp.dot(xn_ref[...], we_ref[...], preferred_element_type=F32)

    o_ref[...] = jnp.dot(xn_ref[...], w_ref[...], preferred_element_type=F32)


def _moe_in_specs(mode, moe, tm, n_rows):
    if mode == "none":
        return [], []
    if mode == "sum":
        return [pl.BlockSpec((tm, D_MODEL), lambda i, j: (i, 0))], [moe[0]]
    y2, wts = moe
    off = n_rows // tm
    specs = [pl.BlockSpec((tm, D_MODEL), lambda i, j: (i, 0)),
             pl.BlockSpec((tm, D_MODEL), lambda i, j: (i + off, 0)),
             pl.BlockSpec((tm, LANES), lambda i, j: (i, 0))]
    return specs, [y2, y2, wts]


def norm_matmul(x, g, w, *, tm, tn, mode="none", moe=(), w_extra=None, write_h=False):
    n_rows, n_out = x.shape[0], w.shape[1]
    assert n_rows % tm == 0 and n_out % tn == 0
    moe_specs, moe_ops = _moe_in_specs(mode, moe, tm, n_rows)
    in_specs = [pl.BlockSpec((tm, D_MODEL), lambda i, j: (i, 0))] + moe_specs + [
        pl.BlockSpec((1, D_MODEL), lambda i, j: (0, 0)),
        _resident_spec((D_MODEL, tn), lambda i, j: (0, j), resident=n_out == tn)]
    operands = [x] + moe_ops + [g.reshape(1, D_MODEL), w]
    out_shape = [jax.ShapeDtypeStruct((n_rows, n_out), F32)]
    out_specs = [pl.BlockSpec((tm, tn), lambda i, j: (i, j))]
    if w_extra is not None:
        in_specs.append(pl.BlockSpec((D_MODEL, LANES), lambda i, j: (0, 0)))
        operands.append(w_extra)
    if write_h:
        out_shape.append(jax.ShapeDtypeStruct((n_rows, D_MODEL), F32))
        out_specs.append(pl.BlockSpec((tm, D_MODEL), lambda i, j: (i, 0)))
    if w_extra is not None:
        out_shape.append(jax.ShapeDtypeStruct((n_rows, LANES), F32))
        out_specs.append(pl.BlockSpec((tm, LANES), lambda i, j: (i, 0)))
    kern = functools.partial(_norm_mm_kernel, mode=mode, write_h=write_h,
                             has_extra=w_extra is not None, tm=tm)
    return pl.pallas_call(
        kern, grid=(n_rows // tm, n_out // tn), in_specs=in_specs, out_specs=out_specs,
        out_shape=out_shape, scratch_shapes=[pltpu.VMEM((tm, D_MODEL), BF16)],
        compiler_params=_cparams(("parallel", "arbitrary")), name="norm_matmul")(*operands)


def _final_norm_kernel(*refs, mode, tm):
    x_ref = refs[0]
    moe_refs = refs[1:1 + _N_MOE_REFS[mode]]
    g_ref, o_ref = refs[-2], refs[-1]
    rc = _row_chunk(tm)

    def body(r, carry):
        rows = pl.ds(pl.multiple_of(r * rc, rc), rc)
        o_ref[rows, :] = _rms(_combine_rows(x_ref, moe_refs, mode, rows), g_ref[...])
        return carry

    lax.fori_loop(0, tm // rc, body, 0)


def final_norm(x, g, *, tm, mode, moe):
    n_rows = x.shape[0]
    moe_specs, moe_ops = _moe_in_specs(mode, moe, tm, n_rows)
    in_specs = [pl.BlockSpec((tm, D_MODEL), lambda i, j: (i, 0))] + moe_specs + [
        pl.BlockSpec((1, D_MODEL), lambda i, j: (0, 0))]
    return pl.pallas_call(
        functools.partial(_final_norm_kernel, mode=mode, tm=tm), grid=(n_rows // tm, 1),
        in_specs=in_specs, out_specs=pl.BlockSpec((tm, D_MODEL), lambda i, j: (i, 0)),
        out_shape=jax.ShapeDtypeStruct((n_rows, D_MODEL), F32),
        compiler_params=_cparams(("parallel", "arbitrary")), name="final_norm")(
            x, *moe_ops, g.reshape(1, D_MODEL))


def _in_proj_kernel(x_ref, g_ref, w_ref, wdt_ref, cw_ref, cb_ref, c0_ref, o_ref, dt_ref, rt_ref,
                    xn_ref, tail_ref, stage_ref, *, tm, tn, tiles_per_batch, n_z):
    i, j = pl.program_id(0), pl.program_id(1)
    rc = _row_chunk(tm)

    @pl.when(j == 0)
    def _():
        def body(r, carry):
            rows = pl.ds(pl.multiple_of(r * rc, rc), rc)
            xn_ref[rows, :] = _rms(x_ref[rows, :], g_ref[...]).astype(BF16)
            return carry

        lax.fori_loop(0, tm // rc, body, 0)
        dt_ref[...] = jnp.dot(xn_ref[...], wdt_ref[...], preferred_element_type=F32)

    def chunks():
        for c in range(tn // IN_NC):
            cs = slice(c * IN_NC, (c + 1) * IN_NC)
            yield c, cs, jnp.dot(xn_ref[...], w_ref[:, cs], preferred_element_type=F32)

    @pl.when(j < n_z)
    def _():
        for _, cs, raw in chunks():
            o_ref[:, cs] = raw * _sigmoid(raw)

    @pl.when(j >= n_z)
    def _():
        e = j - n_z

        @pl.when(i % tiles_per_batch == 0)
        def _():
            tail_ref[e] = c0_ref[...]

        for c, cs, raw in chunks():
            stage = stage_ref.at[c % 2]
            stage[0:8, :] = tail_ref[e, :, cs]
            stage[8:8 + tm, :] = raw
            acc = cb_ref[:, cs]
            for k in range(SSD_CONV):
                acc = acc + stage[pl.ds(8 - (SSD_CONV - 1) + k, tm), :] * cw_ref[k:k + 1, cs]
            o_ref[:, cs] = acc * _sigmoid(acc)
            last = stage[tm:tm + 8, :]
            tail_ref[e, :, cs] = last
            rt_ref[0, :, cs] = last


def ssd_in_proj(x, g, w, w_dt, p, c0, *, tm, tn, seq):
    n_rows = x.shape[0]
    assert n_rows % tm == 0 and seq % tm == 0 and D_INNER % tn == 0 and CONV_DIM % tn == 0
    n_z = D_INNER // tn
    conv_col = lambda i, j: (0, jnp.maximum(j - n_z, 0))
    kern = functools.partial(_in_proj_kernel, tm=tm, tn=tn, tiles_per_batch=seq // tm, n_z=n_z)
    return pl.pallas_call(
        kern, grid=(n_rows // tm, ZX_DIM // tn),
        in_specs=[pl.BlockSpec((tm, D_MODEL), lambda i, j: (i, 0)),
                  pl.BlockSpec((1, D_MODEL), lambda i, j: (0, 0)),
                  pl.BlockSpec((D_MODEL, tn), lambda i, j: (0, j)),
                  pl.BlockSpec((D_MODEL, LANES), lambda i, j: (0, 0)),
                  pl.BlockSpec((8, tn), conv_col),
                  pl.BlockSpec((1, tn), conv_col),
                  pl.BlockSpec((8, tn), conv_col)],
        out_specs=[pl.BlockSpec((tm, tn), lambda i, j: (i, j)),
                   pl.BlockSpec((tm, LANES), lambda i, j: (i, 0)),
                   pl.BlockSpec((1, 8, tn), lambda i, j: (i, 0, jnp.maximum(j - n_z, 0)))],
        out_shape=[jax.ShapeDtypeStruct((n_rows, ZX_DIM), F32),
                   jax.ShapeDtypeStruct((n_rows, LANES), F32),
                   jax.ShapeDtypeStruct((n_rows // tm, 8, CONV_DIM), F32)],
        scratch_shapes=[pltpu.VMEM((tm, D_MODEL), BF16),
                        pltpu.VMEM((CONV_DIM // tn, 8, tn), F32),
                        pltpu.VMEM((2, tm + 8, IN_NC), F32)],
        compiler_params=_cparams(("arbitrary", "arbitrary")), name="ssd_in_proj")(
            x, g.reshape(1, D_MODEL), w, w_dt, p["conv_w"], p["conv_b"], c0)


def _route(lg):
    lane = lax.broadcasted_iota(I32, lg.shape, 1)
    is_g = lane < N_GROUPS
    gl = jnp.where(is_g, lg, NEG)
    gmax = jnp.max(gl, axis=-1, keepdims=True)
    gsum = jnp.sum(jnp.where(is_g, jnp.exp(gl - gmax), 0.0), axis=-1, keepdims=True)
    gate = 1.0 / gsum
    gidx = jnp.min(jnp.where(gl == gmax, lane, LANES), axis=-1, keepdims=True)
    ecol = lane - N_GROUPS
    in_group = (ecol >= 0) & (ecol < N_EXPERTS) & ((ecol >> 3) == gidx)
    el = jnp.where(in_group, lg, NEG)
    v1 = jnp.max(el, axis=-1, keepdims=True)
    i1 = jnp.min(jnp.where(el == v1, lane, LANES), axis=-1, keepdims=True)
    el2 = jnp.where(lane == i1, NEG, el)
    v2 = jnp.max(el2, axis=-1, keepdims=True)
    i2 = jnp.min(jnp.where(el2 == v2, lane, LANES), axis=-1, keepdims=True)
    e2 = jnp.exp(v2 - v1)
    w1 = gate / (1.0 + e2)
    w2 = gate * e2 / (1.0 + e2)
    eid = jnp.where(lane == 0, i1 - N_GROUPS, jnp.where(lane == 1, i2 - N_GROUPS, 0))
    wts = jnp.where(lane == 0, w1, jnp.where(lane == 1, w2, 0.0))
    return eid, wts


def _mm_route_kernel(a_ref, w_ref, h_ref, g_ref, wr_ref, br_ref,
                     h1_ref, xn_ref, e_ref, p_ref, acc_ref, *, nk, tm):
    k = pl.program_id(1)
    rc = _row_chunk(tm)

    @pl.when(k == 0)
    def _():
        acc_ref[...] = jnp.zeros_like(acc_ref)

    acc_ref[...] += jnp.dot(a_ref[...], w_ref[...], preferred_element_type=F32)

    @pl.when(k == nk - 1)
    def _():
        def body(r, carry):
            rows = pl.ds(pl.multiple_of(r * rc, rc), rc)
            h1 = h_ref[rows, :] + acc_ref[rows, :]
            h1_ref[rows, :] = h1
            xn = _rms(h1, g_ref[...])
            xn_ref[rows, :] = xn
            lg = jnp.dot(xn.astype(BF16), wr_ref[...], preferred_element_type=F32) + br_ref[...]
            eid, wts = _route(lg)
            e_ref[rows, :] = eid
            p_ref[rows, :] = wts
            return carry

        lax.fori_loop(0, tm // rc, body, 0)


def matmul_route(a, w, h, g, wr, br, *, tm, tk):
    n_rows, kdim = a.shape
    assert n_rows % tm == 0 and kdim % tk == 0
    nk = kdim // tk
    row_spec = pl.BlockSpec((tm, D_MODEL), lambda i, k: (i, 0))
    lane_spec = pl.BlockSpec((tm, LANES), lambda i, k: (i, 0))
    return pl.pallas_call(
        functools.partial(_mm_route_kernel, nk=nk, tm=tm), grid=(n_rows // tm, nk),
        in_specs=[pl.BlockSpec((tm, tk), lambda i, k: (i, k)),
                  _resident_spec((tk, D_MODEL), lambda i, k: (k, 0), resident=nk == 1),
                  row_spec,
                  pl.BlockSpec((1, D_MODEL), lambda i, k: (0, 0)),
                  pl.BlockSpec((D_MODEL, LANES), lambda i, k: (0, 0)),
                  pl.BlockSpec((1, LANES), lambda i, k: (0, 0))],
        out_specs=[row_spec, row_spec, lane_spec, lane_spec],
        out_shape=[jax.ShapeDtypeStruct((n_rows, D_MODEL), F32),
                   jax.ShapeDtypeStruct((n_rows, D_MODEL), F32),
                   jax.ShapeDtypeStruct((n_rows, LANES), I32),
                   jax.ShapeDtypeStruct((n_rows, LANES), F32)],
        scratch_shapes=[pltpu.VMEM((tm, D_MODEL), F32)],
        compiler_params=_cparams(("parallel", "arbitrary")), name="matmul_route")(
            a, w, h, g.reshape(1, D_MODEL), wr, br)


def _expand_heads(v, eg):
    v1 = v.astype(BF16).astype(F32)
    r1 = v - v1
    v2 = r1.astype(BF16).astype(F32)
    v3 = (r1 - v2).astype(BF16).astype(F32)
    parts = jnp.concatenate([v1, v2, v3, jnp.zeros_like(v)], axis=0).astype(BF16)
    return lax.dot_general(parts, eg, (((0,), (0,)), ((), ())), preferred_element_type=F32)


def _ssd_kernel(z_ref, x_ref, b_ref, c_ref, dt_ref, cw_ref, cb_ref, dtb_ref, alog_ref, dsk_ref,
                nw_ref, eg_ref, s0_ref, c0_ref, y_ref, so_ref,
                state_ref, xp_ref, cumt_ref, dtt_ref, *, valid_len, n_chunks, conv_done):
    blk = SSD_BLOCK
    c = pl.program_id(1)

    @pl.when(c == 0)
    def _():
        state_ref[...] = s0_ref[0].T
        if not conv_done:
            xp_ref[0:8, :] = c0_ref[0]

    if not conv_done:
        @pl.when(c > 0)
        def _():
            xp_ref[0:8, :] = xp_ref[blk:blk + 8, :]

        xp_ref[8:8 + blk, 0:D_INNER] = x_ref[...]
        xp_ref[8:8 + blk, D_INNER:D_INNER + GN] = b_ref[...]
        xp_ref[8:8 + blk, D_INNER + GN:CONV_DIM] = c_ref[...]

    row = lax.broadcasted_iota(I32, (blk, LANES), 0)
    lane = lax.broadcasted_iota(I32, (blk, LANES), 1)
    dtr = dt_ref[...] + dtb_ref[...]
    dt = jnp.maximum(dtr, 0.0) + jnp.log1p(jnp.exp(-jnp.abs(dtr)))
    if valid_len < blk:
        dt = jnp.where(row < valid_len, dt, 0.0)
    cum = dt * (-jnp.exp(alog_ref[...]))
    sh = 1
    while sh < blk:
        cum = cum + jnp.where(row >= sh, pltpu.roll(cum, sh, 0), 0.0)
        sh *= 2
    cumt_ref[...] = cum.T
    dtt_ref[...] = dt.T

    causal = lane <= row
    left = lane < SSD_HEAD_DIM

    def conv(cols):
        acc = cb_ref[:, cols]
        for k in range(SSD_CONV):
            acc = acc + xp_ref[pl.ds(8 - (SSD_CONV - 1) + k, blk), cols] * cw_ref[k:k + 1, cols]
        return acc * _sigmoid(acc)

    def group(g, carry):
        cols = pl.ds(pl.multiple_of(g * GROUP_W, GROUP_W), GROUP_W)
        if conv_done:
            ncols = pl.ds(pl.multiple_of(g * SSD_STATE, SSD_STATE), SSD_STATE)
            xg, bg, cg = x_ref[:, cols], b_ref[:, ncols], c_ref[:, ncols]
        else:
            xg = conv(cols)
            bg = conv(pl.ds(pl.multiple_of(D_INNER + g * SSD_STATE, SSD_STATE), SSD_STATE))
            cg = conv(pl.ds(pl.multiple_of(D_INNER + GN + g * SSD_STATE, SSD_STATE), SSD_STATE))
        cb16, bb16 = cg.astype(BF16), bg.astype(BF16)
        cbm = lax.dot_general(cb16, bb16, (((1,), (1,)), ((), ())), preferred_element_type=F32)
        sg = state_ref[:, cols]
        heads = pl.ds(pl.multiple_of(g * (GROUP_W // SSD_HEAD_DIM), 8), GROUP_W // SSD_HEAD_DIM)
        ecg = _expand_heads(cumt_ref[heads, :], eg_ref[...])
        edg = _expand_heads(dtt_ref[heads, :], eg_ref[...])
        y_off =jnp.dot(cb16, sg.astype(BF16), preferred_element_type=F32) * jnp.exp(ecg)
        parts = []
        for pr in range(GROUP_W // LANES):
            xpair = xg[:, pr * LANES:(pr + 1) * LANES]
            ec = ecg[:, pr * LANES:(pr + 1) * LANES]
            ec_sw = pltpu.roll(ec, SSD_HEAD_DIM, 1)
            acc = jnp.zeros((blk, LANES), F32)
            for q in range(2):
                head = g * (GROUP_W // SSD_HEAD_DIM) + pr * 2 + q
                cum_l = jnp.where(left, ec, ec_sw) if q == 0 else jnp.where(left, ec_sw, ec)
                cum_s = cumt_ref[pl.ds(head, 1), :]
                decay = jnp.exp(jnp.where(causal, cum_l - cum_s, NEG))
                wmat = (decay * cbm * dtt_ref[pl.ds(head, 1), :]).astype(BF16)
                xm = jnp.where(left if q == 0 else jnp.logical_not(left), xpair, 0.0).astype(BF16)
                acc = acc + jnp.dot(wmat, xm, preferred_element_type=F32)
            parts.append(acc)
        y = jnp.concatenate(parts, axis=1) + y_off
        y = y + xg * dsk_ref[:, cols]
        zg = z_ref[:, cols]
        y = y * (zg if conv_done else zg * _sigmoid(zg))
        y = y * lax.rsqrt(jnp.mean(y * y, axis=-1, keepdims=True) + EPS) * nw_ref[:, cols]
        y_ref[:, cols] = y.astype(BF16)
        cl = ecg[blk - 1:blk, :]
        xt = (xg * (jnp.exp(cl - ecg) * edg)).astype(BF16)
        h_in = lax.dot_general(bb16, xt, (((0,), (0,)), ((), ())), preferred_element_type=F32)
        state_ref[:, cols] = sg * jnp.exp(cl) + h_in
        return carry

    lax.fori_loop(0, SSD_GROUPS, group, 0, unroll=4)

    @pl.when(c == n_chunks - 1)
    def _():
        so_ref[0] = state_ref[...].T


def ssd_scan(zx, dt, p, s0, c0, *, n_batch, n_chunks, valid_len, shared_init, conv_done=False):
    blk = SSD_BLOCK
    n_rows = n_batch * n_chunks * blk
    assert zx.shape == (n_rows, ZX_DIM)
    rowi = lambda b, c: b * n_chunks + c
    init = (lambda b, c: (0, 0, 0)) if shared_init else (lambda b, c: (b, 0, 0))
    const = lambda b, c: (0, 0)
    in_specs = [
        pl.BlockSpec((blk, D_INNER), lambda b, c: (rowi(b, c), 0)),
        pl.BlockSpec((blk, D_INNER), lambda b, c: (rowi(b, c), 1)),
        pl.BlockSpec((blk, GN), lambda b, c: (rowi(b, c), 2 * D_INNER // GN)),
        pl.BlockSpec((blk, GN), lambda b, c: (rowi(b, c), 2 * D_INNER // GN + 1)),
        pl.BlockSpec((blk, LANES), lambda b, c: (rowi(b, c), 0)),
        pl.BlockSpec((8, CONV_DIM), const),
        pl.BlockSpec((1, CONV_DIM), const),
        pl.BlockSpec((1, LANES), const),
        pl.BlockSpec((1, LANES), const),
        pl.BlockSpec((1, D_INNER), const),
        pl.BlockSpec((1, D_INNER), const),
        pl.BlockSpec((32, GROUP_W), const),
        pl.BlockSpec((1, D_INNER, SSD_STATE), init),
        pl.BlockSpec((1, 8, CONV_DIM), init),
    ]
    out_specs = [pl.BlockSpec((blk, D_INNER), lambda b, c: (rowi(b, c), 0)),
                 pl.BlockSpec((1, D_INNER, SSD_STATE), lambda b, c: (b, 0, 0))]
    out_shape = [jax.ShapeDtypeStruct((n_rows, D_INNER), BF16),
                 jax.ShapeDtypeStruct((n_batch, D_INNER, SSD_STATE), F32)]
    scratch = [pltpu.VMEM((SSD_STATE, D_INNER), F32),
               pltpu.VMEM((8, LANES) if conv_done else (blk + 8, CONV_DIM), F32),
               pltpu.VMEM((LANES, blk), F32),
               pltpu.VMEM((LANES, blk), F32)]
    kern = functools.partial(_ssd_kernel, valid_len=valid_len, n_chunks=n_chunks, conv_done=conv_done)
    return pl.pallas_call(
        kern, grid=(n_batch, n_chunks), in_specs=in_specs, out_specs=out_specs,
        out_shape=out_shape, scratch_shapes=scratch,
        compiler_params=_cparams(("parallel", "arbitrary")), name="ssd_scan")(
            zx, zx, zx, zx, dt, p["conv_w"], p["conv_b"], p["dt_bias"], p["a_log"], p["d_skip"],
            p["norm_w"], p["expand"], s0, c0)


def _attn_core(items, sink_ref):
    n_q = items[0][0].shape[0]
    scale = ATTN_HEAD_DIM ** -0.5
    left = lax.broadcasted_iota(I32, (n_q, LANES), 1) < ATTN_HEAD_DIM
    nt = (((1,), (1,)), ((), ()))
    n_pairs = KV_DIM // LANES
    kps = [[kk[:, pr * LANES:(pr + 1) * LANES].astype(BF16) for pr in range(n_pairs)] for _, kk, _, _ in items]
    vps = [[vv[:, pr * LANES:(pr + 1) * LANES].astype(BF16) for pr in range(n_pairs)] for _, _, vv, _ in items]

    def scores(t, h):
        q, kmask = items[t][0], items[t][3]
        blk, half = divmod(h, 2)
        qb = q[:, blk * LANES:(blk + 1) * LANES]
        qh = jnp.where(left if half == 0 else jnp.logical_not(left), qb, 0.0).astype(BF16)
        s = lax.dot_general(qh, kps[t][blk // Q_PER_KV], nt, preferred_element_type=F32) * scale
        return s if kmask is None else jnp.where(kmask, s, NEG)

    def attend(t, h, s):
        sink = sink_ref[h]
        m = jnp.maximum(jnp.max(s, axis=-1, keepdims=True), sink)
        p = jnp.exp(s - m)
        den = jnp.sum(p, axis=-1, keepdims=True) + jnp.exp(sink - m)
        p = p * (1.0 / den)
        return jnp.dot(p.astype(BF16), vps[t][h // (2 * Q_PER_KV)], preferred_element_type=F32)

    work = [(t, h) for t in range(len(items)) for h in range(N_HEADS)]
    pending, done = {}, {}
    for n in range(len(work) + ATTN_LAG):
        if n < len(work):
            pending[work[n]] = scores(*work[n])
        if n >= ATTN_LAG:
            key = work[n - ATTN_LAG]
            done[key] = attend(*key, pending.pop(key))
    return [jnp.concatenate([jnp.where(left, done[(t, 2 * b)], done[(t, 2 * b + 1)])
                             for b in range(N_HEADS // 2)], axis=1) for t in range(len(items))]


def _attn_band_kernel(sink_ref, q_ref, k_ref, v_ref, mk_ref, mv_ref, o_ref):
    band = WINDOW + CHUNK
    items = []
    for sub in range(ATTN_CHUNKS):
        c = pl.program_id(1) * ATTN_CHUNKS + sub
        first = jnp.maximum(c - WINDOW // CHUNK, 0)
        start = pl.multiple_of(first * CHUNK, CHUNK)
        kk = jnp.concatenate([mk_ref[...], k_ref[pl.ds(start, band), :]], axis=0)
        vv = jnp.concatenate([mv_ref[...], v_ref[pl.ds(start, band), :]], axis=0)
        j = lax.broadcasted_iota(I32, (1, N_META + band), 1)
        key_chunk = first + jnp.maximum(j - N_META, 0) // CHUNK
        kmask = (j < N_META) | (key_chunk <= c)
        items.append((q_ref[sub * CHUNK:(sub + 1) * CHUNK, :], kk, vv, kmask))
    for sub, o in enumerate(_attn_core(items, sink_ref)):
        o_ref[sub * CHUNK:(sub + 1) * CHUNK, :] = o.astype(BF16)


def attn_band(qkv, meta_k, meta_v, sinks, *, n_batch, seq):
    rows = ATTN_CHUNKS * CHUNK
    n_chunks = seq // rows
    assert seq >= WINDOW + CHUNK and seq % rows == 0
    kcol = Q_DIM // KV_DIM
    grid_spec = pltpu.PrefetchScalarGridSpec(
        num_scalar_prefetch=1, grid=(n_batch, n_chunks),
        in_specs=[pl.BlockSpec((rows, Q_DIM), lambda b, c, s: (b * n_chunks + c, 0)),
                  pl.BlockSpec((seq, KV_DIM), lambda b, c, s: (b, kcol)),
                  pl.BlockSpec((seq, KV_DIM), lambda b, c, s: (b, kcol + 1)),
                  pl.BlockSpec((N_META, KV_DIM), lambda b, c, s: (0, 0)),
                  pl.BlockSpec((N_META, KV_DIM), lambda b, c, s: (0, 0))],
        out_specs=pl.BlockSpec((rows, Q_DIM), lambda b, c, s: (b * n_chunks + c, 0)))
    return pl.pallas_call(
        _attn_band_kernel, grid_spec=grid_spec,
        out_shape=jax.ShapeDtypeStruct((n_batch * seq, Q_DIM), BF16),
        compiler_params=_cparams(("parallel", "arbitrary")), name="attn_band")(
            sinks, qkv, qkv, qkv, meta_k, meta_v)


def _attn_full_kernel(sink_ref, q_ref, k_ref, v_ref, o_ref):
    o_ref[...] = _attn_core([(q_ref[...], k_ref[...], v_ref[...], None)], sink_ref)[0].astype(BF16)


def attn_full(q, kk, vv, sinks):
    n_batch, n_q, _ = q.shape
    n_k = kk.shape[1]
    grid_spec = pltpu.PrefetchScalarGridSpec(
        num_scalar_prefetch=1, grid=(n_batch,),
        in_specs=[pl.BlockSpec((None, n_q, Q_DIM), lambda b, s: (b, 0, 0)),
                  pl.BlockSpec((None, n_k, KV_DIM), lambda b, s: (b, 0, 0)),
                  pl.BlockSpec((None, n_k, KV_DIM), lambda b, s: (b, 0, 0))],
        out_specs=pl.BlockSpec((None, n_q, Q_DIM), lambda b, s: (b, 0, 0)))
    return pl.pallas_call(
        _attn_full_kernel, grid_spec=grid_spec,
        out_shape=jax.ShapeDtypeStruct((n_batch, n_q, Q_DIM), BF16),
        compiler_params=_cparams(("parallel",)), name="attn_full")(sinks, q, kk, vv)


def _expert_ffn(x16, w_in, w_out):
    gu = jnp.dot(x16, w_in, preferred_element_type=F32)
    a, b = gu[:, :D_EXPERT], gu[:, D_EXPERT:]
    hmid = a * _sigmoid(a) * b
    return jnp.dot(hmid.astype(BF16), w_out, preferred_element_type=F32)


def _moe_dense_kernel(x_ref, e_ref, p_ref, wi_ref, wo_ref, o_ref):
    e = pl.program_id(0)

    @pl.when(e == 0)
    def _():
        o_ref[...] = jnp.zeros_like(o_ref)

    eid, wts = e_ref[...], p_ref[...]
    sel0, sel1 = eid[:, 0:1] == e, eid[:, 1:2] == e
    coef = jnp.where(sel0, wts[:, 0:1], jnp.where(sel1, wts[:, 1:2], 0.0))
    y = _expert_ffn(x_ref[...].astype(BF16), wi_ref[0].astype(BF16), wo_ref[0].astype(BF16))
    o_ref[...] += jnp.where(sel0 | sel1, y * coef, 0.0)


def moe_dense(xn, eid, wts, w_in, w_out, *, layer):
    n_rows = xn.shape[0]
    full = lambda e: (0, 0)
    return pl.pallas_call(
        _moe_dense_kernel, grid=(N_EXPERTS,),
        in_specs=[pl.BlockSpec((n_rows, D_MODEL), full), pl.BlockSpec((n_rows, LANES), full),
                  pl.BlockSpec((n_rows, LANES), full),
                  pl.BlockSpec((None, 1, D_MODEL, 2 * D_EXPERT), lambda e: (layer, e, 0, 0)),
                  pl.BlockSpec((None, 1, D_EXPERT, D_MODEL), lambda e: (layer, e, 0, 0))],
        out_specs=pl.BlockSpec((n_rows, D_MODEL), full),
        out_shape=jax.ShapeDtypeStruct((n_rows, D_MODEL), F32),
        compiler_params=_cparams(("arbitrary",)), name="moe_dense")(xn, eid, wts, w_in, w_out)


def _moe_sorted_kernel(blk_e_ref, n_used_ref, src_ref, src_next_ref, src_ahead_ref, dst_prev_ref, dst_ref,
                       x_hbm, wi_ref, wo_ref, y_hbm, xbuf, obuf, x16_ref, hmid_ref, wi16_ref, wo16_ref,
                       gsem, ssem, *, tm, n_tok):
    i = pl.program_id(0)
    n_used = n_used_ref[0]
    nc = MOE_NC

    def start_gathers(rows_ref, s, lo=0, hi=tm):
        for r in range(lo, hi):
            pltpu.make_async_copy(x_hbm.at[pl.ds(rows_ref[0, 0, r], 1), :], xbuf.at[s, pl.ds(r, 1), :],
                                  gsem.at[s]).start()

    def start_scatters(rows_ref, s, lo=0, hi=tm):
        for r in range(lo, hi):
            pltpu.make_async_copy(obuf.at[s, pl.ds(r, 1), :], y_hbm.at[pl.ds(rows_ref[0, 0, r], 1), :],
                                  ssem.at[s]).start()

    def wait_block(buf, sem, s):
        pltpu.make_async_copy(buf.at[s, pl.ds(0, tm)], buf.at[s, pl.ds(0, tm)], sem.at[s]).wait()

    @pl.when(i == 0)
    def _():
        xbuf[...] = jnp.zeros_like(xbuf)
        obuf[...] = jnp.zeros_like(obuf)
        spare = [pltpu.make_async_copy(obuf.at[s], y_hbm.at[pl.ds(2 * n_tok + s * tm, tm), :], ssem.at[s])
                 for s in range(MOE_SLOTS)]
        for cp in spare:
            cp.start()
        for cp in spare:
            cp.wait()
        start_gathers(src_ref, 0)
        start_gathers(src_next_ref, 1)

    def step(slot):
        ahead, after = (slot + 2) % MOE_SLOTS, (slot + 1) % MOE_SLOTS
        wait_block(xbuf, gsem, slot)

        @pl.when(i >= 2)
        def _():
            wait_block(obuf, ssem, slot)

        n_a, n_b = 2 * D_EXPERT // nc, D_MODEL // nc
        t_a, t_b = D_MODEL // nc, D_EXPERT // nc
        first = 8
        unit = (2 * tm - first) / (n_a * t_a + n_b * t_b)
        quota = [first] + [round(unit * t_a)] * n_a + [round(unit * t_b)] * (n_b - 1)
        quota.append(2 * tm - sum(quota))
        issued = [0]

        def issue(lhs_ref):
            lo, hi = issued[0], issued[0] + quota.pop(0)
            issued[0] = hi
            start_scatters(dst_prev_ref, ahead, (lo + 1) // 2, (hi + 1) // 2)
            start_gathers(src_ahead_ref, ahead, lo // 2, hi // 2)
            if lhs_ref is not None:
                zero = xbuf[ahead, tm:tm + 8, 0:LANES]
                zero = jnp.concatenate([zero, zero], axis=0)
                lhs_ref[0:16, 0:LANES] = (lhs_ref[0:16, 0:LANES].astype(F32) + zero).astype(BF16)

        x16_ref[...] = xbuf[slot, 0:tm].astype(BF16)
        for k in range(D_EXPERT // nc):
            issue(x16_ref)
            a = jnp.dot(x16_ref[...], wi16_ref[:, k * nc:(k + 1) * nc], preferred_element_type=F32)
            issue(x16_ref)
            b = jnp.dot(x16_ref[...], wi16_ref[:, D_EXPERT + k * nc:D_EXPERT + (k + 1) * nc],
                        preferred_element_type=F32)
            hmid_ref[:, k * nc:(k + 1) * nc] = (a * _sigmoid(a) * b).astype(BF16)
        for k in range(n_b):
            issue(hmid_ref)
            obuf[slot, :, k * nc:(k + 1) * nc] = jnp.dot(hmid_ref[...], wo16_ref[:, k * nc:(k + 1) * nc],
                                                         preferred_element_type=F32)
        issue(None)
        assert not quota and issued[0] == 2 * tm

        @pl.when(i == n_used - 1)
        def _():
            start_scatters(dst_ref, slot)

            @pl.when(i >= 1)
            def _():
                wait_block(obuf, ssem, after)
            wait_block(obuf, ssem, ahead)
            wait_block(obuf, ssem, slot)
            wait_block(xbuf, gsem, after)
            wait_block(xbuf, gsem, ahead)

    @pl.when((i < n_used) & ((i == 0) | (blk_e_ref[i] != blk_e_ref[jnp.maximum(i - 1, 0)])))
    def _():
        wi16_ref[...] = wi_ref[0].astype(BF16)
        wo16_ref[...] = wo_ref[0].astype(BF16)

    for parity in range(MOE_SLOTS):
        pl.when((i < n_used) & (i % MOE_SLOTS == parity))(functools.partial(step, parity))


def moe_sorted(xn, w_in, w_out, tables, *, tm, layer):
    blk_e, n_used, src, dst = tables
    n_tok = xn.shape[0]
    n_blocks = blk_e.shape[0]
    smem = lambda index_map: pl.BlockSpec((1, 1, tm), index_map, memory_space=pltpu.SMEM)
    later = lambda k: (lambda i, be, nu: (jnp.maximum(jnp.minimum(i + k, nu[0] - 1), 0), 0, 0))
    grid_spec = pltpu.PrefetchScalarGridSpec(
        num_scalar_prefetch=2, grid=(n_blocks,),
        in_specs=[smem(lambda i, be, nu: (i, 0, 0)), smem(later(1)), smem(later(2)),
                  smem(lambda i, be, nu: (i, 0, 0)), smem(lambda i, be, nu: (i + 1, 0, 0)),
                  pl.BlockSpec(memory_space=pl.ANY),
                  pl.BlockSpec((None, 1, D_MODEL, 2 * D_EXPERT), lambda i, be, nu: (layer, be[i], 0, 0)),
                  pl.BlockSpec((None, 1, D_EXPERT, D_MODEL), lambda i, be, nu: (layer, be[i], 0, 0))],
        out_specs=pl.BlockSpec(memory_space=pl.ANY),
        scratch_shapes=[pltpu.VMEM((MOE_SLOTS, tm + 8, D_MODEL), F32), pltpu.VMEM((MOE_SLOTS, tm, D_MODEL), F32),
                        pltpu.VMEM((tm, D_MODEL), BF16), pltpu.VMEM((tm, D_EXPERT), BF16),
                        pltpu.VMEM((D_MODEL, 2 * D_EXPERT), BF16), pltpu.VMEM((D_EXPERT, D_MODEL), BF16),
                        pltpu.SemaphoreType.DMA((MOE_SLOTS,)), pltpu.SemaphoreType.DMA((MOE_SLOTS,))])
    return pl.pallas_call(
        functools.partial(_moe_sorted_kernel, tm=tm, n_tok=n_tok), grid_spec=grid_spec,
        out_shape=jax.ShapeDtypeStruct((2 * n_tok + MOE_SLOTS * tm, D_MODEL), F32),
        compiler_params=_cparams(("arbitrary",)), name="moe_sorted")(
            blk_e, n_used, src, src, src, dst, dst, xn, w_in, w_out)


def route_tables(eid, *, tm):
    n_tok = eid.shape[0]
    n_asg = 2 * n_tok
    flat = eid.reshape(-1)
    skey = jnp.sort(flat * n_asg + jnp.arange(n_asg, dtype=I32))
    order = skey % n_asg
    counts = jnp.sum((flat[:, None] == jnp.arange(N_EXPERTS, dtype=I32)[None, :]).astype(I32), axis=0)
    starts = jnp.cumsum(counts) - counts
    padded = (counts + tm - 1) // tm * tm
    pend = jnp.cumsum(padded)
    pstarts = pend - padded
    n_blocks = -(-(n_asg + N_EXPERTS * (tm - 1)) // tm)
    blk = jnp.arange(n_blocks, dtype=I32)
    ends = jnp.zeros((n_blocks + 1,), I32).at[jnp.minimum(pend // tm, n_blocks)].add(1)
    blk_e = jnp.minimum(jnp.cumsum(ends)[:n_blocks], N_EXPERTS - 1)
    blk_off = blk * tm - pstarts[blk_e]
    blk_n = jnp.clip(counts[blk_e] - blk_off, 0, tm)
    r = jnp.arange(tm, dtype=I32)[None, :]
    valid = r < blk_n[:, None]
    code = order[jnp.clip((starts[blk_e] + blk_off)[:, None] + r, 0, n_asg - 1)]
    src = jnp.where(valid, code >> 1, 0)
    spare = 2 * n_tok + (blk % MOE_SLOTS)[:, None] * tm + r
    dst = jnp.where(valid, (code & 1) * n_tok + (code >> 1), spare)
    dst = jnp.concatenate([2 * n_tok + (MOE_SLOTS - 1) * tm + r, dst], axis=0)
    n_used = (pend[-1] // tm).reshape(1)
    return (blk_e.astype(I32), n_used.astype(I32), src.reshape(n_blocks, 1, tm).astype(I32),
            dst.reshape(n_blocks + 1, 1, tm).astype(I32))


def _q_head_order():
    order = []
    for pr in range(N_KV_HEADS // 2):
        for r in range(Q_PER_KV):
            for half in range(2):
                order.append((2 * pr + half) * Q_PER_KV + r)
    return jnp.asarray(order, dtype=I32)


def _pad_lanes(a):
    return jnp.pad(a, [(0, 0)] * (a.ndim - 1) + [(0, LANES - a.shape[-1])])


def _ssd_params(conv_w, conv_b, dt_bias, a_log, d_skip, norm_w):
    head_of_col = jnp.arange(GROUP_W, dtype=I32) // SSD_HEAD_DIM
    row = jnp.arange(32, dtype=I32)
    expand = ((row[:, None] % 8 == head_of_col[None, :]) & (row[:, None] < 24)).astype(BF16)
    return {
        "conv_w": jnp.pad(conv_w, ((0, 8 - SSD_CONV), (0, 0))),
        "conv_b": conv_b.reshape(1, CONV_DIM),
        "dt_bias": _pad_lanes(dt_bias.reshape(1, SSD_HEADS)),
        "a_log": _pad_lanes(a_log.reshape(1, SSD_HEADS)),
        "d_skip": jnp.repeat(d_skip, SSD_HEAD_DIM).reshape(1, D_INNER),
        "norm_w": norm_w.reshape(1, D_INNER),
        "expand": expand,
    }


def _router_params(w_group, b_group, w_router, b_router):
    wr = _pad_lanes(jnp.concatenate([w_group, w_router], axis=1)).astype(BF16)
    br = _pad_lanes(jnp.concatenate([b_group, b_router]).reshape(1, -1)).astype(F32)
    return wr, br


def _pick_tile(n, prefs):
    for t in prefs:
        if n % t == 0:
            return t
    return n


def kernel(x_prompt, x_sample, cache_ssm_state, cache_ssm_conv, cache_attn_k, cache_attn_v, cache_meta_k, cache_meta_v, meta_tokens, norm_mix, norm_ffn, norm_final, ssd_w_in, ssd_conv_w, ssd_conv_b, ssd_dt_bias, ssd_a_log, ssd_d, ssd_norm, ssd_w_out, attn_w_qkv, attn_sinks, attn_w_out, moe_w_group, moe_b_group, moe_w_router, moe_b_router, moe_w_in, moe_w_out):
    n_batch, seq, _ = x_prompt.shape
    dec_batch, dec_seq, _ = x_sample.shape
    n_real = n_batch * seq
    n_small = N_META + dec_batch * dec_seq
    blk = SSD_BLOCK
    assert seq % blk == 0 and dec_seq <= blk and N_META <= blk
    tm_mid = _pick_tile(n_real, (512, 256))
    tm_res = _pick_tile(n_real, (256,))

    h_real = x_prompt.reshape(n_real, D_MODEL)
    h_small = jnp.concatenate([meta_tokens, x_sample.reshape(-1, D_MODEL)], axis=0)

    w_in16 = ssd_w_in[0].astype(BF16)
    w_zx, w_dt = w_in16[:, :ZX_DIM], _pad_lanes(w_in16[:, ZX_DIM:])
    sp = _ssd_params(ssd_conv_w[0], ssd_conv_b[0], ssd_dt_bias[0], ssd_a_log[0], ssd_d[0], ssd_norm[0])
    w_out16 = ssd_w_out[0].astype(BF16)
    wr0, br0 = _router_params(moe_w_group[0], moe_b_group[0], moe_w_router[0], moe_b_router[0])

    zx_s, dt_s = norm_matmul(h_small, norm_mix[0], w_zx, tm=n_small, tn=512, w_extra=w_dt)

    n_sb = 1 + dec_batch

    def to_blocks(a, n_valid_meta=N_META):
        meta = jnp.pad(a[:N_META], ((0, blk - N_META), (0, 0)))[None]
        samp = jnp.pad(a[N_META:].reshape(dec_batch, dec_seq, -1), ((0, 0), (0, blk - dec_seq), (0, 0)))
        return jnp.concatenate([meta, samp], axis=0).reshape(n_sb * blk, -1)

    assert dec_seq == N_META, "small-path blocks share one valid length"
    s0_small = jnp.concatenate([jnp.zeros((1, D_INNER, SSD_STATE), F32),
                                cache_ssm_state[0].reshape(dec_batch, D_INNER, SSD_STATE)], axis=0)
    c0_small = jnp.pad(jnp.concatenate([jnp.zeros((1, SSD_CONV - 1, CONV_DIM), F32), cache_ssm_conv[0]], axis=0),
                       ((0, 0), (8 - (SSD_CONV - 1), 0), (0, 0)))
    yn_sb, state_small = ssd_scan(to_blocks(zx_s), to_blocks(dt_s), sp, s0_small, c0_small,
                                  n_batch=n_sb, n_chunks=1, valid_len=dec_seq, shared_init=False)
    yn_sb = yn_sb.reshape(n_sb, blk, D_INNER)
    yn_small = jnp.concatenate([yn_sb[0, :N_META], yn_sb[1:, :dec_seq].reshape(-1, D_INNER)], axis=0)

    c0_real = jnp.pad(zx_s[N_META - (SSD_CONV - 1):N_META, D_INNER:], ((8 - (SSD_CONV - 1), 0), (0, 0)))
    tm_in = _pick_tile(seq, (1024, 512, 256, 128))
    zx_r, dt_r, raw_tail = ssd_in_proj(h_real, norm_mix[0], w_zx, w_dt, sp, c0_real, tm=tm_in, tn=1024, seq=seq)
    yn_real, state_real = ssd_scan(zx_r, dt_r, sp, state_small[0:1], c0_real[None],
                                   n_batch=n_batch, n_chunks=seq // blk, valid_len=blk, shared_init=True,
                                   conv_done=True)
    conv_real = raw_tail.reshape(n_batch, seq // tm_in, 8, CONV_DIM)[:, -1, 8 - (SSD_CONV - 1):]

    tk0 = 2048
    h1_s, xn_s, eid_s, wts_s = matmul_route(yn_small, w_out16, h_small, norm_ffn[0], wr0, br0, tm=n_small, tk=tk0)
    h1_r, xn_r, eid_r, wts_r = matmul_route(yn_real, w_out16, h_real, norm_ffn[0], wr0, br0, tm=tm_res,
                                            tk=D_INNER)

    moe_s = moe_dense(xn_s, eid_s, wts_s, moe_w_in, moe_w_out, layer=0)
    moe_r = moe_sorted(xn_r, moe_w_in, moe_w_out, route_tables(eid_r[:, :2], tm=MOE_TM), tm=MOE_TM, layer=0)

    head_order = _q_head_order()
    wq = attn_w_qkv[0][:, :Q_DIM].reshape(D_MODEL, N_HEADS, ATTN_HEAD_DIM)[:, head_order].reshape(D_MODEL, Q_DIM)
    w_qkv16 = jnp.concatenate([wq, attn_w_qkv[0][:, Q_DIM:]], axis=1).astype(BF16)
    sinks = attn_sinks[0][head_order].astype(F32)
    w_ao16 = attn_w_out[0].reshape(N_HEADS, ATTN_HEAD_DIM, D_MODEL)[head_order].reshape(Q_DIM, D_MODEL).astype(BF16)
    wr1, br1 = _router_params(moe_w_group[1], moe_b_group[1], moe_w_router[1], moe_b_router[1])

    qkv_s, h2_s = norm_matmul(h1_s, norm_mix[1], w_qkv16, tm=n_small, tn=512, mode="sum", moe=(moe_s,),
                              write_h=True)
    qkv_r, h2_r = norm_matmul(h1_r, norm_mix[1], w_qkv16, tm=tm_res, tn=Q_DIM + 2 * KV_DIM, mode="pair",
                              moe=(moe_r, wts_r), write_h=True)

    k_meta, v_meta = qkv_s[:N_META, Q_DIM:Q_DIM + KV_DIM], qkv_s[:N_META, Q_DIM + KV_DIM:]
    k_new = qkv_s[N_META:, Q_DIM:Q_DIM + KV_DIM].reshape(dec_batch, dec_seq, KV_DIM)
    v_new = qkv_s[N_META:, Q_DIM + KV_DIM:].reshape(dec_batch, dec_seq, KV_DIM)
    o_meta = attn_full(qkv_s[None, :N_META, :Q_DIM], k_meta[None], v_meta[None], sinks)
    kk_s = jnp.concatenate([cache_meta_k[0].reshape(dec_batch, N_META, KV_DIM),
                            cache_attn_k[0].reshape(dec_batch, WINDOW, KV_DIM), k_new], axis=1)
    vv_s = jnp.concatenate([cache_meta_v[0].reshape(dec_batch, N_META, KV_DIM),
                            cache_attn_v[0].reshape(dec_batch, WINDOW, KV_DIM), v_new], axis=1)
    o_samp = attn_full(qkv_s[N_META:, :Q_DIM].reshape(dec_batch, dec_seq, Q_DIM), kk_s, vv_s, sinks)
    o_small = jnp.concatenate([o_meta.reshape(N_META, Q_DIM), o_samp.reshape(-1, Q_DIM)], axis=0)
    o_real = attn_band(qkv_r, k_meta, v_meta, sinks, n_batch=n_batch, seq=seq)

    h3_s, xn_s, eid_s, wts_s = matmul_route(o_small, w_ao16, h2_s, norm_ffn[1], wr1, br1, tm=n_small, tk=1024)
    h3_r, xn_r, eid_r, wts_r = matmul_route(o_real, w_ao16, h2_r, norm_ffn[1], wr1, br1, tm=tm_mid, tk=2048)

    moe_s = moe_dense(xn_s, eid_s, wts_s, moe_w_in, moe_w_out, layer=1)
    moe_r = moe_sorted(xn_r, moe_w_in, moe_w_out, route_tables(eid_r[:, :2], tm=MOE_TM), tm=MOE_TM, layer=1)

    y_small = final_norm(h3_s, norm_final, tm=n_small, mode="sum", moe=(moe_s,))
    y_real = final_norm(h3_r, norm_final, tm=tm_mid, mode="pair", moe=(moe_r, wts_r))

    kv_shape = (N_KV_HEADS, ATTN_HEAD_DIM)
    qkv_r3 = qkv_r.reshape(n_batch, seq, -1)
    zx_samp = zx_s[N_META:].reshape(dec_batch, dec_seq, ZX_DIM)
    return (
        y_real.reshape(n_batch, seq, D_MODEL),
        y_small[N_META:].reshape(dec_batch, dec_seq, D_MODEL),
        state_real.reshape(1, n_batch, SSD_HEADS, SSD_HEAD_DIM, SSD_STATE),
        conv_real[None],
        qkv_r3[:, seq - WINDOW:, Q_DIM:Q_DIM + KV_DIM].reshape((1, n_batch, WINDOW) + kv_shape),
        qkv_r3[:, seq - WINDOW:, Q_DIM + KV_DIM:].reshape((1, n_batch, WINDOW) + kv_shape),
        jnp.broadcast_to(k_meta.reshape((1, 1, N_META) + kv_shape), (1, n_batch, N_META) + kv_shape),
        jnp.broadcast_to(v_meta.reshape((1, 1, N_META) + kv_shape), (1, n_batch, N_META) + kv_shape),
        state_small[1:].reshape(1, dec_batch, SSD_HEADS, SSD_HEAD_DIM, SSD_STATE),
        zx_samp[None, :, dec_seq - (SSD_CONV - 1):, D_INNER:],
        k_new.reshape((1, dec_batch, dec_seq) + kv_shape),
        v_new.reshape((1, dec_batch, dec_seq) + kv_shape),
    )
```

```python
import functools

import jax
import jax.numpy as jnp
from jax import lax
from jax.experimental import pallas as pl
from jax.experimental.pallas import tpu as pltpu

F32, BF16, I32 = jnp.float32, jnp.bfloat16, jnp.int32

D_MODEL = 2048
N_META = 16
EPS = 1e-5
D_INNER = 4096
SSD_HEAD_DIM = 64
SSD_HEADS = 64
SSD_GROUPS = 8
SSD_STATE = 128
SSD_CONV = 4
GN = SSD_GROUPS * SSD_STATE
CONV_DIM = D_INNER + 2 * GN
ZX_DIM = D_INNER + CONV_DIM
GROUP_W = D_INNER // SSD_GROUPS
WINDOW = 128
CHUNK = 64
ATTN_HEAD_DIM = 64
N_HEADS = 32
N_KV_HEADS = 8
Q_PER_KV = N_HEADS // N_KV_HEADS
Q_DIM = N_HEADS * ATTN_HEAD_DIM
KV_DIM = N_KV_HEADS * ATTN_HEAD_DIM
N_GROUPS = 4
EXPERTS_PER_GROUP = 8
N_EXPERTS = 32
D_EXPERT = 512

LANES = 128
SSD_BLOCK = 128
NEG = -1e30
VMEM_LIMIT = 56 * 1024 * 1024
MOE_TM = 256
MOE_NC = 256
MOE_SLOTS = 3
IN_NC = 256
ATTN_CHUNKS = 4
ATTN_LAG = ATTN_CHUNKS * N_HEADS


def _cparams(sem):
    return pltpu.CompilerParams(dimension_semantics=sem, vmem_limit_bytes=VMEM_LIMIT)


def _resident_spec(block_shape, index_map, *, resident):
    if resident:
        return pl.BlockSpec(block_shape, index_map, pipeline_mode=pl.Buffered(1))
    return pl.BlockSpec(block_shape, index_map)


def _sigmoid(x):
    return 1.0 / (1.0 + jnp.exp(-x))


def _rms(h, g):
    var = jnp.mean(h * h, axis=-1, keepdims=True)
    return h * lax.rsqrt(var + EPS) * g


def _row_chunk(tm):
    for rc in (256, 128, 64, 32, 16, 8):
        if tm % rc == 0:
            return rc
    raise ValueError(tm)


_N_MOE_REFS = {"none": 0, "sum": 1, "pair": 3}


def _combine_rows(x_ref, moe_refs, mode, rows):
    h = x_ref[rows, :]
    if mode == "sum":
        h = h + moe_refs[0][rows, :]
    elif mode == "pair":
        y0, y1, w = moe_refs
        wv = w[rows, :]
        h = h + (y0[rows, :] * wv[:, 0:1] + y1[rows, :] * wv[:, 1:2])
    return h


def _norm_mm_kernel(*refs, mode, write_h, has_extra, tm):
    it = iter(refs)
    x_ref = next(it)
    moe_refs = [next(it) for _ in range(_N_MOE_REFS[mode])]
    g_ref, w_ref = next(it), next(it)
    we_ref = next(it) if has_extra else None
    o_ref = next(it)
    h_ref = next(it) if write_h else None
    e_ref = next(it) if has_extra else None
    xn_ref = next(it)
    rc = _row_chunk(tm)

    @pl.when(pl.program_id(1) == 0)
    def _():
        def body(r, carry):
            rows = pl.ds(pl.multiple_of(r * rc, rc), rc)
            h = _combine_rows(x_ref, moe_refs, mode, rows)
            if write_h:
                h_ref[rows, :] = h
            xn_ref[rows, :] = _rms(h, g_ref[...]).astype(BF16)
            return carry

        lax.fori_loop(0, tm // rc, body, 0)
        if has_extra:
            e_ref[...] = jnp.dot(xn_ref[...], we_ref[...], preferred_element_type=F32)

    o_ref[...] = jnp.dot(xn_ref[...], w_ref[...], preferred_element_type=F32)


def _moe_in_specs(mode, moe, tm, n_rows):
    if mode == "none":
        return [], []
    if mode == "sum":
        return [pl.BlockSpec((tm, D_MODEL), lambda i, j: (i, 0))], [moe[0]]
    y2, wts = moe
    off = n_rows // tm
    specs = [pl.BlockSpec((tm, D_MODEL), lambda i, j: (i, 0)),
             pl.BlockSpec((tm, D_MODEL), lambda i, j: (i + off, 0)),
             pl.BlockSpec((tm, LANES), lambda i, j: (i, 0))]
    return specs, [y2, y2, wts]


def norm_matmul(x, g, w, *, tm, tn, mode="none", moe=(), w_extra=None, write_h=False):
    n_rows, n_out = x.shape[0], w.shape[1]
    assert n_rows % tm == 0 and n_out % tn == 0
    moe_specs, moe_ops = _moe_in_specs(mode, moe, tm, n_rows)
    in_specs = [pl.BlockSpec((tm, D_MODEL), lambda i, j: (i, 0))] + moe_specs + [
        pl.BlockSpec((1, D_MODEL), lambda i, j: (0, 0)),
        _resident_spec((D_MODEL, tn), lambda i, j: (0, j), resident=n_out == tn)]
    operands = [x] + moe_ops + [g.reshape(1, D_MODEL), w]
    out_shape = [jax.ShapeDtypeStruct((n_rows, n_out), F32)]
    out_specs = [pl.BlockSpec((tm, tn), lambda i, j: (i, j))]
    if w_extra is not None:
        in_specs.append(pl.BlockSpec((D_MODEL, LANES), lambda i, j: (0, 0)))
        operands.append(w_extra)
    if write_h:
        out_shape.append(jax.ShapeDtypeStruct((n_rows, D_MODEL), F32))
        out_specs.append(pl.BlockSpec((tm, D_MODEL), lambda i, j: (i, 0)))
    if w_extra is not None:
        out_shape.append(jax.ShapeDtypeStruct((n_rows, LANES), F32))
        out_specs.append(pl.BlockSpec((tm, LANES), lambda i, j: (i, 0)))
    kern = functools.partial(_norm_mm_kernel, mode=mode, write_h=write_h,
                             has_extra=w_extra is not None, tm=tm)
    return pl.pallas_call(
        kern, grid=(n_rows // tm, n_out // tn), in_specs=in_specs, out_specs=out_specs,
        out_shape=out_shape, scratch_shapes=[pltpu.VMEM((tm, D_MODEL), BF16)],
        compiler_params=_cparams(("parallel", "arbitrary")), name="norm_matmul")(*operands)


def _final_norm_kernel(*refs, mode, tm):
    x_ref = refs[0]
    moe_refs = refs[1:1 + _N_MOE_REFS[mode]]
    g_ref, o_ref = refs[-2], refs[-1]
    rc = _row_chunk(tm)

    def body(r, carry):
        rows = pl.ds(pl.multiple_of(r * rc, rc), rc)
        o_ref[rows, :] = _rms(_combine_rows(x_ref, moe_refs, mode, rows), g_ref[...])
        return carry

    lax.fori_loop(0, tm // rc, body, 0)


def final_norm(x, g, *, tm, mode, moe):
    n_rows = x.shape[0]
    moe_specs, moe_ops = _moe_in_specs(mode, moe, tm, n_rows)
    in_specs = [pl.BlockSpec((tm, D_MODEL), lambda i, j: (i, 0))] + moe_specs + [
        pl.BlockSpec((1, D_MODEL), lambda i, j: (0, 0))]
    return pl.pallas_call(
        functools.partial(_final_norm_kernel, mode=mode, tm=tm), grid=(n_rows // tm, 1),
        in_specs=in_specs, out_specs=pl.BlockSpec((tm, D_MODEL), lambda i, j: (i, 0)),
        out_shape=jax.ShapeDtypeStruct((n_rows, D_MODEL), F32),
        compiler_params=_cparams(("parallel", "arbitrary")), name="final_norm")(
            x, *moe_ops, g.reshape(1, D_MODEL))


def _in_proj_kernel(x_ref, g_ref, w_ref, wdt_ref, cw_ref, cb_ref, c0_ref, o_ref, dt_ref, rt_ref,
                    xn_ref, tail_ref, stage_ref, *, tm, tn, tiles_per_batch, n_z):
    i, j = pl.program_id(0), pl.program_id(1)
    rc = _row_chunk(tm)

    @pl.when(j == 0)
    def _():
        def body(r, carry):
            rows = pl.ds(pl.multiple_of(r * rc, rc), rc)
            xn_ref[rows, :] = _rms(x_ref[rows, :], g_ref[...]).astype(BF16)
            return carry

        lax.fori_loop(0, tm // rc, body, 0)
        dt_ref[...] = jnp.dot(xn_ref[...], wdt_ref[...], preferred_element_type=F32)

    def chunks():
        for c in range(tn // IN_NC):
            cs = slice(c * IN_NC, (c + 1) * IN_NC)
            yield c, cs, jnp.dot(xn_ref[...], w_ref[:, cs], preferred_element_type=F32)

    @pl.when(j < n_z)
    def _():
        for _, cs, raw in chunks():
            o_ref[:, cs] = raw * _sigmoid(raw)

    @pl.when(j >= n_z)
    def _():
        e = j - n_z

        @pl.when(i % tiles_per_batch == 0)
        def _():
            tail_ref[e] = c0_ref[...]

        for c, cs, raw in chunks():
            stage = stage_ref.at[c % 2]
            stage[0:8, :] = tail_ref[e, :, cs]
            stage[8:8 + tm, :] = raw
            acc = cb_ref[:, cs]
            for k in range(SSD_CONV):
                acc = acc + stage[pl.ds(8 - (SSD_CONV - 1) + k, tm), :] * cw_ref[k:k + 1, cs]
            o_ref[:, cs] = acc * _sigmoid(acc)
            last = stage[tm:tm + 8, :]
            tail_ref[e, :, cs] = last
            rt_ref[0, :, cs] = last


def ssd_in_proj(x, g, w, w_dt, p, c0, *, tm, tn, seq):
    n_rows = x.shape[0]
    assert n_rows % tm == 0 and seq % tm == 0 and D_INNER % tn == 0 and CONV_DIM % tn == 0
    n_z = D_INNER // tn
    conv_col = lambda i, j: (0, jnp.maximum(j - n_z, 0))
    kern = functools.partial(_in_proj_kernel, tm=tm, tn=tn, tiles_per_batch=seq // tm, n_z=n_z)
    return pl.pallas_call(
        kern, grid=(n_rows // tm, ZX_DIM // tn),
        in_specs=[pl.BlockSpec((tm, D_MODEL), lambda i, j: (i, 0)),
                  pl.BlockSpec((1, D_MODEL), lambda i, j: (0, 0)),
                  pl.BlockSpec((D_MODEL, tn), lambda i, j: (0, j)),
                  pl.BlockSpec((D_MODEL, LANES), lambda i, j: (0, 0)),
                  pl.BlockSpec((8, tn), conv_col),
                  pl.BlockSpec((1, tn), conv_col),
                  pl.BlockSpec((8, tn), conv_col)],
        out_specs=[pl.BlockSpec((tm, tn), lambda i, j: (i, j)),
                   pl.BlockSpec((tm, LANES), lambda i, j: (i, 0)),
                   pl.BlockSpec((1, 8, tn), lambda i, j: (i, 0, jnp.maximum(j - n_z, 0)))],
        out_shape=[jax.ShapeDtypeStruct((n_rows, ZX_DIM), F32),
                   jax.ShapeDtypeStruct((n_rows, LANES), F32),
                   jax.ShapeDtypeStruct((n_rows // tm, 8, CONV_DIM), F32)],
        scratch_shapes=[pltpu.VMEM((tm, D_MODEL), BF16),
                        pltpu.VMEM((CONV_DIM // tn, 8, tn), F32),
                        pltpu.VMEM((2, tm + 8, IN_NC), F32)],
        compiler_params=_cparams(("arbitrary", "arbitrary")), name="ssd_in_proj")(
            x, g.reshape(1, D_MODEL), w, w_dt, p["conv_w"], p["conv_b"], c0)


def _route(lg):
    lane = lax.broadcasted_iota(I32, lg.shape, 1)
    is_g = lane < N_GROUPS
    gl = jnp.where(is_g, lg, NEG)
    gmax = jnp.max(gl, axis=-1, keepdims=True)
    gsum = jnp.sum(jnp.where(is_g, jnp.exp(gl - gmax), 0.0), axis=-1, keepdims=True)
    gate = 1.0 / gsum
    gidx = jnp.min(jnp.where(gl == gmax, lane, LANES), axis=-1, keepdims=True)
    ecol = lane - N_GROUPS
    in_group = (ecol >= 0) & (ecol < N_EXPERTS) & ((ecol >> 3) == gidx)
    el = jnp.where(in_group, lg, NEG)
    v1 = jnp.max(el, axis=-1, keepdims=True)
    i1 = jnp.min(jnp.where(el == v1, lane, LANES), axis=-1, keepdims=True)
    el2 = jnp.where(lane == i1, NEG, el)
    v2 = jnp.max(el2, axis=-1, keepdims=True)
    i2 = jnp.min(jnp.where(el2 == v2, lane, LANES), axis=-1, keepdims=True)
    e2 = jnp.exp(v2 - v1)
    w1 = gate / (1.0 + e2)
    w2 = gate * e2 / (1.0 + e2)
    eid = jnp.where(lane == 0, i1 - N_GROUPS, jnp.where(lane == 1, i2 - N_GROUPS, 0))
    wts = jnp.where(lane == 0, w1, jnp.where(lane == 1, w2, 0.0))
    return eid, wts


def _mm_route_kernel(a_ref, w_ref, h_ref, g_ref, wr_ref, br_ref,
                     h1_ref, xn_ref, e_ref, p_ref, acc_ref, *, nk, tm):
    k = pl.program_id(1)
    rc = _row_chunk(tm)

    @pl.when(k == 0)
    def _():
        acc_ref[...] = jnp.zeros_like(acc_ref)

    acc_ref[...] += jnp.dot(a_ref[...], w_ref[...], preferred_element_type=F32)

    @pl.when(k == nk - 1)
    def _():
        def body(r, carry):
            rows = pl.ds(pl.multiple_of(r * rc, rc), rc)
            h1 = h_ref[rows, :] + acc_ref[rows, :]
            h1_ref[rows, :] = h1
            xn = _rms(h1, g_ref[...])
            xn_ref[rows, :] = xn
            lg = jnp.dot(xn.astype(BF16), wr_ref[...], preferred_element_type=F32) + br_ref[...]
            eid, wts = _route(lg)
            e_ref[rows, :] = eid
            p_ref[rows, :] = wts
            return carry

        lax.fori_loop(0, tm // rc, body, 0)


def matmul_route(a, w, h, g, wr, br, *, tm, tk):
    n_rows, kdim = a.shape
    assert n_rows % tm == 0 and kdim % tk == 0
    nk = kdim // tk
    row_spec = pl.BlockSpec((tm, D_MODEL), lambda i, k: (i, 0))
    lane_spec = pl.BlockSpec((tm, LANES), lambda i, k: (i, 0))
    return pl.pallas_call(
        functools.partial(_mm_route_kernel, nk=nk, tm=tm), grid=(n_rows // tm, nk),
        in_specs=[pl.BlockSpec((tm, tk), lambda i, k: (i, k)),
                  _resident_spec((tk, D_MODEL), lambda i, k: (k, 0), resident=nk == 1),
                  row_spec,
                  pl.BlockSpec((1, D_MODEL), lambda i, k: (0, 0)),
                  pl.BlockSpec((D_MODEL, LANES), lambda i, k: (0, 0)),
                  pl.BlockSpec((1, LANES), lambda i, k: (0, 0))],
        out_specs=[row_spec, row_spec, lane_spec, lane_spec],
        out_shape=[jax.ShapeDtypeStruct((n_rows, D_MODEL), F32),
                   jax.ShapeDtypeStruct((n_rows, D_MODEL), F32),
                   jax.ShapeDtypeStruct((n_rows, LANES), I32),
                   jax.ShapeDtypeStruct((n_rows, LANES), F32)],
        scratch_shapes=[pltpu.VMEM((tm, D_MODEL), F32)],
        compiler_params=_cparams(("parallel", "arbitrary")), name="matmul_route")(
            a, w, h, g.reshape(1, D_MODEL), wr, br)


def _expand_heads(v, eg):
    v1 = v.astype(BF16).astype(F32)
    r1 = v - v1
    v2 = r1.astype(BF16).astype(F32)
    v3 = (r1 - v2).astype(BF16).astype(F32)
    parts = jnp.concatenate([v1, v2, v3, jnp.zeros_like(v)], axis=0).astype(BF16)
    return lax.dot_general(parts, eg, (((0,), (0,)), ((), ())), preferred_element_type=F32)


def _ssd_kernel(z_ref, x_ref, b_ref, c_ref, dt_ref, cw_ref, cb_ref, dtb_ref, alog_ref, dsk_ref,
                nw_ref, eg_ref, s0_ref, c0_ref, y_ref, so_ref,
                state_ref, xp_ref, cumt_ref, dtt_ref, *, valid_len, n_chunks, conv_done):
    blk = SSD_BLOCK
    c = pl.program_id(1)

    @pl.when(c == 0)
    def _():
        state_ref[...] = s0_ref[0].T
        if not conv_done:
            xp_ref[0:8, :] = c0_ref[0]

    if not conv_done:
        @pl.when(c > 0)
        def _():
            xp_ref[0:8, :] = xp_ref[blk:blk + 8, :]

        xp_ref[8:8 + blk, 0:D_INNER] = x_ref[...]
        xp_ref[8:8 + blk, D_INNER:D_INNER + GN] = b_ref[...]
        xp_ref[8:8 + blk, D_INNER + GN:CONV_DIM] = c_ref[...]

    row = lax.broadcasted_iota(I32, (blk, LANES), 0)
    lane = lax.broadcasted_iota(I32, (blk, LANES), 1)
    dtr = dt_ref[...] + dtb_ref[...]
    dt = jnp.maximum(dtr, 0.0) + jnp.log1p(jnp.exp(-jnp.abs(dtr)))
    if valid_len < blk:
        dt = jnp.where(row < valid_len, dt, 0.0)
    cum = dt * (-jnp.exp(alog_ref[...]))
    sh = 1
    while sh < blk:
        cum = cum + jnp.where(row >= sh, pltpu.roll(cum, sh, 0), 0.0)
        sh *= 2
    cumt_ref[...] = cum.T
    dtt_ref[...] = dt.T

    causal = lane <= row
    left = lane < SSD_HEAD_DIM

    def conv(cols):
        acc = cb_ref[:, cols]
        for k in range(SSD_CONV):
            acc = acc + xp_ref[pl.ds(8 - (SSD_CONV - 1) + k, blk), cols] * cw_ref[k:k + 1, cols]
        return acc * _sigmoid(acc)

    def group(g, carry):
        cols = pl.ds(pl.multiple_of(g * GROUP_W, GROUP_W), GROUP_W)
        if conv_done:
            ncols = pl.ds(pl.multiple_of(g * SSD_STATE, SSD_STATE), SSD_STATE)
            xg, bg, cg = x_ref[:, cols], b_ref[:, ncols], c_ref[:, ncols]
        else:
            xg = conv(cols)
            bg = conv(pl.ds(pl.multiple_of(D_INNER + g * SSD_STATE, SSD_STATE), SSD_STATE))
            cg = conv(pl.ds(pl.multiple_of(D_INNER + GN + g * SSD_STATE, SSD_STATE), SSD_STATE))
        cb16, bb16 = cg.astype(BF16), bg.astype(BF16)
        cbm = lax.dot_general(cb16, bb16, (((1,), (1,)), ((), ())), preferred_element_type=F32)
        sg = state_ref[:, cols]
        heads = pl.ds(pl.multiple_of(g * (GROUP_W // SSD_HEAD_DIM), 8), GROUP_W // SSD_HEAD_DIM)
        ecg = _expand_heads(cumt_ref[heads, :], eg_ref[...])
        edg = _expand_heads(dtt_ref[heads, :], eg_ref[...])
        y_off =jnp.dot(cb16, sg.astype(BF16), preferred_element_type=F32) * jnp.exp(ecg)
        parts = []
        for pr in range(GROUP_W // LANES):
            xpair = xg[:, pr * LANES:(pr + 1) * LANES]
            ec = ecg[:, pr * LANES:(pr + 1) * LANES]
            ec_sw = pltpu.roll(ec, SSD_HEAD_DIM, 1)
            acc = jnp.zeros((blk, LANES), F32)
            for q in range(2):
                head = g * (GROUP_W // SSD_HEAD_DIM) + pr * 2 + q
                cum_l = jnp.where(left, ec, ec_sw) if q == 0 else jnp.where(left, ec_sw, ec)
                cum_s = cumt_ref[pl.ds(head, 1), :]
                decay = jnp.exp(jnp.where(causal, cum_l - cum_s, NEG))
                wmat = (decay * cbm * dtt_ref[pl.ds(head, 1), :]).astype(BF16)
                xm = jnp.where(left if q == 0 else jnp.logical_not(left), xpair, 0.0).astype(BF16)
                acc = acc + jnp.dot(wmat, xm, preferred_element_type=F32)
            parts.append(acc)
        y = jnp.concatenate(parts, axis=1) + y_off
        y = y + xg * dsk_ref[:, cols]
        zg = z_ref[:, cols]
        y = y * (zg if conv_done else zg * _sigmoid(zg))
        y = y * lax.rsqrt(jnp.mean(y * y, axis=-1, keepdims=True) + EPS) * nw_ref[:, cols]
        y_ref[:, cols] = y.astype(BF16)
        cl = ecg[blk - 1:blk, :]
        xt = (xg * (jnp.exp(cl - ecg) * edg)).astype(BF16)
        h_in = lax.dot_general(bb16, xt, (((0,), (0,)), ((), ())), preferred_element_type=F32)
        state_ref[:, cols] = sg * jnp.exp(cl) + h_in
        return carry

    lax.fori_loop(0, SSD_GROUPS, group, 0, unroll=4)

    @pl.when(c == n_chunks - 1)
    def _():
        so_ref[0] = state_ref[...].T


def ssd_scan(zx, dt, p, s0, c0, *, n_batch, n_chunks, valid_len, shared_init, conv_done=False):
    blk = SSD_BLOCK
    n_rows = n_batch * n_chunks * blk
    assert zx.shape == (n_rows, ZX_DIM)
    rowi = lambda b, c: b * n_chunks + c
    init = (lambda b, c: (0, 0, 0)) if shared_init else (lambda b, c: (b, 0, 0))
    const = lambda b, c: (0, 0)
    in_specs = [
        pl.BlockSpec((blk, D_INNER), lambda b, c: (rowi(b, c), 0)),
        pl.BlockSpec((blk, D_INNER), lambda b, c: (rowi(b, c), 1)),
        pl.BlockSpec((blk, GN), lambda b, c: (rowi(b, c), 2 * D_INNER // GN)),
        pl.BlockSpec((blk, GN), lambda b, c: (rowi(b, c), 2 * D_INNER // GN + 1)),
        pl.BlockSpec((blk, LANES), lambda b, c: (rowi(b, c), 0)),
        pl.BlockSpec((8, CONV_DIM), const),
        pl.BlockSpec((1, CONV_DIM), const),
        pl.BlockSpec((1, LANES), const),
        pl.BlockSpec((1, LANES), const),
        pl.BlockSpec((1, D_INNER), const),
        pl.BlockSpec((1, D_INNER), const),
        pl.BlockSpec((32, GROUP_W), const),
        pl.BlockSpec((1, D_INNER, SSD_STATE), init),
        pl.BlockSpec((1, 8, CONV_DIM), init),
    ]
    out_specs = [pl.BlockSpec((blk, D_INNER), lambda b, c: (rowi(b, c), 0)),
                 pl.BlockSpec((1, D_INNER, SSD_STATE), lambda b, c: (b, 0, 0))]
    out_shape = [jax.ShapeDtypeStruct((n_rows, D_INNER), BF16),
                 jax.ShapeDtypeStruct((n_batch, D_INNER, SSD_STATE), F32)]
    scratch = [pltpu.VMEM((SSD_STATE, D_INNER), F32),
               pltpu.VMEM((8, LANES) if conv_done else (blk + 8, CONV_DIM), F32),
               pltpu.VMEM((LANES, blk), F32),
               pltpu.VMEM((LANES, blk), F32)]
    kern = functools.partial(_ssd_kernel, valid_len=valid_len, n_chunks=n_chunks, conv_done=conv_done)
    return pl.pallas_call(
        kern, grid=(n_batch, n_chunks), in_specs=in_specs, out_specs=out_specs,
        out_shape=out_shape, scratch_shapes=scratch,
        compiler_params=_cparams(("parallel", "arbitrary")), name="ssd_scan")(
            zx, zx, zx, zx, dt, p["conv_w"], p["conv_b"], p["dt_bias"], p["a_log"], p["d_skip"],
            p["norm_w"], p["expand"], s0, c0)


def _attn_core(items, sink_ref):
    n_q = items[0][0].shape[0]
    scale = ATTN_HEAD_DIM ** -0.5
    left = lax.broadcasted_iota(I32, (n_q, LANES), 1) < ATTN_HEAD_DIM
    nt = (((1,), (1,)), ((), ()))
    n_pairs = KV_DIM // LANES
    kps = [[kk[:, pr * LANES:(pr + 1) * LANES].astype(BF16) for pr in range(n_pairs)] for _, kk, _, _ in items]
    vps = [[vv[:, pr * LANES:(pr + 1) * LANES].astype(BF16) for pr in range(n_pairs)] for _, _, vv, _ in items]

    def scores(t, h):
        q, kmask = items[t][0], items[t][3]
        blk, half = divmod(h, 2)
        qb = q[:, blk * LANES:(blk + 1) * LANES]
        qh = jnp.where(left if half == 0 else jnp.logical_not(left), qb, 0.0).astype(BF16)
        s = lax.dot_general(qh, kps[t][blk // Q_PER_KV], nt, preferred_element_type=F32) * scale
        return s if kmask is None else jnp.where(kmask, s, NEG)

    def attend(t, h, s):
        sink = sink_ref[h]
        m = jnp.maximum(jnp.max(s, axis=-1, keepdims=True), sink)
        p = jnp.exp(s - m)
        den = jnp.sum(p, axis=-1, keepdims=True) + jnp.exp(sink - m)
        p = p * (1.0 / den)
        return jnp.dot(p.astype(BF16), vps[t][h // (2 * Q_PER_KV)], preferred_element_type=F32)

    work = [(t, h) for t in range(len(items)) for h in range(N_HEADS)]
    pending, done = {}, {}
    for n in range(len(work) + ATTN_LAG):
        if n < len(work):
            pending[work[n]] = scores(*work[n])
        if n >= ATTN_LAG:
            key = work[n - ATTN_LAG]
            done[key] = attend(*key, pending.pop(key))
    return [jnp.concatenate([jnp.where(left, done[(t, 2 * b)], done[(t, 2 * b + 1)])
                             for b in range(N_HEADS // 2)], axis=1) for t in range(len(items))]


def _attn_band_kernel(sink_ref, q_ref, k_ref, v_ref, mk_ref, mv_ref, o_ref):
    band = WINDOW + CHUNK
    items = []
    for sub in range(ATTN_CHUNKS):
        c = pl.program_id(1) * ATTN_CHUNKS + sub
        first = jnp.maximum(c - WINDOW // CHUNK, 0)
        start = pl.multiple_of(first * CHUNK, CHUNK)
        kk = jnp.concatenate([mk_ref[...], k_ref[pl.ds(start, band), :]], axis=0)
        vv = jnp.concatenate([mv_ref[...], v_ref[pl.ds(start, band), :]], axis=0)
        j = lax.broadcasted_iota(I32, (1, N_META + band), 1)
        key_chunk = first + jnp.maximum(j - N_META, 0) // CHUNK
        kmask = (j < N_META) | (key_chunk <= c)
        items.append((q_ref[sub * CHUNK:(sub + 1) * CHUNK, :], kk, vv, kmask))
    for sub, o in enumerate(_attn_core(items, sink_ref)):
        o_ref[sub * CHUNK:(sub + 1) * CHUNK, :] = o.astype(BF16)


def attn_band(qkv, meta_k, meta_v, sinks, *, n_batch, seq):
    rows = ATTN_CHUNKS * CHUNK
    n_chunks = seq // rows
    assert seq >= WINDOW + CHUNK and seq % rows == 0
    kcol = Q_DIM // KV_DIM
    grid_spec = pltpu.PrefetchScalarGridSpec(
        num_scalar_prefetch=1, grid=(n_batch, n_chunks),
        in_specs=[pl.BlockSpec((rows, Q_DIM), lambda b, c, s: (b * n_chunks + c, 0)),
                  pl.BlockSpec((seq, KV_DIM), lambda b, c, s: (b, kcol)),
                  pl.BlockSpec((seq, KV_DIM), lambda b, c, s: (b, kcol + 1)),
                  pl.BlockSpec((N_META, KV_DIM), lambda b, c, s: (0, 0)),
                  pl.BlockSpec((N_META, KV_DIM), lambda b, c, s: (0, 0))],
        out_specs=pl.BlockSpec((rows, Q_DIM), lambda b, c, s: (b * n_chunks + c, 0)))
    return pl.pallas_call(
        _attn_band_kernel, grid_spec=grid_spec,
        out_shape=jax.ShapeDtypeStruct((n_batch * seq, Q_DIM), BF16),
        compiler_params=_cparams(("parallel", "arbitrary")), name="attn_band")(
            sinks, qkv, qkv, qkv, meta_k, meta_v)


def _attn_full_kernel(sink_ref, q_ref, k_ref, v_ref, o_ref):
    o_ref[...] = _attn_core([(q_ref[...], k_ref[...], v_ref[...], None)], sink_ref)[0].astype(BF16)


def attn_full(q, kk, vv, sinks):
    n_batch, n_q, _ = q.shape
    n_k = kk.shape[1]
    grid_spec = pltpu.PrefetchScalarGridSpec(
        num_scalar_prefetch=1, grid=(n_batch,),
        in_specs=[pl.BlockSpec((None, n_q, Q_DIM), lambda b, s: (b, 0, 0)),
                  pl.BlockSpec((None, n_k, KV_DIM), lambda b, s: (b, 0, 0)),
                  pl.BlockSpec((None, n_k, KV_DIM), lambda b, s: (b, 0, 0))],
        out_specs=pl.BlockSpec((None, n_q, Q_DIM), lambda b, s: (b, 0, 0)))
    return pl.pallas_call(
        _attn_full_kernel, grid_spec=grid_spec,
        out_shape=jax.ShapeDtypeStruct((n_batch, n_q, Q_DIM), BF16),
        compiler_params=_cparams(("parallel",)), name="attn_full")(sinks, q, kk, vv)


def _expert_ffn(x16, w_in, w_out):
    gu = jnp.dot(x16, w_in, preferred_element_type=F32)
    a, b = gu[:, :D_EXPERT], gu[:, D_EXPERT:]
    hmid = a * _sigmoid(a) * b
    return jnp.dot(hmid.astype(BF16), w_out, preferred_element_type=F32)


def _moe_dense_kernel(x_ref, e_ref, p_ref, wi_ref, wo_ref, o_ref):
    e = pl.program_id(0)

    @pl.when(e == 0)
    def _():
        o_ref[...] = jnp.zeros_like(o_ref)

    eid, wts = e_ref[...], p_ref[...]
    sel0, sel1 = eid[:, 0:1] == e, eid[:, 1:2] == e
    coef = jnp.where(sel0, wts[:, 0:1], jnp.where(sel1, wts[:, 1:2], 0.0))
    y = _expert_ffn(x_ref[...].astype(BF16), wi_ref[0].astype(BF16), wo_ref[0].astype(BF16))
    o_ref[...] += jnp.where(sel0 | sel1, y * coef, 0.0)


def moe_dense(xn, eid, wts, w_in, w_out, *, layer):
    n_rows = xn.shape[0]
    full = lambda e: (0, 0)
    return pl.pallas_call(
        _moe_dense_kernel, grid=(N_EXPERTS,),
        in_specs=[pl.BlockSpec((n_rows, D_MODEL), full), pl.BlockSpec((n_rows, LANES), full),
                  pl.BlockSpec((n_rows, LANES), full),
                  pl.BlockSpec((None, 1, D_MODEL, 2 * D_EXPERT), lambda e: (layer, e, 0, 0)),
                  pl.BlockSpec((None, 1, D_EXPERT, D_MODEL), lambda e: (layer, e, 0, 0))],
        out_specs=pl.BlockSpec((n_rows, D_MODEL), full),
        out_shape=jax.ShapeDtypeStruct((n_rows, D_MODEL), F32),
        compiler_params=_cparams(("arbitrary",)), name="moe_dense")(xn, eid, wts, w_in, w_out)


def _moe_sorted_kernel(blk_e_ref, n_used_ref, src_ref, src_next_ref, src_ahead_ref, dst_prev_ref, dst_ref,
                       x_hbm, wi_ref, wo_ref, y_hbm, xbuf, obuf, x16_ref, hmid_ref, wi16_ref, wo16_ref,
                       gsem, ssem, *, tm, n_tok):
    i = pl.program_id(0)
    n_used = n_used_ref[0]
    nc = MOE_NC

    def start_gathers(rows_ref, s, lo=0, hi=tm):
        for r in range(lo, hi):
            pltpu.make_async_copy(x_hbm.at[pl.ds(rows_ref[0, 0, r], 1), :], xbuf.at[s, pl.ds(r, 1), :],
                                  gsem.at[s]).start(priority=r % 2)

    def start_scatters(rows_ref, s, lo=0, hi=tm):
        for r in range(lo, hi):
            pltpu.make_async_copy(obuf.at[s, pl.ds(r, 1), :], y_hbm.at[pl.ds(rows_ref[0, 0, r], 1), :],
                                  ssem.at[s]).start(priority=r % 2)

    def wait_block(buf, sem, s):
        pltpu.make_async_copy(buf.at[s, pl.ds(0, tm)], buf.at[s, pl.ds(0, tm)], sem.at[s]).wait()

    @pl.when(i == 0)
    def _():
        xbuf[...] = jnp.zeros_like(xbuf)
        obuf[...] = jnp.zeros_like(obuf)
        spare = [pltpu.make_async_copy(obuf.at[s], y_hbm.at[pl.ds(2 * n_tok + s * tm, tm), :], ssem.at[s])
                 for s in range(MOE_SLOTS)]
        for cp in spare:
            cp.start()
        for cp in spare:
            cp.wait()
        start_gathers(src_ref, 0)
        start_gathers(src_next_ref, 1)

    def step(slot):
        ahead, after = (slot + 2) % MOE_SLOTS, (slot + 1) % MOE_SLOTS
        wait_block(xbuf, gsem, slot)

        @pl.when(i >= 2)
        def _():
            wait_block(obuf, ssem, slot)

        n_a, n_b = 2 * D_EXPERT // nc, D_MODEL // nc
        t_a, t_b = D_MODEL // nc, D_EXPERT // nc
        first = 8
        unit = (2 * tm - first) / (n_a * t_a + n_b * t_b)
        quota = [first] + [round(unit * t_a)] * n_a + [round(unit * t_b)] * (n_b - 1)
        quota.append(2 * tm - sum(quota))
        issued = [0]

        def issue(lhs_ref):
            lo, hi = issued[0], issued[0] + quota.pop(0)
            issued[0] = hi
            start_scatters(dst_prev_ref, ahead, (lo + 1) // 2, (hi + 1) // 2)
            start_gathers(src_ahead_ref, ahead, lo // 2, hi // 2)
            if lhs_ref is not None:
                zero = xbuf[ahead, tm:tm + 8, 0:LANES]
                zero = jnp.concatenate([zero, zero], axis=0)
                lhs_ref[0:16, 0:LANES] = (lhs_ref[0:16, 0:LANES].astype(F32) + zero).astype(BF16)

        x16_ref[...] = xbuf[slot, 0:tm].astype(BF16)
        for k in range(D_EXPERT // nc):
            issue(x16_ref)
            a = jnp.dot(x16_ref[...], wi16_ref[:, k * nc:(k + 1) * nc], preferred_element_type=F32)
            issue(x16_ref)
            b = jnp.dot(x16_ref[...], wi16_ref[:, D_EXPERT + k * nc:D_EXPERT + (k + 1) * nc],
                        preferred_element_type=F32)
            hmid_ref[:, k * nc:(k + 1) * nc] = (a * _sigmoid(a) * b).astype(BF16)
        for k in range(n_b):
            issue(hmid_ref)
            obuf[slot, :, k * nc:(k + 1) * nc] = jnp.dot(hmid_ref[...], wo16_ref[:, k * nc:(k + 1) * nc],
                                                         preferred_element_type=F32)
        issue(None)
        assert not quota and issued[0] == 2 * tm

        @pl.when(i == n_used - 1)
        def _():
            start_scatters(dst_ref, slot)

            @pl.when(i >= 1)
            def _():
                wait_block(obuf, ssem, after)
            wait_block(obuf, ssem, ahead)
            wait_block(obuf, ssem, slot)
            wait_block(xbuf, gsem, after)
            wait_block(xbuf, gsem, ahead)

    @pl.when((i < n_used) & ((i == 0) | (blk_e_ref[i] != blk_e_ref[jnp.maximum(i - 1, 0)])))
    def _():
        wi16_ref[...] = wi_ref[0].astype(BF16)
        wo16_ref[...] = wo_ref[0].astype(BF16)

    for parity in range(MOE_SLOTS):
        pl.when((i < n_used) & (i % MOE_SLOTS == parity))(functools.partial(step, parity))


def moe_sorted(xn, w_in, w_out, tables, *, tm, layer):
    blk_e, n_used, src, dst = tables
    n_tok = xn.shape[0]
    n_blocks = blk_e.shape[0]
    smem = lambda index_map: pl.BlockSpec((1, 1, tm), index_map, memory_space=pltpu.SMEM)
    later = lambda k: (lambda i, be, nu: (jnp.maximum(jnp.minimum(i + k, nu[0] - 1), 0), 0, 0))
    grid_spec = pltpu.PrefetchScalarGridSpec(
        num_scalar_prefetch=2, grid=(n_blocks,),
        in_specs=[smem(lambda i, be, nu: (i, 0, 0)), smem(later(1)), smem(later(2)),
                  smem(lambda i, be, nu: (i, 0, 0)), smem(lambda i, be, nu: (i + 1, 0, 0)),
                  pl.BlockSpec(memory_space=pl.ANY),
                  pl.BlockSpec((None, 1, D_MODEL, 2 * D_EXPERT), lambda i, be, nu: (layer, be[i], 0, 0)),
                  pl.BlockSpec((None, 1, D_EXPERT, D_MODEL), lambda i, be, nu: (layer, be[i], 0, 0))],
        out_specs=pl.BlockSpec(memory_space=pl.ANY),
        scratch_shapes=[pltpu.VMEM((MOE_SLOTS, tm + 8, D_MODEL), F32), pltpu.VMEM((MOE_SLOTS, tm, D_MODEL), F32),
                        pltpu.VMEM((tm, D_MODEL), BF16), pltpu.VMEM((tm, D_EXPERT), BF16),
                        pltpu.VMEM((D_MODEL, 2 * D_EXPERT), BF16), pltpu.VMEM((D_EXPERT, D_MODEL), BF16),
                        pltpu.SemaphoreType.DMA((MOE_SLOTS,)), pltpu.SemaphoreType.DMA((MOE_SLOTS,))])
    return pl.pallas_call(
        functools.partial(_moe_sorted_kernel, tm=tm, n_tok=n_tok), grid_spec=grid_spec,
        out_shape=jax.ShapeDtypeStruct((2 * n_tok + MOE_SLOTS * tm, D_MODEL), F32),
        compiler_params=_cparams(("arbitrary",)), name="moe_sorted")(
            blk_e, n_used, src, src, src, dst, dst, xn, w_in, w_out)


def route_tables(eid, *, tm):
    n_tok = eid.shape[0]
    n_asg = 2 * n_tok
    flat = eid.reshape(-1)
    skey = jnp.sort(flat * n_asg + jnp.arange(n_asg, dtype=I32))
    order = skey % n_asg
    counts = jnp.sum((flat[:, None] == jnp.arange(N_EXPERTS, dtype=I32)[None, :]).astype(I32), axis=0)
    starts = jnp.cumsum(counts) - counts
    padded = (counts + tm - 1) // tm * tm
    pend = jnp.cumsum(padded)
    pstarts = pend - padded
    n_blocks = -(-(n_asg + N_EXPERTS * (tm - 1)) // tm)
    blk = jnp.arange(n_blocks, dtype=I32)
    ends = jnp.zeros((n_blocks + 1,), I32).at[jnp.minimum(pend // tm, n_blocks)].add(1)
    blk_e = jnp.minimum(jnp.cumsum(ends)[:n_blocks], N_EXPERTS - 1)
    blk_off = blk * tm - pstarts[blk_e]
    blk_n = jnp.clip(counts[blk_e] - blk_off, 0, tm)
    r = jnp.arange(tm, dtype=I32)[None, :]
    valid = r < blk_n[:, None]
    code = order[jnp.clip((starts[blk_e] + blk_off)[:, None] + r, 0, n_asg - 1)]
    src = jnp.where(valid, code >> 1, 0)
    spare = 2 * n_tok + (blk % MOE_SLOTS)[:, None] * tm + r
    dst = jnp.where(valid, (code & 1) * n_tok + (code >> 1), spare)
    dst = jnp.concatenate([2 * n_tok + (MOE_SLOTS - 1) * tm + r, dst], axis=0)
    n_used = (pend[-1] // tm).reshape(1)
    return (blk_e.astype(I32), n_used.astype(I32), src.reshape(n_blocks, 1, tm).astype(I32),
            dst.reshape(n_blocks + 1, 1, tm).astype(I32))


def _q_head_order():
    order = []
    for pr in range(N_KV_HEADS // 2):
        for r in range(Q_PER_KV):
            for half in range(2):
                order.append((2 * pr + half) * Q_PER_KV + r)
    return jnp.asarray(order, dtype=I32)


def _pad_lanes(a):
    return jnp.pad(a, [(0, 0)] * (a.ndim - 1) + [(0, LANES - a.shape[-1])])


def _ssd_params(conv_w, conv_b, dt_bias, a_log, d_skip, norm_w):
    head_of_col = jnp.arange(GROUP_W, dtype=I32) // SSD_HEAD_DIM
    row = jnp.arange(32, dtype=I32)
    expand = ((row[:, None] % 8 == head_of_col[None, :]) & (row[:, None] < 24)).astype(BF16)
    return {
        "conv_w": jnp.pad(conv_w, ((0, 8 - SSD_CONV), (0, 0))),
        "conv_b": conv_b.reshape(1, CONV_DIM),
        "dt_bias": _pad_lanes(dt_bias.reshape(1, SSD_HEADS)),
        "a_log": _pad_lanes(a_log.reshape(1, SSD_HEADS)),
        "d_skip": jnp.repeat(d_skip, SSD_HEAD_DIM).reshape(1, D_INNER),
        "norm_w": norm_w.reshape(1, D_INNER),
        "expand": expand,
    }


def _router_params(w_group, b_group, w_router, b_router):
    wr = _pad_lanes(jnp.concatenate([w_group, w_router], axis=1)).astype(BF16)
    br = _pad_lanes(jnp.concatenate([b_group, b_router]).reshape(1, -1)).astype(F32)
    return wr, br


def _pick_tile(n, prefs):
    for t in prefs:
        if n % t == 0:
            return t
    return n


def kernel(x_prompt, x_sample, cache_ssm_state, cache_ssm_conv, cache_attn_k, cache_attn_v, cache_meta_k, cache_meta_v, meta_tokens, norm_mix, norm_ffn, norm_final, ssd_w_in, ssd_conv_w, ssd_conv_b, ssd_dt_bias, ssd_a_log, ssd_d, ssd_norm, ssd_w_out, attn_w_qkv, attn_sinks, attn_w_out, moe_w_group, moe_b_group, moe_w_router, moe_b_router, moe_w_in, moe_w_out):
    n_batch, seq, _ = x_prompt.shape
    dec_batch, dec_seq, _ = x_sample.shape
    n_real = n_batch * seq
    n_small = N_META + dec_batch * dec_seq
    blk = SSD_BLOCK
    assert seq % blk == 0 and dec_seq <= blk and N_META <= blk
    tm_mid = _pick_tile(n_real, (512, 256))
    tm_res = _pick_tile(n_real, (256,))

    h_real = x_prompt.reshape(n_real, D_MODEL)
    h_small = jnp.concatenate([meta_tokens, x_sample.reshape(-1, D_MODEL)], axis=0)

    w_in16 = ssd_w_in[0].astype(BF16)
    w_zx, w_dt = w_in16[:, :ZX_DIM], _pad_lanes(w_in16[:, ZX_DIM:])
    sp = _ssd_params(ssd_conv_w[0], ssd_conv_b[0], ssd_dt_bias[0], ssd_a_log[0], ssd_d[0], ssd_norm[0])
    w_out16 = ssd_w_out[0].astype(BF16)
    wr0, br0 = _router_params(moe_w_group[0], moe_b_group[0], moe_w_router[0], moe_b_router[0])

    zx_s, dt_s = norm_matmul(h_small, norm_mix[0], w_zx, tm=n_small, tn=512, w_extra=w_dt)

    n_sb = 1 + dec_batch

    def to_blocks(a, n_valid_meta=N_META):
        meta = jnp.pad(a[:N_META], ((0, blk - N_META), (0, 0)))[None]
        samp = jnp.pad(a[N_META:].reshape(dec_batch, dec_seq, -1), ((0, 0), (0, blk - dec_seq), (0, 0)))
        return jnp.concatenate([meta, samp], axis=0).reshape(n_sb * blk, -1)

    assert dec_seq == N_META, "small-path blocks share one valid length"
    s0_small = jnp.concatenate([jnp.zeros((1, D_INNER, SSD_STATE), F32),
                                cache_ssm_state[0].reshape(dec_batch, D_INNER, SSD_STATE)], axis=0)
    c0_small = jnp.pad(jnp.concatenate([jnp.zeros((1, SSD_CONV - 1, CONV_DIM), F32), cache_ssm_conv[0]], axis=0),
                       ((0, 0), (8 - (SSD_CONV - 1), 0), (0, 0)))
    yn_sb, state_small = ssd_scan(to_blocks(zx_s), to_blocks(dt_s), sp, s0_small, c0_small,
                                  n_batch=n_sb, n_chunks=1, valid_len=dec_seq, shared_init=False)
    yn_sb = yn_sb.reshape(n_sb, blk, D_INNER)
    yn_small = jnp.concatenate([yn_sb[0, :N_META], yn_sb[1:, :dec_seq].reshape(-1, D_INNER)], axis=0)

    c0_real = jnp.pad(zx_s[N_META - (SSD_CONV - 1):N_META, D_INNER:], ((8 - (SSD_CONV - 1), 0), (0, 0)))
    tm_in = _pick_tile(seq, (1024, 512, 256, 128))
    zx_r, dt_r, raw_tail = ssd_in_proj(h_real, norm_mix[0], w_zx, w_dt, sp, c0_real, tm=tm_in, tn=1024, seq=seq)
    yn_real, state_real = ssd_scan(zx_r, dt_r, sp, state_small[0:1], c0_real[None],
                                   n_batch=n_batch, n_chunks=seq // blk, valid_len=blk, shared_init=True,
                                   conv_done=True)
    conv_real = raw_tail.reshape(n_batch, seq // tm_in, 8, CONV_DIM)[:, -1, 8 - (SSD_CONV - 1):]

    tk0 = 2048
    h1_s, xn_s, eid_s, wts_s = matmul_route(yn_small, w_out16, h_small, norm_ffn[0], wr0, br0, tm=n_small, tk=tk0)
    h1_r, xn_r, eid_r, wts_r = matmul_route(yn_real, w_out16, h_real, norm_ffn[0], wr0, br0, tm=tm_res,
                                            tk=D_INNER)

    moe_s = moe_dense(xn_s, eid_s, wts_s, moe_w_in, moe_w_out, layer=0)
    moe_r = moe_sorted(xn_r, moe_w_in, moe_w_out, route_tables(eid_r[:, :2], tm=MOE_TM), tm=MOE_TM, layer=0)

    head_order = _q_head_order()
    wq = attn_w_qkv[0][:, :Q_DIM].reshape(D_MODEL, N_HEADS, ATTN_HEAD_DIM)[:, head_order].reshape(D_MODEL, Q_DIM)
    w_qkv16 = jnp.concatenate([wq, attn_w_qkv[0][:, Q_DIM:]], axis=1).astype(BF16)
    sinks = attn_sinks[0][head_order].astype(F32)
    w_ao16 = attn_w_out[0].reshape(N_HEADS, ATTN_HEAD_DIM, D_MODEL)[head_order].reshape(Q_DIM, D_MODEL).astype(BF16)
    wr1, br1 = _router_params(moe_w_group[1], moe_b_group[1], moe_w_router[1], moe_b_router[1])

    qkv_s, h2_s = norm_matmul(h1_s, norm_mix[1], w_qkv16, tm=n_small, tn=512, mode="sum", moe=(moe_s,),
                              write_h=True)
    qkv_r, h2_r = norm_matmul(h1_r, norm_mix[1], w_qkv16, tm=tm_res, tn=Q_DIM + 2 * KV_DIM, mode="pair",
                              moe=(moe_r, wts_r), write_h=True)

    k_meta, v_meta = qkv_s[:N_META, Q_DIM:Q_DIM + KV_DIM], qkv_s[:N_META, Q_DIM + KV_DIM:]
    k_new = qkv_s[N_META:, Q_DIM:Q_DIM + KV_DIM].reshape(dec_batch, dec_seq, KV_DIM)
    v_new = qkv_s[N_META:, Q_DIM + KV_DIM:].reshape(dec_batch, dec_seq, KV_DIM)
    o_meta = attn_full(qkv_s[None, :N_META, :Q_DIM], k_meta[None], v_meta[None], sinks)
    kk_s = jnp.concatenate([cache_meta_k[0].reshape(dec_batch, N_META, KV_DIM),
                            cache_attn_k[0].reshape(dec_batch, WINDOW, KV_DIM), k_new], axis=1)
    vv_s = jnp.concatenate([cache_meta_v[0].reshape(dec_batch, N_META, KV_DIM),
                            cache_attn_v[0].reshape(dec_batch, WINDOW, KV_DIM), v_new], axis=1)
    o_samp = attn_full(qkv_s[N_META:, :Q_DIM].reshape(dec_batch, dec_seq, Q_DIM), kk_s, vv_s, sinks)
    o_small = jnp.concatenate([o_meta.reshape(N_META, Q_DIM), o_samp.reshape(-1, Q_DIM)], axis=0)
    o_real = attn_band(qkv_r, k_meta, v_meta, sinks, n_batch=n_batch, seq=seq)

    h3_s, xn_s, eid_s, wts_s = matmul_route(o_small, w_ao16, h2_s, norm_ffn[1], wr1, br1, tm=n_small, tk=1024)
    h3_r, xn_r, eid_r, wts_r = matmul_route(o_real, w_ao16, h2_r, norm_ffn[1], wr1, br1, tm=tm_mid, tk=2048)

    moe_s = moe_dense(xn_s, eid_s, wts_s, moe_w_in, moe_w_out, layer=1)
    moe_r = moe_sorted(xn_r, moe_w_in, moe_w_out, route_tables(eid_r[:, :2], tm=MOE_TM), tm=MOE_TM, layer=1)

    y_small = final_norm(h3_s, norm_final, tm=n_small, mode="sum", moe=(moe_s,))
    y_real = final_norm(h3_r, norm_final, tm=tm_mid, mode="pair", moe=(moe_r, wts_r))

    kv_shape = (N_KV_HEADS, ATTN_HEAD_DIM)
    qkv_r3 = qkv_r.reshape(n_batch, seq, -1)
    zx_samp = zx_s[N_META:].reshape(dec_batch, dec_seq, ZX_DIM)
    return (
        y_real.reshape(n_batch, seq, D_MODEL),
        y_small[N_META:].reshape(dec_batch, dec_seq, D_MODEL),
        state_real.reshape(1, n_batch, SSD_HEADS, SSD_HEAD_DIM, SSD_STATE),
        conv_real[None],
        qkv_r3[:, seq - WINDOW:, Q_DIM:Q_DIM + KV_DIM].reshape((1, n_batch, WINDOW) + kv_shape),
        qkv_r3[:, seq - WINDOW:, Q_DIM + KV_DIM:].reshape((1, n_batch, WINDOW) + kv_shape),
        jnp.broadcast_to(k_meta.reshape((1, 1, N_META) + kv_shape), (1, n_batch, N_META) + kv_shape),
        jnp.broadcast_to(v_meta.reshape((1, 1, N_META) + kv_shape), (1, n_batch, N_META) + kv_shape),
        state_small[1:].reshape(1, dec_batch, SSD_HEADS, SSD_HEAD_DIM, SSD_STATE),
        zx_samp[None, :, dec_seq - (SSD_CONV - 1):, D_INNER:],
        k_new.reshape((1, dec_batch, dec_seq) + kv_shape),
        v_new.reshape((1, dec_batch, dec_seq) + kv_shape),
    )
```

```python
import functools

import jax
import jax.numpy as jnp
from jax import lax
from jax.experimental import pallas as pl
from jax.experimental.pallas import tpu as pltpu

F32, BF16, I32 = jnp.float32, jnp.bfloat16, jnp.int32

D_MODEL = 2048
N_META = 16
EPS = 1e-5
D_INNER = 4096
SSD_HEAD_DIM = 64
SSD_HEADS = 64
SSD_GROUPS = 8
SSD_STATE = 128
SSD_CONV = 4
GN = SSD_GROUPS * SSD_STATE
CONV_DIM = D_INNER + 2 * GN
ZX_DIM = D_INNER + CONV_DIM
GROUP_W = D_INNER // SSD_GROUPS
WINDOW = 128
CHUNK = 64
ATTN_HEAD_DIM = 64
N_HEADS = 32
N_KV_HEADS = 8
Q_PER_KV = N_HEADS // N_KV_HEADS
Q_DIM = N_HEADS * ATTN_HEAD_DIM
KV_DIM = N_KV_HEADS * ATTN_HEAD_DIM
N_GROUPS = 4
EXPERTS_PER_GROUP = 8
N_EXPERTS = 32
D_EXPERT = 512

LANES = 128
SSD_BLOCK = 128
NEG = -1e30
VMEM_LIMIT = 56 * 1024 * 1024
MOE_TM = 256
MOE_NC = 256
MOE_SLOTS = 3
IN_NC = 256
ATTN_CHUNKS = 4
ATTN_LAG = ATTN_CHUNKS * N_HEADS


def _cparams(sem):
    return pltpu.CompilerParams(dimension_semantics=sem, vmem_limit_bytes=VMEM_LIMIT)


def _resident_spec(block_shape, index_map, *, resident):
    if resident:
        return pl.BlockSpec(block_shape, index_map, pipeline_mode=pl.Buffered(1))
    return pl.BlockSpec(block_shape, index_map)


def _sigmoid(x):
    return 1.0 / (1.0 + jnp.exp(-x))


def _rms(h, g):
    var = jnp.mean(h * h, axis=-1, keepdims=True)
    return h * lax.rsqrt(var + EPS) * g


def _row_chunk(tm):
    for rc in (256, 128, 64, 32, 16, 8):
        if tm % rc == 0:
            return rc
    raise ValueError(tm)


_N_MOE_REFS = {"none": 0, "sum": 1, "pair": 3}


def _combine_rows(x_ref, moe_refs, mode, rows):
    h = x_ref[rows, :]
    if mode == "sum":
        h = h + moe_refs[0][rows, :]
    elif mode == "pair":
        y0, y1, w = moe_refs
        wv = w[rows, :]
        h = h + (y0[rows, :] * wv[:, 0:1] + y1[rows, :] * wv[:, 1:2])
    return h


def _norm_mm_kernel(*refs, mode, write_h, has_extra, tm):
    it = iter(refs)
    x_ref = next(it)
    moe_refs = [next(it) for _ in range(_N_MOE_REFS[mode])]
    g_ref, w_ref = next(it), next(it)
    we_ref = next(it) if has_extra else None
    o_ref = next(it)
    h_ref = next(it) if write_h else None
    e_ref = next(it) if has_extra else None
    xn_ref = next(it)
    rc = _row_chunk(tm)

    @pl.when(pl.program_id(1) == 0)
    def _():
        def body(r, carry):
            rows = pl.ds(pl.multiple_of(r * rc, rc), rc)
            h = _combine_rows(x_ref, moe_refs, mode, rows)
            if write_h:
                h_ref[rows, :] = h
            xn_ref[rows, :] = _rms(h, g_ref[...]).astype(BF16)
            return carry

        lax.fori_loop(0, tm // rc, body, 0)
        if has_extra:
            e_ref[...] = jnp.dot(xn_ref[...], we_ref[...], preferred_element_type=F32)

    o_ref[...] = jnp.dot(xn_ref[...], w_ref[...], preferred_element_type=F32)


def _moe_in_specs(mode, moe, tm, n_rows):
    if mode == "none":
        return [], []
    if mode == "sum":
        return [pl.BlockSpec((tm, D_MODEL), lambda i, j: (i, 0))], [moe[0]]
    y2, wts = moe
    off = n_rows // tm
    specs = [pl.BlockSpec((tm, D_MODEL), lambda i, j: (i, 0)),
             pl.BlockSpec((tm, D_MODEL), lambda i, j: (i + off, 0)),
             pl.BlockSpec((tm, LANES), lambda i, j: (i, 0))]
    return specs, [y2, y2, wts]


def norm_matmul(x, g, w, *, tm, tn, mode="none", moe=(), w_extra=None, write_h=False):
    n_rows, n_out = x.shape[0], w.shape[1]
    assert n_rows % tm == 0 and n_out % tn == 0
    moe_specs, moe_ops = _moe_in_specs(mode, moe, tm, n_rows)
    in_specs = [pl.BlockSpec((tm, D_MODEL), lambda i, j: (i, 0))] + moe_specs + [
        pl.BlockSpec((1, D_MODEL), lambda i, j: (0, 0)),
        _resident_spec((D_MODEL, tn), lambda i, j: (0, j), resident=n_out == tn)]
    operands = [x] + moe_ops + [g.reshape(1, D_MODEL), w]
    out_shape = [jax.ShapeDtypeStruct((n_rows, n_out), F32)]
    out_specs = [pl.BlockSpec((tm, tn), lambda i, j: (i, j))]
    if w_extra is not None:
        in_specs.append(pl.BlockSpec((D_MODEL, LANES), lambda i, j: (0, 0)))
        operands.append(w_extra)
    if write_h:
        out_shape.append(jax.ShapeDtypeStruct((n_rows, D_MODEL), F32))
        out_specs.append(pl.BlockSpec((tm, D_MODEL), lambda i, j: (i, 0)))
    if w_extra is not None:
        out_shape.append(jax.ShapeDtypeStruct((n_rows, LANES), F32))
        out_specs.append(pl.BlockSpec((tm, LANES), lambda i, j: (i, 0)))
    kern = functools.partial(_norm_mm_kernel, mode=mode, write_h=write_h,
                             has_extra=w_extra is not None, tm=tm)
    return pl.pallas_call(
        kern, grid=(n_rows // tm, n_out // tn), in_specs=in_specs, out_specs=out_specs,
        out_shape=out_shape, scratch_shapes=[pltpu.VMEM((tm, D_MODEL), BF16)],
        compiler_params=_cparams(("parallel", "arbitrary")), name="norm_matmul")(*operands)


def _final_norm_kernel(*refs, mode, tm):
    x_ref = refs[0]
    moe_refs = refs[1:1 + _N_MOE_REFS[mode]]
    g_ref, o_ref = refs[-2], refs[-1]
    rc = _row_chunk(tm)

    def body(r, carry):
        rows = pl.ds(pl.multiple_of(r * rc, rc), rc)
        o_ref[rows, :] = _rms(_combine_rows(x_ref, moe_refs, mode, rows), g_ref[...])
        return carry

    lax.fori_loop(0, tm // rc, body, 0)


def final_norm(x, g, *, tm, mode, moe):
    n_rows = x.shape[0]
    moe_specs, moe_ops = _moe_in_specs(mode, moe, tm, n_rows)
    in_specs = [pl.BlockSpec((tm, D_MODEL), lambda i, j: (i, 0))] + moe_specs + [
        pl.BlockSpec((1, D_MODEL), lambda i, j: (0, 0))]
    return pl.pallas_call(
        functools.partial(_final_norm_kernel, mode=mode, tm=tm), grid=(n_rows // tm, 1),
        in_specs=in_specs, out_specs=pl.BlockSpec((tm, D_MODEL), lambda i, j: (i, 0)),
        out_shape=jax.ShapeDtypeStruct((n_rows, D_MODEL), F32),
        compiler_params=_cparams(("parallel", "arbitrary")), name="final_norm")(
            x, *moe_ops, g.reshape(1, D_MODEL))


def _in_proj_kernel(x_ref, g_ref, w_ref, wdt_ref, cw_ref, cb_ref, c0_ref, o_ref, dt_ref, rt_ref,
                    xn_ref, tail_ref, stage_ref, *, tm, tn, tiles_per_batch, n_z):
    i, j = pl.program_id(0), pl.program_id(1)
    rc = _row_chunk(tm)

    @pl.when(j == 0)
    def _():
        def body(r, carry):
            rows = pl.ds(pl.multiple_of(r * rc, rc), rc)
            xn_ref[rows, :] = _rms(x_ref[rows, :], g_ref[...]).astype(BF16)
            return carry

        lax.fori_loop(0, tm // rc, body, 0)
        dt_ref[...] = jnp.dot(xn_ref[...], wdt_ref[...], preferred_element_type=F32)

    def chunks():
        for c in range(tn // IN_NC):
            cs = slice(c * IN_NC, (c + 1) * IN_NC)
            yield c, cs, jnp.dot(xn_ref[...], w_ref[:, cs], preferred_element_type=F32)

    @pl.when(j < n_z)
    def _():
        for _, cs, raw in chunks():
            o_ref[:, cs] = raw

    @pl.when(j >= n_z)
    def _():
        e = j - n_z

        @pl.when(i % tiles_per_batch == 0)
        def _():
            tail_ref[e] = c0_ref[...]

        for c, cs, raw in chunks():
            stage = stage_ref.at[c % 2]
            stage[0:8, :] = tail_ref[e, :, cs]
            stage[8:8 + tm, :] = raw
            acc = cb_ref[:, cs]
            for k in range(SSD_CONV):
                acc = acc + stage[pl.ds(8 - (SSD_CONV - 1) + k, tm), :] * cw_ref[k:k + 1, cs]
            o_ref[:, cs] = acc * _sigmoid(acc)
            last = stage[tm:tm + 8, :]
            tail_ref[e, :, cs] = last
            rt_ref[0, :, cs] = last


def ssd_in_proj(x, g, w, w_dt, p, c0, *, tm, tn, seq):
    n_rows = x.shape[0]
    assert n_rows % tm == 0 and seq % tm == 0 and D_INNER % tn == 0 and CONV_DIM % tn == 0
    n_z = D_INNER // tn
    conv_col = lambda i, j: (0, jnp.maximum(j - n_z, 0))
    kern = functools.partial(_in_proj_kernel, tm=tm, tn=tn, tiles_per_batch=seq // tm, n_z=n_z)
    return pl.pallas_call(
        kern, grid=(n_rows // tm, ZX_DIM // tn),
        in_specs=[pl.BlockSpec((tm, D_MODEL), lambda i, j: (i, 0)),
                  pl.BlockSpec((1, D_MODEL), lambda i, j: (0, 0)),
                  pl.BlockSpec((D_MODEL, tn), lambda i, j: (0, j)),
                  pl.BlockSpec((D_MODEL, LANES), lambda i, j: (0, 0)),
                  pl.BlockSpec((8, tn), conv_col),
                  pl.BlockSpec((1, tn), conv_col),
                  pl.BlockSpec((8, tn), conv_col)],
        out_specs=[pl.BlockSpec((tm, tn), lambda i, j: (i, j)),
                   pl.BlockSpec((tm, LANES), lambda i, j: (i, 0)),
                   pl.BlockSpec((1, 8, tn), lambda i, j: (i, 0, jnp.maximum(j - n_z, 0)))],
        out_shape=[jax.ShapeDtypeStruct((n_rows, ZX_DIM), F32),
                   jax.ShapeDtypeStruct((n_rows, LANES), F32),
                   jax.ShapeDtypeStruct((n_rows // tm, 8, CONV_DIM), F32)],
        scratch_shapes=[pltpu.VMEM((tm, D_MODEL), BF16),
                        pltpu.VMEM((CONV_DIM // tn, 8, tn), F32),
                        pltpu.VMEM((2, tm + 8, IN_NC), F32)],
        compiler_params=_cparams(("arbitrary", "arbitrary")), name="ssd_in_proj")(
            x, g.reshape(1, D_MODEL), w, w_dt, p["conv_w"], p["conv_b"], c0)


def _route(lg):
    lane = lax.broadcasted_iota(I32, lg.shape, 1)
    is_g = lane < N_GROUPS
    gl = jnp.where(is_g, lg, NEG)
    gmax = jnp.max(gl, axis=-1, keepdims=True)
    gsum = jnp.sum(jnp.where(is_g, jnp.exp(gl - gmax), 0.0), axis=-1, keepdims=True)
    gate = 1.0 / gsum
    gidx = jnp.min(jnp.where(gl == gmax, lane, LANES), axis=-1, keepdims=True)
    ecol = lane - N_GROUPS
    in_group = (ecol >= 0) & (ecol < N_EXPERTS) & ((ecol >> 3) == gidx)
    el = jnp.where(in_group, lg, NEG)
    v1 = jnp.max(el, axis=-1, keepdims=True)
    i1 = jnp.min(jnp.where(el == v1, lane, LANES), axis=-1, keepdims=True)
    el2 = jnp.where(lane == i1, NEG, el)
    v2 = jnp.max(el2, axis=-1, keepdims=True)
    i2 = jnp.min(jnp.where(el2 == v2, lane, LANES), axis=-1, keepdims=True)
    e2 = jnp.exp(v2 - v1)
    w1 = gate / (1.0 + e2)
    w2 = gate * e2 / (1.0 + e2)
    eid = jnp.where(lane == 0, i1 - N_GROUPS, jnp.where(lane == 1, i2 - N_GROUPS, 0))
    wts = jnp.where(lane == 0, w1, jnp.where(lane == 1, w2, 0.0))
    return eid, wts


def _mm_route_kernel(a_ref, w_ref, h_ref, g_ref, wr_ref, br_ref,
                     h1_ref, xn_ref, e_ref, p_ref, acc_ref, *, nk, tm):
    k = pl.program_id(1)
    rc = _row_chunk(tm)

    @pl.when(k == 0)
    def _():
        acc_ref[...] = jnp.zeros_like(acc_ref)

    acc_ref[...] += jnp.dot(a_ref[...], w_ref[...], preferred_element_type=F32)

    @pl.when(k == nk - 1)
    def _():
        def body(r, carry):
            rows = pl.ds(pl.multiple_of(r * rc, rc), rc)
            h1 = h_ref[rows, :] + acc_ref[rows, :]
            h1_ref[rows, :] = h1
            xn = _rms(h1, g_ref[...])
            xn_ref[rows, :] = xn
            lg = jnp.dot(xn.astype(BF16), wr_ref[...], preferred_element_type=F32) + br_ref[...]
            eid, wts = _route(lg)
            e_ref[rows, :] = eid
            p_ref[rows, :] = wts
            return carry

        lax.fori_loop(0, tm // rc, body, 0)


def matmul_route(a, w, h, g, wr, br, *, tm, tk):
    n_rows, kdim = a.shape
    assert n_rows % tm == 0 and kdim % tk == 0
    nk = kdim // tk
    row_spec = pl.BlockSpec((tm, D_MODEL), lambda i, k: (i, 0))
    lane_spec = pl.BlockSpec((tm, LANES), lambda i, k: (i, 0))
    return pl.pallas_call(
        functools.partial(_mm_route_kernel, nk=nk, tm=tm), grid=(n_rows // tm, nk),
        in_specs=[pl.BlockSpec((tm, tk), lambda i, k: (i, k)),
                  _resident_spec((tk, D_MODEL), lambda i, k: (k, 0), resident=nk == 1),
                  row_spec,
                  pl.BlockSpec((1, D_MODEL), lambda i, k: (0, 0)),
                  pl.BlockSpec((D_MODEL, LANES), lambda i, k: (0, 0)),
                  pl.BlockSpec((1, LANES), lambda i, k: (0, 0))],
        out_specs=[row_spec, row_spec, lane_spec, lane_spec],
        out_shape=[jax.ShapeDtypeStruct((n_rows, D_MODEL), F32),
                   jax.ShapeDtypeStruct((n_rows, D_MODEL), F32),
                   jax.ShapeDtypeStruct((n_rows, LANES), I32),
                   jax.ShapeDtypeStruct((n_rows, LANES), F32)],
        scratch_shapes=[pltpu.VMEM((tm, D_MODEL), F32)],
        compiler_params=_cparams(("parallel", "arbitrary")), name="matmul_route")(
            a, w, h, g.reshape(1, D_MODEL), wr, br)


def _expand_heads(v, eg):
    v1 = v.astype(BF16).astype(F32)
    r1 = v - v1
    v2 = r1.astype(BF16).astype(F32)
    v3 = (r1 - v2).astype(BF16).astype(F32)
    parts = jnp.concatenate([v1, v2, v3, jnp.zeros_like(v)], axis=0).astype(BF16)
    return lax.dot_general(parts, eg, (((0,), (0,)), ((), ())), preferred_element_type=F32)


def _ssd_kernel(z_ref, x_ref, b_ref, c_ref, dt_ref, cw_ref, cb_ref, dtb_ref, alog_ref, dsk_ref,
                nw_ref, eg_ref, s0_ref, c0_ref, y_ref, so_ref,
                state_ref, xp_ref, cumt_ref, dtt_ref, *, valid_len, n_chunks, conv_done):
    blk = SSD_BLOCK
    c = pl.program_id(1)

    @pl.when(c == 0)
    def _():
        state_ref[...] = s0_ref[0].T
        if not conv_done:
            xp_ref[0:8, :] = c0_ref[0]

    if not conv_done:
        @pl.when(c > 0)
        def _():
            xp_ref[0:8, :] = xp_ref[blk:blk + 8, :]

        xp_ref[8:8 + blk, 0:D_INNER] = x_ref[...]
        xp_ref[8:8 + blk, D_INNER:D_INNER + GN] = b_ref[...]
        xp_ref[8:8 + blk, D_INNER + GN:CONV_DIM] = c_ref[...]

    row = lax.broadcasted_iota(I32, (blk, LANES), 0)
    lane = lax.broadcasted_iota(I32, (blk, LANES), 1)
    dtr = dt_ref[...] + dtb_ref[...]
    dt = jnp.maximum(dtr, 0.0) + jnp.log1p(jnp.exp(-jnp.abs(dtr)))
    if valid_len < blk:
        dt = jnp.where(row < valid_len, dt, 0.0)
    cum = dt * (-jnp.exp(alog_ref[...]))
    sh = 1
    while sh < blk:
        cum = cum + jnp.where(row >= sh, pltpu.roll(cum, sh, 0), 0.0)
        sh *= 2
    cumt_ref[...] = cum.T
    dtt_ref[...] = dt.T

    causal = lane <= row
    left = lane < SSD_HEAD_DIM

    def conv(cols):
        acc = cb_ref[:, cols]
        for k in range(SSD_CONV):
            acc = acc + xp_ref[pl.ds(8 - (SSD_CONV - 1) + k, blk), cols] * cw_ref[k:k + 1, cols]
        return acc * _sigmoid(acc)

    def group(g, carry):
        cols = pl.ds(pl.multiple_of(g * GROUP_W, GROUP_W), GROUP_W)
        if conv_done:
            ncols = pl.ds(pl.multiple_of(g * SSD_STATE, SSD_STATE), SSD_STATE)
            xg, bg, cg = x_ref[:, cols], b_ref[:, ncols], c_ref[:, ncols]
        else:
            xg = conv(cols)
            bg = conv(pl.ds(pl.multiple_of(D_INNER + g * SSD_STATE, SSD_STATE), SSD_STATE))
            cg = conv(pl.ds(pl.multiple_of(D_INNER + GN + g * SSD_STATE, SSD_STATE), SSD_STATE))
        cb16, bb16 = cg.astype(BF16), bg.astype(BF16)
        cbm = lax.dot_general(cb16, bb16, (((1,), (1,)), ((), ())), preferred_element_type=F32)
        sg = state_ref[:, cols]
        heads = pl.ds(pl.multiple_of(g * (GROUP_W // SSD_HEAD_DIM), 8), GROUP_W // SSD_HEAD_DIM)
        ecg = _expand_heads(cumt_ref[heads, :], eg_ref[...])
        edg = _expand_heads(dtt_ref[heads, :], eg_ref[...])
        y_off =jnp.dot(cb16, sg.astype(BF16), preferred_element_type=F32) * jnp.exp(ecg)
        parts = []
        for pr in range(GROUP_W // LANES):
            xpair = xg[:, pr * LANES:(pr + 1) * LANES]
            ec = ecg[:, pr * LANES:(pr + 1) * LANES]
            ec_sw = pltpu.roll(ec, SSD_HEAD_DIM, 1)
            acc = jnp.zeros((blk, LANES), F32)
            for q in range(2):
                head = g * (GROUP_W // SSD_HEAD_DIM) + pr * 2 + q
                cum_l = jnp.where(left, ec, ec_sw) if q == 0 else jnp.where(left, ec_sw, ec)
                cum_s = cumt_ref[pl.ds(head, 1), :]
                decay = jnp.exp(jnp.where(causal, cum_l - cum_s, NEG))
                wmat = (decay * cbm * dtt_ref[pl.ds(head, 1), :]).astype(BF16)
                xm = jnp.where(left if q == 0 else jnp.logical_not(left), xpair, 0.0).astype(BF16)
                acc = acc + jnp.dot(wmat, xm, preferred_element_type=F32)
            parts.append(acc)
        y = jnp.concatenate(parts, axis=1) + y_off
        y = y + xg * dsk_ref[:, cols]
        zg = z_ref[:, cols]
        y = y * (zg * _sigmoid(zg))
        y = y * lax.rsqrt(jnp.mean(y * y, axis=-1, keepdims=True) + EPS) * nw_ref[:, cols]
        y_ref[:, cols] = y.astype(BF16)
        cl = ecg[blk - 1:blk, :]
        xt = (xg * (jnp.exp(cl - ecg) * edg)).astype(BF16)
        h_in = lax.dot_general(bb16, xt, (((0,), (0,)), ((), ())), preferred_element_type=F32)
        state_ref[:, cols] = sg * jnp.exp(cl) + h_in
        return carry

    lax.fori_loop(0, SSD_GROUPS, group, 0, unroll=4)

    @pl.when(c == n_chunks - 1)
    def _():
        so_ref[0] = state_ref[...].T


def ssd_scan(zx, dt, p, s0, c0, *, n_batch, n_chunks, valid_len, shared_init, conv_done=False):
    blk = SSD_BLOCK
    n_rows = n_batch * n_chunks * blk
    assert zx.shape == (n_rows, ZX_DIM)
    rowi = lambda b, c: b * n_chunks + c
    init = (lambda b, c: (0, 0, 0)) if shared_init else (lambda b, c: (b, 0, 0))
    const = lambda b, c: (0, 0)
    in_specs = [
        pl.BlockSpec((blk, D_INNER), lambda b, c: (rowi(b, c), 0)),
        pl.BlockSpec((blk, D_INNER), lambda b, c: (rowi(b, c), 1)),
        pl.BlockSpec((blk, GN), lambda b, c: (rowi(b, c), 2 * D_INNER // GN)),
        pl.BlockSpec((blk, GN), lambda b, c: (rowi(b, c), 2 * D_INNER // GN + 1)),
        pl.BlockSpec((blk, LANES), lambda b, c: (rowi(b, c), 0)),
        pl.BlockSpec((8, CONV_DIM), const),
        pl.BlockSpec((1, CONV_DIM), const),
        pl.BlockSpec((1, LANES), const),
        pl.BlockSpec((1, LANES), const),
        pl.BlockSpec((1, D_INNER), const),
        pl.BlockSpec((1, D_INNER), const),
        pl.BlockSpec((32, GROUP_W), const),
        pl.BlockSpec((1, D_INNER, SSD_STATE), init),
        pl.BlockSpec((1, 8, CONV_DIM), init),
    ]
    out_specs = [pl.BlockSpec((blk, D_INNER), lambda b, c: (rowi(b, c), 0)),
                 pl.BlockSpec((1, D_INNER, SSD_STATE), lambda b, c: (b, 0, 0))]
    out_shape = [jax.ShapeDtypeStruct((n_rows, D_INNER), BF16),
                 jax.ShapeDtypeStruct((n_batch, D_INNER, SSD_STATE), F32)]
    scratch = [pltpu.VMEM((SSD_STATE, D_INNER), F32),
               pltpu.VMEM((8, LANES) if conv_done else (blk + 8, CONV_DIM), F32),
               pltpu.VMEM((LANES, blk), F32),
               pltpu.VMEM((LANES, blk), F32)]
    kern = functools.partial(_ssd_kernel, valid_len=valid_len, n_chunks=n_chunks, conv_done=conv_done)
    return pl.pallas_call(
        kern, grid=(n_batch, n_chunks), in_specs=in_specs, out_specs=out_specs,
        out_shape=out_shape, scratch_shapes=scratch,
        compiler_params=_cparams(("parallel", "arbitrary")), name="ssd_scan")(
            zx, zx, zx, zx, dt, p["conv_w"], p["conv_b"], p["dt_bias"], p["a_log"], p["d_skip"],
            p["norm_w"], p["expand"], s0, c0)


def _attn_core(items, sink_ref):
    n_q = items[0][0].shape[0]
    scale = ATTN_HEAD_DIM ** -0.5
    left = lax.broadcasted_iota(I32, (n_q, LANES), 1) < ATTN_HEAD_DIM
    nt = (((1,), (1,)), ((), ()))
    n_pairs = KV_DIM // LANES
    kps = [[kk[:, pr * LANES:(pr + 1) * LANES].astype(BF16) for pr in range(n_pairs)] for _, kk, _, _ in items]
    vps = [[vv[:, pr * LANES:(pr + 1) * LANES].astype(BF16) for pr in range(n_pairs)] for _, _, vv, _ in items]

    def scores(t, h):
        q, kmask = items[t][0], items[t][3]
        blk, half = divmod(h, 2)
        qb = q[:, blk * LANES:(blk + 1) * LANES]
        qh = jnp.where(left if half == 0 else jnp.logical_not(left), qb, 0.0).astype(BF16)
        s = lax.dot_general(qh, kps[t][blk // Q_PER_KV], nt, preferred_element_type=F32) * scale
        return s if kmask is None else jnp.where(kmask, s, NEG)

    def attend(t, h, s):
        sink = sink_ref[h]
        m = jnp.maximum(jnp.max(s, axis=-1, keepdims=True), sink)
        p = jnp.exp(s - m)
        den = jnp.sum(p, axis=-1, keepdims=True) + jnp.exp(sink - m)
        p = p * (1.0 / den)
        return jnp.dot(p.astype(BF16), vps[t][h // (2 * Q_PER_KV)], preferred_element_type=F32)

    work = [(t, h) for t in range(len(items)) for h in range(N_HEADS)]
    pending, done = {}, {}
    for n in range(len(work) + ATTN_LAG):
        if n < len(work):
            pending[work[n]] = scores(*work[n])
        if n >= ATTN_LAG:
            key = work[n - ATTN_LAG]
            done[key] = attend(*key, pending.pop(key))
    return [jnp.concatenate([jnp.where(left, done[(t, 2 * b)], done[(t, 2 * b + 1)])
                             for b in range(N_HEADS // 2)], axis=1) for t in range(len(items))]


def _attn_band_kernel(sink_ref, q_ref, k_ref, v_ref, mk_ref, mv_ref, o_ref):
    band = WINDOW + CHUNK
    items = []
    for sub in range(ATTN_CHUNKS):
        c = pl.program_id(1) * ATTN_CHUNKS + sub
        first = jnp.maximum(c - WINDOW // CHUNK, 0)
        start = pl.multiple_of(first * CHUNK, CHUNK)
        kk = jnp.concatenate([mk_ref[...], k_ref[pl.ds(start, band), :]], axis=0)
        vv = jnp.concatenate([mv_ref[...], v_ref[pl.ds(start, band), :]], axis=0)
        j = lax.broadcasted_iota(I32, (1, N_META + band), 1)
        key_chunk = first + jnp.maximum(j - N_META, 0) // CHUNK
        kmask = (j < N_META) | (key_chunk <= c)
        items.append((q_ref[sub * CHUNK:(sub + 1) * CHUNK, :], kk, vv, kmask))
    for sub, o in enumerate(_attn_core(items, sink_ref)):
        o_ref[sub * CHUNK:(sub + 1) * CHUNK, :] = o.astype(BF16)


def attn_band(qkv, meta_k, meta_v, sinks, *, n_batch, seq):
    rows = ATTN_CHUNKS * CHUNK
    n_chunks = seq // rows
    assert seq >= WINDOW + CHUNK and seq % rows == 0
    kcol = Q_DIM // KV_DIM
    grid_spec = pltpu.PrefetchScalarGridSpec(
        num_scalar_prefetch=1, grid=(n_batch, n_chunks),
        in_specs=[pl.BlockSpec((rows, Q_DIM), lambda b, c, s: (b * n_chunks + c, 0)),
                  pl.BlockSpec((seq, KV_DIM), lambda b, c, s: (b, kcol)),
                  pl.BlockSpec((seq, KV_DIM), lambda b, c, s: (b, kcol + 1)),
                  pl.BlockSpec((N_META, KV_DIM), lambda b, c, s: (0, 0)),
                  pl.BlockSpec((N_META, KV_DIM), lambda b, c, s: (0, 0))],
        out_specs=pl.BlockSpec((rows, Q_DIM), lambda b, c, s: (b * n_chunks + c, 0)))
    return pl.pallas_call(
        _attn_band_kernel, grid_spec=grid_spec,
        out_shape=jax.ShapeDtypeStruct((n_batch * seq, Q_DIM), BF16),
        compiler_params=_cparams(("parallel", "arbitrary")), name="attn_band")(
            sinks, qkv, qkv, qkv, meta_k, meta_v)


def _attn_full_kernel(sink_ref, q_ref, k_ref, v_ref, o_ref):
    o_ref[...] = _attn_core([(q_ref[...], k_ref[...], v_ref[...], None)], sink_ref)[0].astype(BF16)


def attn_full(q, kk, vv, sinks):
    n_batch, n_q, _ = q.shape
    n_k = kk.shape[1]
    grid_spec = pltpu.PrefetchScalarGridSpec(
        num_scalar_prefetch=1, grid=(n_batch,),
        in_specs=[pl.BlockSpec((None, n_q, Q_DIM), lambda b, s: (b, 0, 0)),
                  pl.BlockSpec((None, n_k, KV_DIM), lambda b, s: (b, 0, 0)),
                  pl.BlockSpec((None, n_k, KV_DIM), lambda b, s: (b, 0, 0))],
        out_specs=pl.BlockSpec((None, n_q, Q_DIM), lambda b, s: (b, 0, 0)))
    return pl.pallas_call(
        _attn_full_kernel, grid_spec=grid_spec,
        out_shape=jax.ShapeDtypeStruct((n_batch, n_q, Q_DIM), BF16),
        compiler_params=_cparams(("parallel",)), name="attn_full")(sinks, q, kk, vv)


def _expert_ffn(x16, w_in, w_out):
    gu = jnp.dot(x16, w_in, preferred_element_type=F32)
    a, b = gu[:, :D_EXPERT], gu[:, D_EXPERT:]
    hmid = a * _sigmoid(a) * b
    return jnp.dot(hmid.astype(BF16), w_out, preferred_element_type=F32)


def _moe_dense_kernel(x_ref, e_ref, p_ref, wi_ref, wo_ref, o_ref):
    e = pl.program_id(0)

    @pl.when(e == 0)
    def _():
        o_ref[...] = jnp.zeros_like(o_ref)

    eid, wts = e_ref[...], p_ref[...]
    sel0, sel1 = eid[:, 0:1] == e, eid[:, 1:2] == e
    coef = jnp.where(sel0, wts[:, 0:1], jnp.where(sel1, wts[:, 1:2], 0.0))
    y = _expert_ffn(x_ref[...].astype(BF16), wi_ref[0].astype(BF16), wo_ref[0].astype(BF16))
    o_ref[...] += jnp.where(sel0 | sel1, y * coef, 0.0)


def moe_dense(xn, eid, wts, w_in, w_out, *, layer):
    n_rows = xn.shape[0]
    full = lambda e: (0, 0)
    return pl.pallas_call(
        _moe_dense_kernel, grid=(N_EXPERTS,),
        in_specs=[pl.BlockSpec((n_rows, D_MODEL), full), pl.BlockSpec((n_rows, LANES), full),
                  pl.BlockSpec((n_rows, LANES), full),
                  pl.BlockSpec((None, 1, D_MODEL, 2 * D_EXPERT), lambda e: (layer, e, 0, 0)),
                  pl.BlockSpec((None, 1, D_EXPERT, D_MODEL), lambda e: (layer, e, 0, 0))],
        out_specs=pl.BlockSpec((n_rows, D_MODEL), full),
        out_shape=jax.ShapeDtypeStruct((n_rows, D_MODEL), F32),
        compiler_params=_cparams(("arbitrary",)), name="moe_dense")(xn, eid, wts, w_in, w_out)


def _moe_sorted_kernel(blk_e_ref, n_used_ref, src_ref, src_next_ref, src_ahead_ref, dst_prev_ref, dst_ref,
                       x_hbm, wi_ref, wo_ref, y_hbm, xbuf, obuf, x16_ref, hmid_ref, wi16_ref, wo16_ref,
                       gsem, ssem, *, tm, n_tok):
    i = pl.program_id(0)
    n_used = n_used_ref[0]
    nc = MOE_NC

    def start_gathers(rows_ref, s, lo=0, hi=tm):
        for r in range(lo, hi):
            pltpu.make_async_copy(x_hbm.at[pl.ds(rows_ref[0, 0, r], 1), :], xbuf.at[s, pl.ds(r, 1), :],
                                  gsem.at[s]).start()

    def start_scatters(rows_ref, s, lo=0, hi=tm):
        for r in range(lo, hi):
            pltpu.make_async_copy(obuf.at[s, pl.ds(r, 1), :], y_hbm.at[pl.ds(rows_ref[0, 0, r], 1), :],
                                  ssem.at[s]).start()

    def wait_block(buf, sem, s):
        pltpu.make_async_copy(buf.at[s, pl.ds(0, tm)], buf.at[s, pl.ds(0, tm)], sem.at[s]).wait()

    @pl.when(i == 0)
    def _():
        xbuf[...] = jnp.zeros_like(xbuf)
        obuf[...] = jnp.zeros_like(obuf)
        spare = [pltpu.make_async_copy(obuf.at[s], y_hbm.at[pl.ds(2 * n_tok + s * tm, tm), :], ssem.at[s])
                 for s in range(MOE_SLOTS)]
        for cp in spare:
            cp.start()
        for cp in spare:
            cp.wait()
        start_gathers(src_ref, 0)
        start_gathers(src_next_ref, 1)

    def step(slot):
        ahead, after = (slot + 2) % MOE_SLOTS, (slot + 1) % MOE_SLOTS
        wait_block(xbuf, gsem, slot)

        @pl.when(i >= 2)
        def _():
            wait_block(obuf, ssem, slot)

        n_a, n_b = 2 * D_EXPERT // nc, D_MODEL // nc
        t_a, t_b = D_MODEL // nc, D_EXPERT // nc
        first = 8
        unit = (2 * tm - first) / (n_a * t_a + n_b * t_b)
        quota = [first] + [round(unit * t_a)] * n_a + [round(unit * t_b)] * (n_b - 1)
        quota.append(2 * tm - sum(quota))
        issued = [0]

        def issue(lhs_ref):
            lo, hi = issued[0], issued[0] + quota.pop(0)
            issued[0] = hi
            start_scatters(dst_prev_ref, ahead, (lo + 1) // 2, (hi + 1) // 2)
            start_gathers(src_ahead_ref, ahead, lo // 2, hi // 2)
            if lhs_ref is not None:
                zero = xbuf[ahead, tm:tm + 8, 0:LANES]
                zero = jnp.concatenate([zero, zero], axis=0)
                lhs_ref[0:16, 0:LANES] = (lhs_ref[0:16, 0:LANES].astype(F32) + zero).astype(BF16)

        x16_ref[...] = xbuf[slot, 0:tm].astype(BF16)
        for k in range(D_EXPERT // nc):
            issue(x16_ref)
            a = jnp.dot(x16_ref[...], wi16_ref[:, k * nc:(k + 1) * nc], preferred_element_type=F32)
            issue(x16_ref)
            b = jnp.dot(x16_ref[...], wi16_ref[:, D_EXPERT + k * nc:D_EXPERT + (k + 1) * nc],
                        preferred_element_type=F32)
            hmid_ref[:, k * nc:(k + 1) * nc] = (a * _sigmoid(a) * b).astype(BF16)
        for k in range(n_b):
            issue(hmid_ref)
            obuf[slot, :, k * nc:(k + 1) * nc] = jnp.dot(hmid_ref[...], wo16_ref[:, k * nc:(k + 1) * nc],
                                                         preferred_element_type=F32)
        issue(None)
        assert not quota and issued[0] == 2 * tm

        @pl.when(i == n_used - 1)
        def _():
            start_scatters(dst_ref, slot)

            @pl.when(i >= 1)
            def _():
                wait_block(obuf, ssem, after)
            wait_block(obuf, ssem, ahead)
            wait_block(obuf, ssem, slot)
            wait_block(xbuf, gsem, after)
            wait_block(xbuf, gsem, ahead)

    @pl.when((i < n_used) & ((i == 0) | (blk_e_ref[i] != blk_e_ref[jnp.maximum(i - 1, 0)])))
    def _():
        wi16_ref[...] = wi_ref[0].astype(BF16)
        wo16_ref[...] = wo_ref[0].astype(BF16)

    for parity in range(MOE_SLOTS):
        pl.when((i < n_used) & (i % MOE_SLOTS == parity))(functools.partial(step, parity))


def moe_sorted(xn, w_in, w_out, tables, *, tm, layer):
    blk_e, n_used, src, dst = tables
    n_tok = xn.shape[0]
    n_blocks = blk_e.shape[0]
    smem = lambda index_map: pl.BlockSpec((1, 1, tm), index_map, memory_space=pltpu.SMEM)
    later = lambda k: (lambda i, be, nu: (jnp.maximum(jnp.minimum(i + k, nu[0] - 1), 0), 0, 0))
    grid_spec = pltpu.PrefetchScalarGridSpec(
        num_scalar_prefetch=2, grid=(n_blocks,),
        in_specs=[smem(lambda i, be, nu: (i, 0, 0)), smem(later(1)), smem(later(2)),
                  smem(lambda i, be, nu: (i, 0, 0)), smem(lambda i, be, nu: (i + 1, 0, 0)),
                  pl.BlockSpec(memory_space=pl.ANY),
                  pl.BlockSpec((None, 1, D_MODEL, 2 * D_EXPERT), lambda i, be, nu: (layer, be[i], 0, 0)),
                  pl.BlockSpec((None, 1, D_EXPERT, D_MODEL), lambda i, be, nu: (layer, be[i], 0, 0))],
        out_specs=pl.BlockSpec(memory_space=pl.ANY),
        scratch_shapes=[pltpu.VMEM((MOE_SLOTS, tm + 8, D_MODEL), F32), pltpu.VMEM((MOE_SLOTS, tm, D_MODEL), F32),
                        pltpu.VMEM((tm, D_MODEL), BF16), pltpu.VMEM((tm, D_EXPERT), BF16),
                        pltpu.VMEM((D_MODEL, 2 * D_EXPERT), BF16), pltpu.VMEM((D_EXPERT, D_MODEL), BF16),
                        pltpu.SemaphoreType.DMA((MOE_SLOTS,)), pltpu.SemaphoreType.DMA((MOE_SLOTS,))])
    return pl.pallas_call(
        functools.partial(_moe_sorted_kernel, tm=tm, n_tok=n_tok), grid_spec=grid_spec,
        out_shape=jax.ShapeDtypeStruct((2 * n_tok + MOE_SLOTS * tm, D_MODEL), F32),
        compiler_params=_cparams(("arbitrary",)), name="moe_sorted")(
            blk_e, n_used, src, src, src, dst, dst, xn, w_in, w_out)


def route_tables(eid, *, tm):
    n_tok = eid.shape[0]
    n_asg = 2 * n_tok
    flat = eid.reshape(-1)
    skey = jnp.sort(flat * n_asg + jnp.arange(n_asg, dtype=I32))
    order = skey % n_asg
    counts = jnp.sum((flat[:, None] == jnp.arange(N_EXPERTS, dtype=I32)[None, :]).astype(I32), axis=0)
    starts = jnp.cumsum(counts) - counts
    padded = (counts + tm - 1) // tm * tm
    pend = jnp.cumsum(padded)
    pstarts = pend - padded
    n_blocks = -(-(n_asg + N_EXPERTS * (tm - 1)) // tm)
    blk = jnp.arange(n_blocks, dtype=I32)
    ends = jnp.zeros((n_blocks + 1,), I32).at[jnp.minimum(pend // tm, n_blocks)].add(1)
    blk_e = jnp.minimum(jnp.cumsum(ends)[:n_blocks], N_EXPERTS - 1)
    blk_off = blk * tm - pstarts[blk_e]
    blk_n = jnp.clip(counts[blk_e] - blk_off, 0, tm)
    r = jnp.arange(tm, dtype=I32)[None, :]
    valid = r < blk_n[:, None]
    code = order[jnp.clip((starts[blk_e] + blk_off)[:, None] + r, 0, n_asg - 1)]
    src = jnp.where(valid, code >> 1, 0)
    spare = 2 * n_tok + (blk % MOE_SLOTS)[:, None] * tm + r
    dst = jnp.where(valid, (code & 1) * n_tok + (code >> 1), spare)
    dst = jnp.concatenate([2 * n_tok + (MOE_SLOTS - 1) * tm + r, dst], axis=0)
    n_used = (pend[-1] // tm).reshape(1)
    return (blk_e.astype(I32), n_used.astype(I32), src.reshape(n_blocks, 1, tm).astype(I32),
            dst.reshape(n_blocks + 1, 1, tm).astype(I32))


def _q_head_order():
    order = []
    for pr in range(N_KV_HEADS // 2):
        for r in range(Q_PER_KV):
            for half in range(2):
                order.append((2 * pr + half) * Q_PER_KV + r)
    return jnp.asarray(order, dtype=I32)


def _pad_lanes(a):
    return jnp.pad(a, [(0, 0)] * (a.ndim - 1) + [(0, LANES - a.shape[-1])])


def _ssd_params(conv_w, conv_b, dt_bias, a_log, d_skip, norm_w):
    head_of_col = jnp.arange(GROUP_W, dtype=I32) // SSD_HEAD_DIM
    row = jnp.arange(32, dtype=I32)
    expand = ((row[:, None] % 8 == head_of_col[None, :]) & (row[:, None] < 24)).astype(BF16)
    return {
        "conv_w": jnp.pad(conv_w, ((0, 8 - SSD_CONV), (0, 0))),
        "conv_b": conv_b.reshape(1, CONV_DIM),
        "dt_bias": _pad_lanes(dt_bias.reshape(1, SSD_HEADS)),
        "a_log": _pad_lanes(a_log.reshape(1, SSD_HEADS)),
        "d_skip": jnp.repeat(d_skip, SSD_HEAD_DIM).reshape(1, D_INNER),
        "norm_w": norm_w.reshape(1, D_INNER),
        "expand": expand,
    }


def _router_params(w_group, b_group, w_router, b_router):
    wr = _pad_lanes(jnp.concatenate([w_group, w_router], axis=1)).astype(BF16)
    br = _pad_lanes(jnp.concatenate([b_group, b_router]).reshape(1, -1)).astype(F32)
    return wr, br


def _pick_tile(n, prefs):
    for t in prefs:
        if n % t == 0:
            return t
    return n


def kernel(x_prompt, x_sample, cache_ssm_state, cache_ssm_conv, cache_attn_k, cache_attn_v, cache_meta_k, cache_meta_v, meta_tokens, norm_mix, norm_ffn, norm_final, ssd_w_in, ssd_conv_w, ssd_conv_b, ssd_dt_bias, ssd_a_log, ssd_d, ssd_norm, ssd_w_out, attn_w_qkv, attn_sinks, attn_w_out, moe_w_group, moe_b_group, moe_w_router, moe_b_router, moe_w_in, moe_w_out):
    n_batch, seq, _ = x_prompt.shape
    dec_batch, dec_seq, _ = x_sample.shape
    n_real = n_batch * seq
    n_small = N_META + dec_batch * dec_seq
    blk = SSD_BLOCK
    assert seq % blk == 0 and dec_seq <= blk and N_META <= blk
    tm_mid = _pick_tile(n_real, (512, 256))
    tm_res = _pick_tile(n_real, (256,))

    h_real = x_prompt.reshape(n_real, D_MODEL)
    h_small = jnp.concatenate([meta_tokens, x_sample.reshape(-1, D_MODEL)], axis=0)

    w_in16 = ssd_w_in[0].astype(BF16)
    w_zx, w_dt = w_in16[:, :ZX_DIM], _pad_lanes(w_in16[:, ZX_DIM:])
    sp = _ssd_params(ssd_conv_w[0], ssd_conv_b[0], ssd_dt_bias[0], ssd_a_log[0], ssd_d[0], ssd_norm[0])
    w_out16 = ssd_w_out[0].astype(BF16)
    wr0, br0 = _router_params(moe_w_group[0], moe_b_group[0], moe_w_router[0], moe_b_router[0])

    zx_s, dt_s = norm_matmul(h_small, norm_mix[0], w_zx, tm=n_small, tn=512, w_extra=w_dt)

    n_sb = 1 + dec_batch

    def to_blocks(a, n_valid_meta=N_META):
        meta = jnp.pad(a[:N_META], ((0, blk - N_META), (0, 0)))[None]
        samp = jnp.pad(a[N_META:].reshape(dec_batch, dec_seq, -1), ((0, 0), (0, blk - dec_seq), (0, 0)))
        return jnp.concatenate([meta, samp], axis=0).reshape(n_sb * blk, -1)

    assert dec_seq == N_META, "small-path blocks share one valid length"
    s0_small = jnp.concatenate([jnp.zeros((1, D_INNER, SSD_STATE), F32),
                                cache_ssm_state[0].reshape(dec_batch, D_INNER, SSD_STATE)], axis=0)
    c0_small = jnp.pad(jnp.concatenate([jnp.zeros((1, SSD_CONV - 1, CONV_DIM), F32), cache_ssm_conv[0]], axis=0),
                       ((0, 0), (8 - (SSD_CONV - 1), 0), (0, 0)))
    yn_sb, state_small = ssd_scan(to_blocks(zx_s), to_blocks(dt_s), sp, s0_small, c0_small,
                                  n_batch=n_sb, n_chunks=1, valid_len=dec_seq, shared_init=False)
    yn_sb = yn_sb.reshape(n_sb, blk, D_INNER)
    yn_small = jnp.concatenate([yn_sb[0, :N_META], yn_sb[1:, :dec_seq].reshape(-1, D_INNER)], axis=0)

    c0_real = jnp.pad(zx_s[N_META - (SSD_CONV - 1):N_META, D_INNER:], ((8 - (SSD_CONV - 1), 0), (0, 0)))
    tm_in = _pick_tile(seq, (1024, 512, 256, 128))
    zx_r, dt_r, raw_tail = ssd_in_proj(h_real, norm_mix[0], w_zx, w_dt, sp, c0_real, tm=tm_in, tn=1024, seq=seq)
    yn_real, state_real = ssd_scan(zx_r, dt_r, sp, state_small[0:1], c0_real[None],
                                   n_batch=n_batch, n_chunks=seq // blk, valid_len=blk, shared_init=True,
                                   conv_done=True)
    conv_real = raw_tail.reshape(n_batch, seq // tm_in, 8, CONV_DIM)[:, -1, 8 - (SSD_CONV - 1):]

    tk0 = 2048
    h1_s, xn_s, eid_s, wts_s = matmul_route(yn_small, w_out16, h_small, norm_ffn[0], wr0, br0, tm=n_small, tk=tk0)
    h1_r, xn_r, eid_r, wts_r = matmul_route(yn_real, w_out16, h_real, norm_ffn[0], wr0, br0, tm=tm_res,
                                            tk=D_INNER)

    moe_s = moe_dense(xn_s, eid_s, wts_s, moe_w_in, moe_w_out, layer=0)
    moe_r = moe_sorted(xn_r, moe_w_in, moe_w_out, route_tables(eid_r[:, :2], tm=MOE_TM), tm=MOE_TM, layer=0)

    head_order = _q_head_order()
    wq = attn_w_qkv[0][:, :Q_DIM].reshape(D_MODEL, N_HEADS, ATTN_HEAD_DIM)[:, head_order].reshape(D_MODEL, Q_DIM)
    w_qkv16 = jnp.concatenate([wq, attn_w_qkv[0][:, Q_DIM:]], axis=1).astype(BF16)
    sinks = attn_sinks[0][head_order].astype(F32)
    w_ao16 = attn_w_out[0].reshape(N_HEADS, ATTN_HEAD_DIM, D_MODEL)[head_order].reshape(Q_DIM, D_MODEL).astype(BF16)
    wr1, br1 = _router_params(moe_w_group[1], moe_b_group[1], moe_w_router[1], moe_b_router[1])

    qkv_s, h2_s = norm_matmul(h1_s, norm_mix[1], w_qkv16, tm=n_small, tn=512, mode="sum", moe=(moe_s,),
                              write_h=True)
    qkv_r, h2_r = norm_matmul(h1_r, norm_mix[1], w_qkv16, tm=tm_res, tn=Q_DIM + 2 * KV_DIM, mode="pair",
                              moe=(moe_r, wts_r), write_h=True)

    k_meta, v_meta = qkv_s[:N_META, Q_DIM:Q_DIM + KV_DIM], qkv_s[:N_META, Q_DIM + KV_DIM:]
    k_new = qkv_s[N_META:, Q_DIM:Q_DIM + KV_DIM].reshape(dec_batch, dec_seq, KV_DIM)
    v_new = qkv_s[N_META:, Q_DIM + KV_DIM:].reshape(dec_batch, dec_seq, KV_DIM)
    o_meta = attn_full(qkv_s[None, :N_META, :Q_DIM], k_meta[None], v_meta[None], sinks)
    kk_s = jnp.concatenate([cache_meta_k[0].reshape(dec_batch, N_META, KV_DIM),
                            cache_attn_k[0].reshape(dec_batch, WINDOW, KV_DIM), k_new], axis=1)
    vv_s = jnp.concatenate([cache_meta_v[0].reshape(dec_batch, N_META, KV_DIM),
                            cache_attn_v[0].reshape(dec_batch, WINDOW, KV_DIM), v_new], axis=1)
    o_samp = attn_full(qkv_s[N_META:, :Q_DIM].reshape(dec_batch, dec_seq, Q_DIM), kk_s, vv_s, sinks)
    o_small = jnp.concatenate([o_meta.reshape(N_META, Q_DIM), o_samp.reshape(-1, Q_DIM)], axis=0)
    o_real = attn_band(qkv_r, k_meta, v_meta, sinks, n_batch=n_batch, seq=seq)

    h3_s, xn_s, eid_s, wts_s = matmul_route(o_small, w_ao16, h2_s, norm_ffn[1], wr1, br1, tm=n_small, tk=1024)
    h3_r, xn_r, eid_r, wts_r = matmul_route(o_real, w_ao16, h2_r, norm_ffn[1], wr1, br1, tm=tm_mid, tk=2048)

    moe_s = moe_dense(xn_s, eid_s, wts_s, moe_w_in, moe_w_out, layer=1)
    moe_r = moe_sorted(xn_r, moe_w_in, moe_w_out, route_tables(eid_r[:, :2], tm=MOE_TM), tm=MOE_TM, layer=1)

    y_small = final_norm(h3_s, norm_final, tm=n_small, mode="sum", moe=(moe_s,))
    y_real = final_norm(h3_r, norm_final, tm=tm_mid, mode="pair", moe=(moe_r, wts_r))

    kv_shape = (N_KV_HEADS, ATTN_HEAD_DIM)
    qkv_r3 = qkv_r.reshape(n_batch, seq, -1)
    zx_samp = zx_s[N_META:].reshape(dec_batch, dec_seq, ZX_DIM)
    return (
        y_real.reshape(n_batch, seq, D_MODEL),
        y_small[N_META:].reshape(dec_batch, dec_seq, D_MODEL),
        state_real.reshape(1, n_batch, SSD_HEADS, SSD_HEAD_DIM, SSD_STATE),
        conv_real[None],
        qkv_r3[:, seq - WINDOW:, Q_DIM:Q_DIM + KV_DIM].reshape((1, n_batch, WINDOW) + kv_shape),
        qkv_r3[:, seq - WINDOW:, Q_DIM + KV_DIM:].reshape((1, n_batch, WINDOW) + kv_shape),
        jnp.broadcast_to(k_meta.reshape((1, 1, N_META) + kv_shape), (1, n_batch, N_META) + kv_shape),
        jnp.broadcast_to(v_meta.reshape((1, 1, N_META) + kv_shape), (1, n_batch, N_META) + kv_shape),
        state_small[1:].reshape(1, dec_batch, SSD_HEADS, SSD_HEAD_DIM, SSD_STATE),
        zx_samp[None, :, dec_seq - (SSD_CONV - 1):, D_INNER:],
        k_new.reshape((1, dec_batch, dec_seq) + kv_shape),
        v_new.reshape((1, dec_batch, dec_seq) + kv_shape),
    )
```
